```python
import math
import jax, jax.numpy as jnp
from jax import lax
import numpy as np

D_MODEL = 1024
BATCH = 16
SEQ = 256
DEPTH = 4
DEC_BATCH = 4
DEC_SEQ = 2048
PAST_LEN = 512

GRID_W = 64
HEAD_DIM = 64
N_HEADS = D_MODEL // HEAD_DIM
N_KV_HEADS = N_HEADS // 4
GQA_GROUP = N_HEADS // N_KV_HEADS
DIFF_HEADS = D_MODEL // (2 * HEAD_DIM)
NA_ROWS = 8
NA_COLS = 16
WINDOW = 128
Q_BLOCK = 128
ROPE_BASE = 10000.0
ROPE_PAIRS_AXIS = HEAD_DIM // 4
N_MIXERS = 4
DENSE_FF = 2816
N_EXPERTS = 8
TOP_K = 2
EXPERT_FF = 2048
QKV_GQA = (N_HEADS + 2 * N_KV_HEADS) * HEAD_DIM
EPS = 1e-6
NEG_INF = -1e30

kernel_name = "hybrid_diffusion_trunk_step"


def rms_norm(x, g):
    xf = x.astype(jnp.float32)
    y = xf * lax.rsqrt(jnp.mean(xf * xf, axis=-1, keepdims=True) + EPS)
    return (y * g.astype(jnp.float32)).astype(x.dtype)


def ada_mod(cond, w_ada, b_ada):
    m = jax.nn.silu(cond) @ w_ada + b_ada
    return [t[:, None, :] for t in jnp.split(m, 6, axis=-1)]


def modulate(h, shift, scale):
    return h * (1.0 + scale) + shift


def axial_rope_tables(T):
    t = jnp.arange(T, dtype=jnp.int32)
    row = (t // GRID_W).astype(jnp.float32)
    col = (t % GRID_W).astype(jnp.float32)
    freqs = ROPE_BASE ** (-jnp.arange(ROPE_PAIRS_AXIS, dtype=jnp.float32) / ROPE_PAIRS_AXIS)
    ang = jnp.concatenate([row[:, None] * freqs, col[:, None] * freqs], axis=-1)
    return jnp.cos(ang), jnp.sin(ang)


def apply_rope(x, cos, sin):
    shp = (x.shape[1],) + (1,) * (x.ndim - 3) + (HEAD_DIM // 2,)
    c = cos.reshape(shp)
    s = sin.reshape(shp)
    xf = x.astype(jnp.float32).reshape(x.shape[:-1] + (HEAD_DIM // 2, 2))
    x0, x1 = xf[..., 0], xf[..., 1]
    out = jnp.stack([x0 * c - x1 * s, x0 * s + x1 * c], axis=-1)
    return out.reshape(x.shape).astype(x.dtype)


def joint_softmax(parts, sink=None):
    sizes = [p.shape[-1] for p in parts]
    z = [p.astype(jnp.float32) for p in parts]
    if sink is not None:
        z.append(jnp.broadcast_to(sink.astype(jnp.float32), z[0].shape[:-1] + (1,)))
    p = jax.nn.softmax(jnp.concatenate(z, axis=-1), axis=-1)
    cuts = [int(v) for v in np.cumsum(sizes)]
    return jnp.split(p, cuts, axis=-1)[:len(parts)]


def gqa_logits(q, k):
    return jnp.einsum('bqkgd,bskd->bkgqs', q, k, preferred_element_type=jnp.float32) * (HEAD_DIM ** -0.5)


def gqa_values(p, v):
    return jnp.einsum('bkgqs,bskd->bqkgd', p.astype(v.dtype), v)


def map_query_blocks(fn, q):
    B, T = q.shape[:2]
    nb = T // Q_BLOCK
    qb = jnp.moveaxis(q.reshape((B, nb, Q_BLOCK) + q.shape[2:]), 1, 0)
    out = lax.map(lambda a: fn(a[0], a[1]), (jnp.arange(nb), qb))
    return jnp.moveaxis(out, 0, 1).reshape(B, T, D_MODEL)


def proj_mha(h, w_qkv, q_norm, k_norm):
    B, T, _ = h.shape
    qkv = (h @ w_qkv).reshape(B, T, 3, N_HEADS, HEAD_DIM)
    q = rms_norm(qkv[:, :, 0], q_norm)[:, :, :, None, :]
    return q, rms_norm(qkv[:, :, 1], k_norm), qkv[:, :, 2]


def proj_gqa(h, w_qkv, q_norm, k_norm):
    B, T, _ = h.shape
    qkv = h @ w_qkv
    nq = N_HEADS * HEAD_DIM
    nk = N_KV_HEADS * HEAD_DIM
    q = qkv[..., :nq].reshape(B, T, N_KV_HEADS, GQA_GROUP, HEAD_DIM)
    k = qkv[..., nq:nq + nk].reshape(B, T, N_KV_HEADS, HEAD_DIM)
    v = qkv[..., nq + nk:].reshape(B, T, N_KV_HEADS, HEAD_DIM)
    return rms_norm(q, q_norm), rms_norm(k, k_norm), v


def proj_diff(h, w_qkv, q_norm, k_norm):
    B, T, _ = h.shape
    qkv = (h @ w_qkv).reshape(B, T, 3, DIFF_HEADS, 2 * HEAD_DIM)
    q = rms_norm(qkv[:, :, 0].reshape(B, T, DIFF_HEADS, 2, HEAD_DIM), q_norm)
    k = rms_norm(qkv[:, :, 1].reshape(B, T, DIFF_HEADS, 2, HEAD_DIM), k_norm)
    return q, k, qkv[:, :, 2]


def natten_ctx(h, w_qkv, q_norm, k_norm):
    B, T, _ = h.shape
    q, k, v = proj_mha(h, w_qkv, q_norm, k_norm)
    (p,) = joint_softmax([gqa_logits(q, k)])
    return gqa_values(p, v).reshape(B, T, D_MODEL), k, v


def natten_latent(h, k_ctx, v_ctx, w_qkv, q_norm, k_norm, rpb):
    B, T, _ = h.shape
    q, k, v = proj_mha(h, w_qkv, q_norm, k_norm)
    rows = T // GRID_W
    kh = min(NA_ROWS, rows)
    qg = q.reshape(B, rows, GRID_W, N_HEADS, 1, HEAD_DIM)
    kg = k.reshape(B, rows, GRID_W, N_HEADS, HEAD_DIM)
    vg = v.reshape(B, rows, GRID_W, N_HEADS, HEAD_DIM)
    col = jnp.arange(GRID_W)
    cstart = jnp.clip(col - NA_COLS // 2, 0, GRID_W - NA_COLS)
    cmask = (col[None, :] >= cstart[:, None]) & (col[None, :] < cstart[:, None] + NA_COLS)
    mask = jnp.tile(cmask, (1, kh))
    coff = jnp.clip(col[None, :] - col[:, None] + NA_COLS - 1, 0, 2 * NA_COLS - 2)

    def row_block(r):
        r0 = jnp.clip(r - kh // 2, 0, rows - kh)
        kb = lax.dynamic_slice_in_dim(kg, r0, kh, axis=1).reshape(B, kh * GRID_W, N_HEADS, HEAD_DIM)
        vb = lax.dynamic_slice_in_dim(vg, r0, kh, axis=1).reshape(B, kh * GRID_W, N_HEADS, HEAD_DIM)
        qb = lax.dynamic_index_in_dim(qg, r, axis=1, keepdims=False)
        roff = r0 + jnp.arange(kh) - r + NA_ROWS - 1
        bias = rpb[:, roff[:, None, None], coff[None, :, :]].astype(jnp.float32)
        bias = bias.transpose(0, 2, 1, 3).reshape(N_HEADS, 1, GRID_W, kh * GRID_W)
        s_loc = jnp.where(mask, gqa_logits(qb, kb) + bias, NEG_INF)
        p_loc, p_ctx = joint_softmax([s_loc, gqa_logits(qb, k_ctx)])
        return gqa_values(p_loc, vb) + gqa_values(p_ctx, v_ctx)

    out = lax.map(row_block, jnp.arange(rows))
    return jnp.moveaxis(out, 0, 1).reshape(B, T, D_MODEL)


def diff_lambda_init(layer):
    return 0.8 - 0.6 * math.exp(-0.3 * layer)


def diff_lambda(lq1, lk1, lq2, lk2, lam_init):
    f = jnp.float32
    return (jnp.exp(jnp.sum(lq1.astype(f) * lk1.astype(f)))
            - jnp.exp(jnp.sum(lq2.astype(f) * lk2.astype(f))) + lam_init)


def diff_logits(q, k):
    return jnp.einsum('bqhcd,bshcd->bhcqs', q, k, preferred_element_type=jnp.float32) * (HEAD_DIM ** -0.5)


def diff_combine(ps, vs, lam, lam_init, subln):
    o = sum(jnp.einsum('bhqs,bshe->bqhe', (p[:, :, 0] - lam * p[:, :, 1]).astype(v.dtype), v)
            for p, v in zip(ps, vs))
    o = rms_norm(o, subln) * (1.0 - lam_init)
    return o.reshape(o.shape[0], o.shape[1], D_MODEL)


def diff_ctx(h, lam_init, w_qkv, q_norm, k_norm, lq1, lk1, lq2, lk2, subln):
    q, k, v = proj_diff(h, w_qkv, q_norm, k_norm)
    lam = diff_lambda(lq1, lk1, lq2, lk2, lam_init)
    (p,) = joint_softmax([diff_logits(q, k)])
    return diff_combine([p], [v], lam, lam_init, subln), k, v


def diff_latent(h, k_ctx, v_ctx, lam_init, w_qkv, q_norm, k_norm, lq1, lk1, lq2, lk2, subln):
    q, k, v = proj_diff(h, w_qkv, q_norm, k_norm)
    cos, sin = axial_rope_tables(h.shape[1])
    q = apply_rope(q, cos, sin)
    k = apply_rope(k, cos, sin)
    lam = diff_lambda(lq1, lk1, lq2, lk2, lam_init)

    def blk(j, qb):
        ps = joint_softmax([diff_logits(qb, k), diff_logits(qb, k_ctx)])
        return diff_combine(ps, [v, v_ctx], lam, lam_init, subln)

    return map_query_blocks(blk, q)


def window_ctx(h, w_qkv, q_norm, k_norm, sink):
    B, T, _ = h.shape
    q, k, v = proj_gqa(h, w_qkv, q_norm, k_norm)
    (p,) = joint_softmax([gqa_logits(q, k)], sink.reshape(N_KV_HEADS, GQA_GROUP, 1, 1))
    return gqa_values(p, v).reshape(B, T, D_MODEL), k, v


def window_latent(h, k_ctx, v_ctx, w_qkv, q_norm, k_norm, sink):
    B, T, _ = h.shape
    q, k, v = proj_gqa(h, w_qkv, q_norm, k_norm)
    cos, sin = axial_rope_tables(T)
    q = apply_rope(q, cos, sin)
    k = apply_rope(k, cos, sin)
    pad = ((0, 0), (WINDOW, WINDOW), (0, 0), (0, 0))
    k_pad = jnp.pad(k, pad)
    v_pad = jnp.pad(v, pad)
    span = Q_BLOCK + 2 * WINDOW
    sink_b = sink.reshape(N_KV_HEADS, GQA_GROUP, 1, 1)

    def blk(j, qb):
        start = j * Q_BLOCK
        kb = lax.dynamic_slice_in_dim(k_pad, start, span, axis=1)
        vb = lax.dynamic_slice_in_dim(v_pad, start, span, axis=1)
        qi = start + jnp.arange(Q_BLOCK)
        kj = start - WINDOW + jnp.arange(span)
        mask = (jnp.abs(qi[:, None] - kj[None, :]) <= WINDOW) & (kj[None, :] >= 0) & (kj[None, :] < T)
        s_loc = jnp.where(mask, gqa_logits(qb, kb), NEG_INF)
        p_loc, p_ctx = joint_softmax([s_loc, gqa_logits(qb, k_ctx)], sink_b)
        return (gqa_values(p_loc, vb) + gqa_values(p_ctx, v_ctx)).reshape(B, Q_BLOCK, D_MODEL)

    return map_query_blocks(blk, q)


def gqa_ctx(h, w_qkv, q_norm, k_norm):
    B, T, _ = h.shape
    q, k, v = proj_gqa(h, w_qkv, q_norm, k_norm)
    (p,) = joint_softmax([gqa_logits(q, k)])
    return gqa_values(p, v).reshape(B, T, D_MODEL), k, v


def gqa_latent(h, k_ctx, v_ctx, w_qkv, q_norm, k_norm):
    B, T, _ = h.shape
    q, k, v = proj_gqa(h, w_qkv, q_norm, k_norm)
    cos, sin = axial_rope_tables(T)
    q = apply_rope(q, cos, sin)
    k = apply_rope(k, cos, sin)

    def blk(j, qb):
        p_loc, p_ctx = joint_softmax([gqa_logits(qb, k), gqa_logits(qb, k_ctx)])
        return (gqa_values(p_loc, v) + gqa_values(p_ctx, v_ctx)).reshape(B, Q_BLOCK, D_MODEL)

    return map_query_blocks(blk, q)


def swiglu(x, w_gate, w_up, w_down):
    return (jax.nn.silu(x @ w_gate) * (x @ w_up)) @ w_down


def moe_swiglu(h, w_router, w_egate, w_eup, w_edown):
    B, T, D = h.shape
    x = h.reshape(B * T, D)
    logits = jnp.dot(x, w_router, preferred_element_type=jnp.float32)
    top_v, top_i = lax.top_k(logits, TOP_K)
    wts = jax.nn.softmax(top_v, axis=-1)
    gates = jnp.sum(jax.nn.one_hot(top_i, N_EXPERTS, dtype=jnp.float32) * wts[..., None], axis=1)
    y = jnp.zeros_like(x)
    for e in range(N_EXPERTS):
        y = y + gates[:, e:e + 1].astype(x.dtype) * swiglu(x, w_egate[e], w_eup[e], w_edown[e])
    return y.reshape(B, T, D)


def channel_mixer(layer, h, p):
    if layer % 2 == 0:
        return swiglu(h, *p)
    return moe_swiglu(h, *p)


def mixer_ctx(layer, h, p):
    m = layer % N_MIXERS
    if m == 0:
        return natten_ctx(h, *p[:3])
    if m == 1:
        return diff_ctx(h, diff_lambda_init(layer), *p)
    if m == 2:
        return window_ctx(h, *p)
    return gqa_ctx(h, *p)


def mixer_latent(layer, h, k_ctx, v_ctx, p):
    m = layer % N_MIXERS
    if m == 0:
        return natten_latent(h, k_ctx, v_ctx, *p)
    if m == 1:
        return diff_latent(h, k_ctx, v_ctx, diff_lambda_init(layer), *p)
    if m == 2:
        return window_latent(h, k_ctx, v_ctx, *p)
    return gqa_latent(h, k_ctx, v_ctx, *p)


def setup_inputs(seed: int = 0) -> dict:
    key = jax.random.key(seed)
    keys = iter(jax.random.split(key, 128))

    def nrm(shape, scale=1.0):
        return jax.random.normal(next(keys), shape, jnp.float32) * scale

    def gain(n):
        return 1.0 + nrm((n,), 0.05)

    D = D_MODEL
    inp = {}
    inp['x_prompt'] = nrm((BATCH, SEQ, D))
    inp['x_sample'] = nrm((DEC_BATCH, DEC_SEQ, D))
    inp['cache_k_0'] = nrm((DEC_BATCH, PAST_LEN, N_HEADS, HEAD_DIM))
    inp['cache_v_0'] = nrm((DEC_BATCH, PAST_LEN, N_HEADS, HEAD_DIM))
    inp['cache_k_1'] = nrm((DEC_BATCH, PAST_LEN, DIFF_HEADS, 2, HEAD_DIM))
    inp['cache_v_1'] = nrm((DEC_BATCH, PAST_LEN, DIFF_HEADS, 2 * HEAD_DIM))
    inp['cache_k_2'] = nrm((DEC_BATCH, PAST_LEN, N_KV_HEADS, HEAD_DIM))
    inp['cache_v_2'] = nrm((DEC_BATCH, PAST_LEN, N_KV_HEADS, HEAD_DIM))
    inp['cache_k_3'] = nrm((DEC_BATCH, PAST_LEN, N_KV_HEADS, HEAD_DIM))
    inp['cache_v_3'] = nrm((DEC_BATCH, PAST_LEN, N_KV_HEADS, HEAD_DIM))
    inp['c'] = nrm((DEC_BATCH, D))
    inp['c_ctx'] = nrm((D,))
    for l in range(DEPTH):
        m = l % N_MIXERS
        inp[f'norm1_{l}'] = gain(D)
        inp[f'w_ada_{l}'] = nrm((D, 6 * D), 0.5 * D ** -0.5)
        inp[f'b_ada_{l}'] = nrm((6 * D,), 0.02)
        if m == 0:
            inp[f'w_qkv_{l}'] = nrm((D, 3 * D), D ** -0.5)
            inp[f'q_norm_{l}'] = gain(HEAD_DIM)
            inp[f'k_norm_{l}'] = gain(HEAD_DIM)
            inp[f'rpb_{l}'] = nrm((N_HEADS, 2 * NA_ROWS - 1, 2 * NA_COLS - 1), 0.1)
        elif m == 1:
            inp[f'w_qkv_{l}'] = nrm((D, 3 * D), D ** -0.5)
            inp[f'q_norm_{l}'] = gain(HEAD_DIM)
            inp[f'k_norm_{l}'] = gain(HEAD_DIM)
            inp[f'lam_q1_{l}'] = nrm((HEAD_DIM,), 0.1)
            inp[f'lam_k1_{l}'] = nrm((HEAD_DIM,), 0.1)
            inp[f'lam_q2_{l}'] = nrm((HEAD_DIM,), 0.1)
            inp[f'lam_k2_{l}'] = nrm((HEAD_DIM,), 0.1)
            inp[f'subln_{l}'] = gain(2 * HEAD_DIM)
        elif m == 2:
            inp[f'w_qkv_{l}'] = nrm((D, QKV_GQA), D ** -0.5)
            inp[f'q_norm_{l}'] = gain(HEAD_DIM)
            inp[f'k_norm_{l}'] = gain(HEAD_DIM)
            inp[f'sink_{l}'] = nrm((N_HEADS,), 0.5)
        else:
            inp[f'w_qkv_{l}'] = nrm((D, QKV_GQA), D ** -0.5)
            inp[f'q_norm_{l}'] = gain(HEAD_DIM)
            inp[f'k_norm_{l}'] = gain(HEAD_DIM)
        inp[f'w_o_{l}'] = nrm((D, D), D ** -0.5)
        inp[f'norm2_{l}'] = gain(D)
        if l % 2 == 0:
            inp[f'w_gate_{l}'] = nrm((D, DENSE_FF), D ** -0.5)
            inp[f'w_up_{l}'] = nrm((D, DENSE_FF), D ** -0.5)
            inp[f'w_down_{l}'] = nrm((DENSE_FF, D), DENSE_FF ** -0.5)
        else:
            inp[f'w_router_{l}'] = nrm((D, N_EXPERTS), D ** -0.5)
            inp[f'w_egate_{l}'] = nrm((N_EXPERTS, D, EXPERT_FF), D ** -0.5)
            inp[f'w_eup_{l}'] = nrm((N_EXPERTS, D, EXPERT_FF), D ** -0.5)
            inp[f'w_edown_{l}'] = nrm((N_EXPERTS, EXPERT_FF, D), EXPERT_FF ** -0.5)
    return inp


def reference(x_prompt, x_sample, cache_k_0, cache_v_0, cache_k_1, cache_v_1, cache_k_2, cache_v_2,
              cache_k_3, cache_v_3, c, c_ctx,
              norm1_0, w_ada_0, b_ada_0, w_qkv_0, q_norm_0, k_norm_0, rpb_0, w_o_0, norm2_0,
              w_gate_0, w_up_0, w_down_0,
              norm1_1, w_ada_1, b_ada_1, w_qkv_1, q_norm_1, k_norm_1, lam_q1_1, lam_k1_1, lam_q2_1,
              lam_k2_1, subln_1, w_o_1, norm2_1, w_router_1, w_egate_1, w_eup_1, w_edown_1,
              norm1_2, w_ada_2, b_ada_2, w_qkv_2, q_norm_2, k_norm_2, sink_2, w_o_2, norm2_2,
              w_gate_2, w_up_2, w_down_2,
              norm1_3, w_ada_3, b_ada_3, w_qkv_3, q_norm_3, k_norm_3, w_o_3, norm2_3,
              w_router_3, w_egate_3, w_eup_3, w_edown_3):
    norm1 = (norm1_0, norm1_1, norm1_2, norm1_3)
    norm2 = (norm2_0, norm2_1, norm2_2, norm2_3)
    ada_p = ((w_ada_0, b_ada_0), (w_ada_1, b_ada_1), (w_ada_2, b_ada_2), (w_ada_3, b_ada_3))
    mixer_p = ((w_qkv_0, q_norm_0, k_norm_0, rpb_0),
               (w_qkv_1, q_norm_1, k_norm_1, lam_q1_1, lam_k1_1, lam_q2_1, lam_k2_1, subln_1),
               (w_qkv_2, q_norm_2, k_norm_2, sink_2),
               (w_qkv_3, q_norm_3, k_norm_3))
    w_out = (w_o_0, w_o_1, w_o_2, w_o_3)
    ffn_p = ((w_gate_0, w_up_0, w_down_0),
             (w_router_1, w_egate_1, w_eup_1, w_edown_1),
             (w_gate_2, w_up_2, w_down_2),
             (w_router_3, w_egate_3, w_eup_3, w_edown_3))
    caches = ((cache_k_0, cache_v_0), (cache_k_1, cache_v_1), (cache_k_2, cache_v_2), (cache_k_3, cache_v_3))

    xp = x_prompt
    xs = x_sample
    st = []
    for l in range(DEPTH):
        w_ada, b_ada = ada_p[l]
        sh1, sc1, g1, sh2, sc2, g2 = ada_mod(c_ctx[None, :], w_ada, b_ada)
        o, k_c, v_c = mixer_ctx(l, modulate(rms_norm(xp, norm1[l]), sh1, sc1), mixer_p[l])
        xp = xp + g1 * (o @ w_out[l])
        xp = xp + g2 * channel_mixer(l, modulate(rms_norm(xp, norm2[l]), sh2, sc2), ffn_p[l])
        st.append((k_c, v_c))
        sh1, sc1, g1, sh2, sc2, g2 = ada_mod(c, w_ada, b_ada)
        o = mixer_latent(l, modulate(rms_norm(xs, norm1[l]), sh1, sc1), caches[l][0], caches[l][1], mixer_p[l])
        xs = xs + g1 * (o @ w_out[l])
        xs = xs + g2 * channel_mixer(l, modulate(rms_norm(xs, norm2[l]), sh2, sc2), ffn_p[l])
    return (xp, xs, st[0][0], st[0][1], st[1][0], st[1][1], st[2][0], st[2][1], st[3][0], st[3][1])
```

```python
import functools
import math

import numpy as np
import jax
import jax.numpy as jnp
from jax import lax
from jax.experimental import pallas as pl
from jax.experimental.pallas import tpu as pltpu

F32 = jnp.float32
BF16 = jnp.bfloat16

HEAD_DIM = 64
GRID_W = 64
NA_ROWS = 8
NA_COLS = 16
WINDOW = 128
ROPE_BASE = 10000.0
ROPE_PAIRS_AXIS = HEAD_DIM // 4
N_EXPERTS = 8
EPS = 1e-6
NEG_INF = -1e30
QK_SCALE = HEAD_DIM ** -0.5

LANES = 128
V7X_VMEM_BYTES = 64 * 1024 * 1024
VMEM_LIMIT = V7X_VMEM_BYTES - 8 * 1024 * 1024

ROW_TILE = 512
ATTN_Q_TILE = 256
NA_KEY_ROWS = 12
MOE_BLOCK = 1024
MOE_SUB = 256


def _cparams(sem):
    return pltpu.CompilerParams(dimension_semantics=sem, vmem_limit_bytes=VMEM_LIMIT)


def _silu(x):
    return x * (1.0 / (1.0 + jnp.exp(-x)))


def _norm_mod(x, g, shift, scale):
    ms = jnp.mean(x * x, axis=-1, keepdims=True)
    y = x * lax.rsqrt(ms + EPS) * g
    return y * (1.0 + scale) + shift


def _dot(a, b):
    return jnp.dot(a, b, preferred_element_type=F32)


def _dot_nt(a, b):
    return lax.dot_general(a, b, (((1,), (1,)), ((), ())), preferred_element_type=F32)


def _ada_kernel(c_ref, w_ref, b_ref, o_ref):
    a = _silu(c_ref[...])
    a_hi = a.astype(BF16)
    a_lo = a - a_hi.astype(F32)
    row = lax.broadcasted_iota(jnp.int32, a.shape, 0)
    lhs = jnp.where(row < 8, a_hi.astype(F32), a_lo).astype(BF16)
    w = w_ref[...]
    w_hi = w.astype(BF16)
    w_lo = (w - w_hi.astype(F32)).astype(BF16)
    r = _dot(lhs, w_hi)
    r2 = _dot(a_hi, w_lo)
    o_ref[...] = r[:8] + r[8:] + r2[:8] + b_ref[...]


def ada_mod(cond16, w_ada, b_ada):
    d, n = w_ada.shape
    tn = 1024
    out = pl.pallas_call(
        _ada_kernel,
        grid=(n // tn,),
        in_specs=[pl.BlockSpec((16, d), lambda j: (0, 0)),
                  pl.BlockSpec((d, tn), lambda j: (0, j)),
                  pl.BlockSpec((1, tn), lambda j: (0, j))],
        out_specs=pl.BlockSpec((8, tn), lambda j: (0, j)),
        out_shape=jax.ShapeDtypeStruct((8, n), F32),
        compiler_params=_cparams(("arbitrary",)),
    )(cond16, w_ada, b_ada.reshape(1, n))
    return out.reshape(8, 6, d)


def _qkv_kernel(*refs, rope):
    if rope:
        x_ref, mod_ref, g_ref, w_ref, seg_ref, hn_ref, isn_ref, cos_ref, sin_ref, o_ref, h_scr = refs
    else:
        x_ref, mod_ref, g_ref, w_ref, seg_ref, hn_ref, isn_ref, o_ref, h_scr = refs

    @pl.when(pl.program_id(1) == 0)
    def _():
        h = _norm_mod(x_ref[...], g_ref[...], mod_ref[0, 0:1, :], mod_ref[0, 1:2, :])
        h_scr[...] = h.astype(BF16)

    acc = _dot(h_scr[...], w_ref[...])
    tn = acc.shape[1]
    sq = (acc * acc).astype(BF16)
    seg = seg_ref[...]
    ssum = jnp.concatenate(
        [_dot(sq[:, c * 2 * LANES:(c + 1) * 2 * LANES], seg) for c in range(tn // (2 * LANES))], axis=1)
    y = acc * lax.rsqrt(ssum * (1.0 / HEAD_DIM) + EPS) * hn_ref[...]
    if rope:
        lane = lax.broadcasted_iota(jnp.int32, (acc.shape[0], LANES), 1)
        even = (lane % 2) == 0
        parts = []
        for c in range(tn // LANES):
            yc = y[:, c * LANES:(c + 1) * LANES]
            nxt = pltpu.roll(yc, LANES - 1, 1)
            prv = pltpu.roll(yc, 1, 1)
            parts.append(jnp.where(even, nxt, prv))
        swapped = jnp.concatenate(parts, axis=1)
        y = y * cos_ref[...] + swapped * sin_ref[...]
    o_ref[...] = jnp.where(isn_ref[...] > 0.5, y, acc).astype(o_ref.dtype)


def _seg_matrix():
    i = np.arange(2 * LANES)
    return jnp.asarray((i[:, None] // HEAD_DIM) == (i[None, :] // HEAD_DIM), BF16)


def qkv_proj(x, mod, mod_of_block, g, w, hn, isn, rope_tabs, out_dtype, tn):
    r, d = x.shape
    n = w.shape[1]
    tm = ROW_TILE
    rope = rope_tabs is not None
    in_specs = [pl.BlockSpec((tm, d), lambda i, j: (i, 0)),
                pl.BlockSpec((1, 6, d), lambda i, j: (mod_of_block(i), 0, 0)),
                pl.BlockSpec((1, d), lambda i, j: (0, 0)),
                pl.BlockSpec((d, tn), lambda i, j: (0, j)),
                pl.BlockSpec((2 * LANES, 2 * LANES), lambda i, j: (0, 0)),
                pl.BlockSpec((1, tn), lambda i, j: (0, j)),
                pl.BlockSpec((1, tn), lambda i, j: (0, j))]
    args = [x, mod, g.reshape(1, d), w, _seg_matrix(), hn, isn]
    if rope:
        cos_t, sin_t = rope_tabs
        nblk = cos_t.shape[0] // tm
        in_specs += [pl.BlockSpec((tm, tn), lambda i, j: (i % nblk, 0)),
                     pl.BlockSpec((tm, tn), lambda i, j: (i % nblk, 0))]
        args += [cos_t, sin_t]
    return pl.pallas_call(
        functools.partial(_qkv_kernel, rope=rope),
        grid=(r // tm, n // tn),
        in_specs=in_specs,
        out_specs=pl.BlockSpec((tm, tn), lambda i, j: (i, j)),
        out_shape=jax.ShapeDtypeStruct((r, n), out_dtype),
        scratch_shapes=[pltpu.VMEM((tm, d), BF16)],
        compiler_params=_cparams(("parallel", "arbitrary")),
    )(*args)


def _rope_tables(t, tn):
    pos = np.arange(t)
    row = (pos // GRID_W).astype(np.float32)
    col = (pos % GRID_W).astype(np.float32)
    freqs = jnp.asarray(ROPE_BASE, F32) ** (-jnp.arange(ROPE_PAIRS_AXIS, dtype=F32) / ROPE_PAIRS_AXIS)
    ang = jnp.concatenate([jnp.asarray(row)[:, None] * freqs, jnp.asarray(col)[:, None] * freqs], axis=-1)
    cos = jnp.repeat(jnp.cos(ang), 2, axis=-1)
    sin = jnp.repeat(jnp.sin(ang), 2, axis=-1)
    sign = jnp.asarray(np.where(np.arange(HEAD_DIM) % 2 == 0, -1.0, 1.0), F32)
    reps = tn // HEAD_DIM
    return jnp.tile(cos, (1, reps)), jnp.tile(sin * sign, (1, reps))


def _attn_kernel(*refs, mode, gqa, t_loc, n_cache, win, band, natten, has_bias, has_sink,
                 lam_init, tq):
    it = iter(refs)
    sink_ref = next(it) if has_sink else None
    q_ref, k_ref, v_ref = next(it), next(it), next(it)
    ck_ref = cv_ref = bias_ref = lam_ref = subln_ref = None
    if n_cache:
        ck_ref, cv_ref = next(it), next(it)
    if has_bias:
        bias_ref = next(it)
    if mode == "diff":
        lam_ref, subln_ref = next(it), next(it)
    o_ref, kf, vf = next(it), next(it), next(it)

    j = pl.program_id(1)
    qb = pl.program_id(2)

    def place(a):
        if not gqa:
            return a.astype(BF16)
        half = (j // 2) % 2
        lane_half = (lax.broadcasted_iota(jnp.int32, a.shape, 1) >= HEAD_DIM).astype(jnp.int32)
        return jnp.where(lane_half == half, a, pltpu.roll(a, HEAD_DIM, 1)).astype(BF16)

    @pl.when(qb == 0)
    def _():
        kf[0:t_loc, :] = place(k_ref[...].astype(F32))
        vf[0:t_loc, :] = place(v_ref[...].astype(F32))
        if n_cache:
            kf[t_loc:t_loc + n_cache, :] = place(ck_ref[0])
            vf[t_loc:t_loc + n_cache, :] = place(cv_ref[0])

    q2 = q_ref[...].astype(F32) * QK_SCALE
    lane = lax.broadcasted_iota(jnp.int32, q2.shape, 1)
    q_st = jnp.concatenate([jnp.where(lane < HEAD_DIM, q2, 0.0),
                            jnp.where(lane >= HEAD_DIM, q2, 0.0)], axis=0).astype(BF16)

    if natten:
        ws = jnp.clip(qb * (tq // GRID_W) - NA_ROWS // 2, 0, t_loc // GRID_W - NA_KEY_ROWS) * GRID_W
    elif band:
        ws = jnp.clip(qb * tq - WINDOW, 0, t_loc - win)
    else:
        ws = 0
    if natten or band:
        ws = pl.multiple_of(ws, GRID_W)
        k_loc = kf[pl.ds(ws, win), :]
        v_loc = vf[pl.ds(ws, win), :]
    else:
        k_loc = kf[0:win, :]
        v_loc = vf[0:win, :]

    s_loc = _dot_nt(q_st, k_loc)
    if has_bias:
        s_loc = s_loc + jnp.concatenate([bias_ref[0, 0], bias_ref[0, 1]], axis=0)
    if band:
        qpos = qb * tq + lax.broadcasted_iota(jnp.int32, s_loc.shape, 0) % tq
        kpos = ws + lax.broadcasted_iota(jnp.int32, s_loc.shape, 1)
        s_loc = jnp.where(jnp.abs(qpos - kpos) <= WINDOW, s_loc, NEG_INF)
    m = jnp.max(s_loc, axis=1, keepdims=True)
    if n_cache:
        s_c = _dot_nt(q_st, kf[t_loc:t_loc + n_cache, :])
        m = jnp.maximum(m, jnp.max(s_c, axis=1, keepdims=True))
    if has_sink:
        row = lax.broadcasted_iota(jnp.int32, (2 * tq, 1), 0)
        sink = jnp.where(row < tq, sink_ref[2 * j], sink_ref[2 * j + 1])
        m = jnp.maximum(m, sink)
    e_loc = jnp.exp(s_loc - m)
    den = jnp.sum(e_loc, axis=1, keepdims=True)
    if n_cache:
        e_c = jnp.exp(s_c - m)
        den = den + jnp.sum(e_c, axis=1, keepdims=True)
    if has_sink:
        den = den + jnp.exp(sink - m)
    inv = 1.0 / den

    if mode == "pair":
        o = _dot(e_loc.astype(BF16), v_loc)
        if n_cache:
            o = o + _dot(e_c.astype(BF16), vf[t_loc:t_loc + n_cache, :])
        o = o * inv
        o_ref[...] = jnp.where(lane < HEAD_DIM, o[:tq], o[tq:]).astype(o_ref.dtype)
    else:
        lp = lam_ref[...]
        lam = (jnp.exp(jnp.sum(lp[0:1] * lp[1:2], axis=1, keepdims=True))
               - jnp.exp(jnp.sum(lp[2:3] * lp[3:4], axis=1, keepdims=True)) + lam_init)
        w0 = inv[:tq]
        w1 = inv[tq:] * lam
        o = _dot((e_loc[:tq] * w0 - e_loc[tq:] * w1).astype(BF16), v_loc)
        if n_cache:
            o = o + _dot((e_c[:tq] * w0 - e_c[tq:] * w1).astype(BF16), vf[t_loc:t_loc + n_cache, :])
        ms = jnp.mean(o * o, axis=-1, keepdims=True)
        o = o * lax.rsqrt(ms + EPS) * subln_ref[...] * (1.0 - lam_init)
        o_ref[...] = o.astype(o_ref.dtype)


def attention(qkv, nb, t_loc, *, mode, gqa, k_col, v_col, cache=None, band=False, bias=None,
              sink=None, lam=None, subln=None, lam_init=0.0, d_model):
    tq = ATTN_Q_TILE
    n_qlb = d_model // LANES
    natten = bias is not None
    n_cache = 0 if cache is None else cache[0].shape[1]
    if natten:
        win = NA_KEY_ROWS * GRID_W
    elif band:
        win = tq + 2 * WINDOW
    else:
        win = t_loc
    kmap = (lambda j: j // 4) if gqa else (lambda j: j)
    kc, vc = k_col // LANES, v_col // LANES
    nqb = t_loc // tq

    in_specs = []
    args = []
    if sink is not None:
        in_specs.append(pl.BlockSpec(memory_space=pltpu.SMEM))
        args.append(sink)
    in_specs += [pl.BlockSpec((tq, LANES), lambda b, j, i: (b * nqb + i, j)),
                 pl.BlockSpec((t_loc, LANES), lambda b, j, i: (b, kc + kmap(j))),
                 pl.BlockSpec((t_loc, LANES), lambda b, j, i: (b, vc + kmap(j)))]
    args += [qkv, qkv, qkv]
    if cache is not None:
        in_specs += [pl.BlockSpec((1, n_cache, LANES), lambda b, j, i: (b, 0, kmap(j))),
                     pl.BlockSpec((1, n_cache, LANES), lambda b, j, i: (b, 0, kmap(j)))]
        args += list(cache)
    if natten:
        last = nqb - 1
        in_specs.append(pl.BlockSpec(
            (1, 2, tq, win),
            lambda b, j, i: (jnp.where(i == 0, 0, jnp.where(i == last, 2, 1)), j, 0, 0)))
        args.append(bias)
    if mode == "diff":
        in_specs += [pl.BlockSpec((8, LANES), lambda b, j, i: (0, 0)),
                     pl.BlockSpec((1, LANES), lambda b, j, i: (0, 0))]
        args += [lam, subln]
    kern = functools.partial(
        _attn_kernel, mode=mode, gqa=gqa, t_loc=t_loc, n_cache=n_cache, win=win, band=band,
        natten=natten, has_bias=natten, has_sink=sink is not None, lam_init=lam_init, tq=tq)
    return pl.pallas_call(
        kern,
        grid=(nb, n_qlb, nqb),
        in_specs=in_specs,
        out_specs=pl.BlockSpec((tq, LANES), lambda b, j, i: (b * nqb + i, j)),
        out_shape=jax.ShapeDtypeStruct((nb * t_loc, d_model), BF16),
        scratch_shapes=[pltpu.VMEM((t_loc + n_cache, LANES), BF16),
                        pltpu.VMEM((t_loc + n_cache, LANES), BF16)],
        compiler_params=_cparams(("parallel", "arbitrary", "arbitrary")),
    )(*args)


def _natten_bias_table(rpb, rows):
    kh = min(NA_ROWS, rows)
    qrows = ATTN_Q_TILE // GRID_W
    col = np.arange(GRID_W)
    cstart = np.clip(col - NA_COLS // 2, 0, GRID_W - NA_COLS)
    cmask = (col[None, :] >= cstart[:, None]) & (col[None, :] < cstart[:, None] + NA_COLS)
    coff = np.clip(col[None, :] - col[:, None] + NA_COLS - 1, 0, 2 * NA_COLS - 2)
    dr_idx = np.zeros((3, qrows, NA_KEY_ROWS), np.int32)
    vis = np.zeros((3, qrows, NA_KEY_ROWS), bool)
    for p, r_first in enumerate((0, 2 * qrows, rows - qrows)):
        ws = int(np.clip(r_first - NA_ROWS // 2, 0, rows - NA_KEY_ROWS))
        for a in range(qrows):
            r = r_first + a
            r0 = int(np.clip(r - kh // 2, 0, rows - kh))
            for i in range(NA_KEY_ROWS):
                kr = ws + i
                if r0 <= kr < r0 + kh:
                    vis[p, a, i] = True
                    dr_idx[p, a, i] = kr - r + NA_ROWS - 1
    g = rpb[:, dr_idx[:, :, None, :, None], coff[None, None, :, None, :]]
    ok = vis[:, :, None, :, None] & cmask[None, None, :, None, :]
    g = jnp.where(ok[None], g.astype(F32), NEG_INF)
    h = rpb.shape[0]
    g = g.reshape(h, 3, qrows * GRID_W, NA_KEY_ROWS * GRID_W)
    return jnp.transpose(g, (1, 0, 2, 3))


def _proj_res_kernel(a_ref, w_ref, x_ref, mod_ref, o_ref):
    o_ref[...] = x_ref[...] + mod_ref[0, 2:3, :] * _dot(a_ref[...], w_ref[...])


def proj_residual(a, w, x, mod, mod_of_block):
    r, d = x.shape
    tm = ROW_TILE
    return pl.pallas_call(
        _proj_res_kernel,
        grid=(r // tm,),
        in_specs=[pl.BlockSpec((tm, a.shape[1]), lambda i: (i, 0)),
                  pl.BlockSpec(w.shape, lambda i: (0, 0)),
                  pl.BlockSpec((tm, d), lambda i: (i, 0)),
                  pl.BlockSpec((1, 6, d), lambda i: (mod_of_block(i), 0, 0))],
        out_specs=pl.BlockSpec((tm, d), lambda i: (i, 0)),
        out_shape=jax.ShapeDtypeStruct((r, d), F32),
        compiler_params=_cparams(("parallel",)),
    )(a, w, x, mod)


def _ffn_kernel(x_ref, mod_ref, g_ref, wg_ref, wu_ref, wd_ref, o_ref, h_scr, acc_scr):
    f = pl.program_id(1)

    @pl.when(f == 0)
    def _():
        h = _norm_mod(x_ref[...], g_ref[...], mod_ref[0, 3:4, :], mod_ref[0, 4:5, :])
        h_scr[...] = h.astype(BF16)

    h = h_scr[...]
    t = (_silu(_dot(h, wg_ref[...])) * _dot(h, wu_ref[...])).astype(BF16)
    y = _dot(t, wd_ref[...])

    @pl.when(f == 0)
    def _():
        acc_scr[...] = y

    @pl.when(f > 0)
    def _():
        acc_scr[...] += y

    @pl.when(f == pl.num_programs(1) - 1)
    def _():
        o_ref[...] = x_ref[...] + mod_ref[0, 5:6, :] * acc_scr[...]


def dense_ffn(x, mod, mod_of_block, g, wg, wu, wd):
    r, d = x.shape
    ff = wg.shape[1]
    tm = ROW_TILE
    nf = 2
    tf = ff // nf
    return pl.pallas_call(
        _ffn_kernel,
        grid=(r // tm, nf),
        in_specs=[pl.BlockSpec((tm, d), lambda i, f: (i, 0)),
                  pl.BlockSpec((1, 6, d), lambda i, f: (mod_of_block(i), 0, 0)),
                  pl.BlockSpec((1, d), lambda i, f: (0, 0)),
                  pl.BlockSpec((d, tf), lambda i, f: (0, f)),
                  pl.BlockSpec((d, tf), lambda i, f: (0, f)),
                  pl.BlockSpec((tf, d), lambda i, f: (f, 0))],
        out_specs=pl.BlockSpec((tm, d), lambda i, f: (i, 0)),
        out_shape=jax.ShapeDtypeStruct((r, d), F32),
        scratch_shapes=[pltpu.VMEM((tm, d), BF16), pltpu.VMEM((tm, d), F32)],
        compiler_params=_cparams(("parallel", "arbitrary")),
    )(x, mod, g.reshape(1, d), wg, wu, wd)


def _router_kernel(x_ref, mod_ref, g_ref, wr_ref, h_ref, rank_row_ref, gate_row_ref, rank_col_ref, cnt_ref):
    h = _norm_mod(x_ref[...], g_ref[...], mod_ref[0, 3:4, :], mod_ref[0, 4:5, :])
    h_hi = h.astype(BF16)
    h_ref[...] = h_hi
    h_lo = (h - h_hi.astype(F32)).astype(BF16)
    wr = wr_ref[...]
    w_hi = wr.astype(BF16)
    w_lo = (wr - w_hi.astype(F32)).astype(BF16)
    logits = _dot(h_hi, w_hi) + (_dot(h_lo, w_hi) + _dot(h_hi, w_lo))
    tb = logits.shape[0]
    lane = lax.broadcasted_iota(jnp.int32, logits.shape, 1)
    lane_f = lane.astype(F32)
    logits = jnp.where(lane < N_EXPERTS, logits, -jnp.inf)
    m1 = jnp.max(logits, axis=1, keepdims=True)
    i1 = jnp.min(jnp.where(logits == m1, lane_f, float(LANES)), axis=1, keepdims=True)
    rest = jnp.where(lane_f == i1, -jnp.inf, logits)
    m2 = jnp.max(rest, axis=1, keepdims=True)
    i2 = jnp.min(jnp.where(rest == m2, lane_f, float(LANES)), axis=1, keepdims=True)
    e2 = jnp.exp(m2 - m1)
    inv = 1.0 / (1.0 + e2)
    is1 = lane_f == i1
    is2 = lane_f == i2
    gates = jnp.where(is1, inv, 0.0) + jnp.where(is2, e2 * inv, 0.0)
    sel = jnp.where(is1 | is2, 1.0, 0.0)
    sel_b = sel.astype(BF16)
    chunk = 256
    parts = []
    for c in range(tb // chunk):
        ri = lax.broadcasted_iota(jnp.int32, (chunk, tb), 0) + c * chunk
        ci = lax.broadcasted_iota(jnp.int32, (chunk, tb), 1)
        parts.append(_dot(jnp.where(ci < ri, 1.0, 0.0).astype(BF16), sel_b))
    rank = jnp.concatenate(parts, axis=0)
    rank_sel = jnp.where(sel > 0.5, rank, -1.0)
    rank_col_ref[...] = rank_sel
    rank_row_ref[0] = rank_sel.T[:N_EXPERTS]
    gate_row_ref[0] = gates.T[:N_EXPERTS]
    cnt_ref[0] = jnp.broadcast_to(jnp.sum(sel, axis=0, keepdims=True), (8, LANES)).astype(jnp.int32)


def _expert_kernel(cnt_ref, h_ref, rank_row_ref, gate_row_ref, rank_col_ref, wg_ref, wu_ref, wd_ref,
                   x_ref, mod_ref, o_ref, xg, yacc, gs):
    b, e, f = pl.program_id(0), pl.program_id(1), pl.program_id(2)
    nf = pl.num_programs(2)
    tb = h_ref.shape[0]
    m = MOE_SUB
    n_rows = cnt_ref[b * N_EXPERTS + e]
    nsub = (n_rows + m - 1) // m

    @pl.when((e == 0) & (f == 0))
    def _():
        o_ref[...] = jnp.zeros_like(o_ref)

    @pl.when(f == 0)
    def _():
        rank_e = rank_row_ref[0, pl.ds(e, 1), :]
        gate_e = gate_row_ref[0, pl.ds(e, 1), :]

        def gather(i, carry):
            base = pl.multiple_of(i * m, m)
            want = (lax.broadcasted_iota(jnp.int32, (m, tb), 0) + base).astype(F32)
            p = jnp.where(rank_e == want, 1.0, 0.0)
            xg[pl.ds(base, m), :] = _dot(p.astype(BF16), h_ref[...]).astype(BF16)
            gs[pl.ds(base, m), :] = jnp.broadcast_to(jnp.sum(p * gate_e, axis=1, keepdims=True), (m, LANES))
            return carry

        lax.fori_loop(0, nsub, gather, 0)

    def ffn(i, carry):
        base = pl.multiple_of(i * m, m)
        xr = xg[pl.ds(base, m), :]
        t = (_silu(_dot(xr, wg_ref[0])) * _dot(xr, wu_ref[0])).astype(BF16)
        y = _dot(t, wd_ref[0])

        @pl.when(f == 0)
        def _():
            yacc[pl.ds(base, m), :] = y

        @pl.when(f > 0)
        def _():
            yacc[pl.ds(base, m), :] += y

        return carry

    lax.fori_loop(0, nsub, ffn, 0)

    @pl.when(f == nf - 1)
    def _():
        lane = lax.broadcasted_iota(jnp.int32, (tb, LANES), 1)
        rank_c = jnp.sum(jnp.where(lane == e, rank_col_ref[...], 0.0), axis=1, keepdims=True)

        def scatter(i, carry):
            base = pl.multiple_of(i * m, m)
            want = (lax.broadcasted_iota(jnp.int32, (tb, m), 1) + base).astype(F32)
            pt = jnp.where(rank_c == want, 1.0, 0.0).astype(BF16)
            yg = (yacc[pl.ds(base, m), :] * gs[pl.ds(base, m), 0:1]).astype(BF16)
            o_ref[...] += _dot(pt, yg)
            return carry

        lax.fori_loop(0, nsub, scatter, 0)

    @pl.when((e == N_EXPERTS - 1) & (f == nf - 1))
    def _():
        o_ref[...] = x_ref[...] + mod_ref[0, 5:6, :] * o_ref[...]


def moe_ffn(x, mod, mod_of_block, g, w_router, wg, wu, wd):
    r, d = x.shape
    ff = wg.shape[2]
    tb = MOE_BLOCK
    nb = r // tb
    wr = jnp.pad(w_router, ((0, 0), (0, LANES - N_EXPERTS)))
    h, rank_row, gate_row, rank_col, cnt = pl.pallas_call(
        _router_kernel,
        grid=(nb,),
        in_specs=[pl.BlockSpec((tb, d), lambda i: (i, 0)),
                  pl.BlockSpec((1, 6, d), lambda i: (mod_of_block(i), 0, 0)),
                  pl.BlockSpec((1, d), lambda i: (0, 0)),
                  pl.BlockSpec((d, LANES), lambda i: (0, 0))],
        out_specs=[pl.BlockSpec((tb, d), lambda i: (i, 0)),
                   pl.BlockSpec((1, N_EXPERTS, tb), lambda i: (i, 0, 0)),
                   pl.BlockSpec((1, N_EXPERTS, tb), lambda i: (i, 0, 0)),
                   pl.BlockSpec((tb, LANES), lambda i: (i, 0)),
                   pl.BlockSpec((1, 8, LANES), lambda i: (i, 0, 0))],
        out_shape=[jax.ShapeDtypeStruct((r, d), BF16),
                   jax.ShapeDtypeStruct((nb, N_EXPERTS, tb), F32),
                   jax.ShapeDtypeStruct((nb, N_EXPERTS, tb), F32),
                   jax.ShapeDtypeStruct((r, LANES), F32),
                   jax.ShapeDtypeStruct((nb, 8, LANES), jnp.int32)],
        compiler_params=_cparams(("parallel",)),
    )(x, mod, g.reshape(1, d), wr)
    counts = cnt[:, 0, :N_EXPERTS].reshape(nb * N_EXPERTS)

    nf = 2
    tf = ff // nf
    grid_spec = pltpu.PrefetchScalarGridSpec(
        num_scalar_prefetch=1,
        grid=(nb, N_EXPERTS, nf),
        in_specs=[pl.BlockSpec((tb, d), lambda i, e, f, c: (i, 0)),
                  pl.BlockSpec((1, N_EXPERTS, tb), lambda i, e, f, c: (i, 0, 0)),
                  pl.BlockSpec((1, N_EXPERTS, tb), lambda i, e, f, c: (i, 0, 0)),
                  pl.BlockSpec((tb, LANES), lambda i, e, f, c: (i, 0)),
                  pl.BlockSpec((1, d, tf), lambda i, e, f, c: (e, 0, f)),
                  pl.BlockSpec((1, d, tf), lambda i, e, f, c: (e, 0, f)),
                  pl.BlockSpec((1, tf, d), lambda i, e, f, c: (e, f, 0)),
                  pl.BlockSpec((tb, d), lambda i, e, f, c: (i, 0)),
                  pl.BlockSpec((1, 6, d), lambda i, e, f, c: (mod_of_block(i), 0, 0))],
        out_specs=pl.BlockSpec((tb, d), lambda i, e, f, c: (i, 0)),
        scratch_shapes=[pltpu.VMEM((tb, d), BF16), pltpu.VMEM((tb, d), F32), pltpu.VMEM((tb, LANES), F32)],
    )
    return pl.pallas_call(
        _expert_kernel,
        grid_spec=grid_spec,
        out_shape=jax.ShapeDtypeStruct((r, d), F32),
        compiler_params=_cparams(("parallel", "arbitrary", "arbitrary")),
    )(counts, h, rank_row, gate_row, rank_col, wg, wu, wd, x, mod)


def _diff_lambda_init(layer):
    return 0.8 - 0.6 * math.exp(-0.3 * layer)


def _head_gain_rows(q_norm, k_norm, n_q, n_k, n_v):
    hn = jnp.concatenate([jnp.tile(q_norm.astype(F32), n_q // HEAD_DIM),
                          jnp.tile(k_norm.astype(F32), n_k // HEAD_DIM),
                          jnp.ones((n_v,), F32)]).reshape(1, -1)
    isn = jnp.concatenate([jnp.ones((n_q + n_k,), F32), jnp.zeros((n_v,), F32)]).reshape(1, -1)
    return hn, isn


def kernel(x_prompt, x_sample, cache_k_0, cache_v_0, cache_k_1, cache_v_1, cache_k_2, cache_v_2, cache_k_3, cache_v_3, c, c_ctx, norm1_0, w_ada_0, b_ada_0, w_qkv_0, q_norm_0, k_norm_0, rpb_0, w_o_0, norm2_0, w_gate_0, w_up_0, w_down_0, norm1_1, w_ada_1, b_ada_1, w_qkv_1, q_norm_1, k_norm_1, lam_q1_1, lam_k1_1, lam_q2_1, lam_k2_1, subln_1, w_o_1, norm2_1, w_router_1, w_egate_1, w_eup_1, w_edown_1, norm1_2, w_ada_2, b_ada_2, w_qkv_2, q_norm_2, k_norm_2, sink_2, w_o_2, norm2_2, w_gate_2, w_up_2, w_down_2, norm1_3, w_ada_3, b_ada_3, w_qkv_3, q_norm_3, k_norm_3, w_o_3, norm2_3, w_router_3, w_egate_3, w_eup_3, w_edown_3):
    nbc, tc, d = x_prompt.shape
    nbl, tl, _ = x_sample.shape
    n_past = cache_k_0.shape[1]
    assert d % (2 * LANES) == 0 and tc % ATTN_Q_TILE == 0 and tl % MOE_BLOCK == 0
    assert (nbc * tc) % MOE_BLOCK == 0 and tl // GRID_W >= NA_KEY_ROWS and nbl <= 7

    norm1 = (norm1_0, norm1_1, norm1_2, norm1_3)
    norm2 = (norm2_0, norm2_1, norm2_2, norm2_3)
    ada_p = ((w_ada_0, b_ada_0), (w_ada_1, b_ada_1), (w_ada_2, b_ada_2), (w_ada_3, b_ada_3))
    w_qkv = (w_qkv_0, w_qkv_1, w_qkv_2, w_qkv_3)
    qk_norm = ((q_norm_0, k_norm_0), (q_norm_1, k_norm_1), (q_norm_2, k_norm_2), (q_norm_3, k_norm_3))
    w_out = (w_o_0, w_o_1, w_o_2, w_o_3)
    ffn_p = ((w_gate_0, w_up_0, w_down_0), (w_router_1, w_egate_1, w_eup_1, w_edown_1),
             (w_gate_2, w_up_2, w_down_2), (w_router_3, w_egate_3, w_eup_3, w_edown_3))
    caches = ((cache_k_0, cache_v_0), (cache_k_1, cache_v_1), (cache_k_2, cache_v_2), (cache_k_3, cache_v_3))

    xc = x_prompt.reshape(nbc * tc, d)
    xl = x_sample.reshape(nbl * tl, d)
    cond = jnp.concatenate([c_ctx[None, :], c, jnp.zeros((7 - nbl, d), F32)], axis=0)
    cond16 = jnp.concatenate([cond, cond], axis=0)

    def ctx_mod(blk_rows):
        return lambda i: 0

    def lat_mod(blk_rows):
        per = tl // blk_rows
        return lambda i: 1 + i // per

    lam_rows = jnp.zeros((8, LANES), F32)
    for r_, v_ in enumerate((lam_q1_1, lam_k1_1, lam_q2_1, lam_k2_1)):
        lam_rows = lam_rows.at[r_, :HEAD_DIM].set(v_.astype(F32))
    bias_tab = _natten_bias_table(rpb_0, tl // GRID_W)

    states = []
    for l in range(4):
        mixer = l % 4
        mod = ada_mod(cond16, *ada_p[l])
        wq = w_qkv[l].astype(BF16)
        n = wq.shape[1]
        n_q = d
        n_k = d if mixer < 2 else (n - d) // 2
        n_v = n - n_q - n_k
        tn = 1024 if n % 1024 == 0 else 512
        hn, isn = _head_gain_rows(*qk_norm[l], n_q, n_k, n_v)
        rope = None if mixer == 0 else _rope_tables(tl, tn)

        qkv_c = qkv_proj(xc, mod, ctx_mod(ROW_TILE), norm1[l], wq, hn, isn, None, F32, tn)
        qkv_l = qkv_proj(xl, mod, lat_mod(ROW_TILE), norm1[l], wq, hn, isn, rope, BF16, tn)
        states.append((qkv_c[:, n_q:n_q + n_k], qkv_c[:, n_q + n_k:]))

        ck, cv = caches[l]
        cache = (ck.reshape(nbl, n_past, n_k), cv.reshape(nbl, n_past, n_v))
        common = dict(k_col=n_q, v_col=n_q + n_k, d_model=d)
        if mixer == 0:
            o_c = attention(qkv_c, nbc, tc, mode="pair", gqa=False, **common)
            o_l = attention(qkv_l, nbl, tl, mode="pair", gqa=False, cache=cache, bias=bias_tab, **common)
        elif mixer == 1:
            li = _diff_lambda_init(l)
            sub = subln_1.astype(F32).reshape(1, LANES)
            o_c = attention(qkv_c, nbc, tc, mode="diff", gqa=False, lam=lam_rows, subln=sub, lam_init=li, **common)
            o_l = attention(qkv_l, nbl, tl, mode="diff", gqa=False, cache=cache, lam=lam_rows, subln=sub,
                            lam_init=li, **common)
        elif mixer == 2:
            sk = sink_2.astype(F32)
            o_c = attention(qkv_c, nbc, tc, mode="pair", gqa=True, sink=sk, **common)
            o_l = attention(qkv_l, nbl, tl, mode="pair", gqa=True, cache=cache, band=True, sink=sk, **common)
        else:
            o_c = attention(qkv_c, nbc, tc, mode="pair", gqa=True, **common)
            o_l = attention(qkv_l, nbl, tl, mode="pair", gqa=True, cache=cache, **common)

        wo = w_out[l].astype(BF16)
        xc = proj_residual(o_c, wo, xc, mod, ctx_mod(ROW_TILE))
        xl = proj_residual(o_l, wo, xl, mod, lat_mod(ROW_TILE))

        if l % 2 == 0:
            wg, wu, wd = (w.astype(BF16) for w in ffn_p[l])
            xc = dense_ffn(xc, mod, ctx_mod(ROW_TILE), norm2[l], wg, wu, wd)
            xl = dense_ffn(xl, mod, lat_mod(ROW_TILE), norm2[l], wg, wu, wd)
        else:
            wr = ffn_p[l][0]
            wg, wu, wd = (w.astype(BF16) for w in ffn_p[l][1:])
            xc = moe_ffn(xc, mod, ctx_mod(MOE_BLOCK), norm2[l], wr, wg, wu, wd)
            xl = moe_ffn(xl, mod, lat_mod(MOE_BLOCK), norm2[l], wr, wg, wu, wd)

    nh = d // HEAD_DIM
    k0, v0 = states[0]
    k1, v1 = states[1]
    k2, v2 = states[2]
    k3, v3 = states[3]
    nkv = k2.shape[1] // HEAD_DIM
    return (xc.reshape(nbc, tc, d), xl.reshape(nbl, tl, d),
            k0.reshape(nbc, tc, nh, HEAD_DIM), v0.reshape(nbc, tc, nh, HEAD_DIM),
            k1.reshape(nbc, tc, nh // 2, 2, HEAD_DIM), v1.reshape(nbc, tc, nh // 2, 2 * HEAD_DIM),
            k2.reshape(nbc, tc, nkv, HEAD_DIM), v2.reshape(nbc, tc, nkv, HEAD_DIM),
            k3.reshape(nbc, tc, nkv, HEAD_DIM), v3.reshape(nbc, tc, nkv, HEAD_DIM))
```

```python
import functools
import math

import numpy as np
import jax
import jax.numpy as jnp
from jax import lax
from jax.experimental import pallas as pl
from jax.experimental.pallas import tpu as pltpu

F32 = jnp.float32
BF16 = jnp.bfloat16

HEAD_DIM = 64
GRID_W = 64
NA_ROWS = 8
NA_COLS = 16
WINDOW = 128
ROPE_BASE = 10000.0
ROPE_PAIRS_AXIS = HEAD_DIM // 4
N_EXPERTS = 8
EPS = 1e-6
NEG_INF = -1e30
QK_SCALE = HEAD_DIM ** -0.5
LOG2E = math.log2(math.e)

LANES = 128
V7X_VMEM_BYTES = 64 * 1024 * 1024
VMEM_LIMIT = V7X_VMEM_BYTES - 8 * 1024 * 1024

ROW_TILE = 512
ATTN_Q_TILE = 256
NA_KEY_ROWS = 12
MOE_BLOCK = 1024
MOE_SUB = 256


def _cparams(sem):
    return pltpu.CompilerParams(dimension_semantics=sem, vmem_limit_bytes=VMEM_LIMIT)


def _silu(x):
    return x * (1.0 / (1.0 + jnp.exp(-x)))


def _norm_mod(x, g, shift, scale):
    ms = jnp.mean(x * x, axis=-1, keepdims=True)
    y = x * lax.rsqrt(ms + EPS) * g
    return y * (1.0 + scale) + shift


def _dot(a, b):
    return jnp.dot(a, b, preferred_element_type=F32)


def _dot_nt(a, b):
    return lax.dot_general(a, b, (((1,), (1,)), ((), ())), preferred_element_type=F32)


def _ada_kernel(c_ref, w_ref, b_ref, o_ref):
    a = _silu(c_ref[...])
    a_hi = a.astype(BF16)
    a_lo = a - a_hi.astype(F32)
    row = lax.broadcasted_iota(jnp.int32, a.shape, 0)
    lhs = jnp.where(row < 8, a_hi.astype(F32), a_lo).astype(BF16)
    w = w_ref[...]
    w_hi = w.astype(BF16)
    w_lo = (w - w_hi.astype(F32)).astype(BF16)
    r = _dot(lhs, w_hi)
    r2 = _dot(a_hi, w_lo)
    o_ref[...] = r[:8] + r[8:] + r2[:8] + b_ref[...]


def ada_mod(cond16, w_ada, b_ada):
    d, n = w_ada.shape
    tn = 1024
    out = pl.pallas_call(
        _ada_kernel,
        grid=(n // tn,),
        in_specs=[pl.BlockSpec((16, d), lambda j: (0, 0)),
                  pl.BlockSpec((d, tn), lambda j: (0, j)),
                  pl.BlockSpec((1, tn), lambda j: (0, j))],
        out_specs=pl.BlockSpec((8, tn), lambda j: (0, j)),
        out_shape=jax.ShapeDtypeStruct((8, n), F32),
        compiler_params=_cparams(("arbitrary",)),
    )(cond16, w_ada, b_ada.reshape(1, n))
    return out.reshape(8, 6, d)


def _qkv_kernel(*refs, rope):
    if rope:
        x_ref, mod_ref, g_ref, w_ref, seg_ref, hn_ref, isn_ref, cos_ref, sin_ref, o_ref, h_scr = refs
    else:
        x_ref, mod_ref, g_ref, w_ref, seg_ref, hn_ref, isn_ref, o_ref, h_scr = refs

    @pl.when(pl.program_id(1) == 0)
    def _():
        h = _norm_mod(x_ref[...], g_ref[...], mod_ref[0, 0:1, :], mod_ref[0, 1:2, :])
        h_scr[...] = h.astype(BF16)

    acc = _dot(h_scr[...], w_ref[...])
    tn = acc.shape[1]
    sq = (acc * acc).astype(BF16)
    seg = seg_ref[...]
    ssum = jnp.concatenate(
        [_dot(sq[:, c * 2 * LANES:(c + 1) * 2 * LANES], seg) for c in range(tn // (2 * LANES))], axis=1)
    y = acc * lax.rsqrt(ssum * (1.0 / HEAD_DIM) + EPS) * hn_ref[...]
    if rope:
        lane = lax.broadcasted_iota(jnp.int32, (acc.shape[0], LANES), 1)
        even = (lane % 2) == 0
        parts = []
        for c in range(tn // LANES):
            yc = y[:, c * LANES:(c + 1) * LANES]
            nxt = pltpu.roll(yc, LANES - 1, 1)
            prv = pltpu.roll(yc, 1, 1)
            parts.append(jnp.where(even, nxt, prv))
        swapped = jnp.concatenate(parts, axis=1)
        y = y * cos_ref[...] + swapped * sin_ref[...]
    o_ref[...] = jnp.where(isn_ref[...] > 0.5, y, acc).astype(o_ref.dtype)


def _seg_matrix():
    i = np.arange(2 * LANES)
    return jnp.asarray((i[:, None] // HEAD_DIM) == (i[None, :] // HEAD_DIM), BF16)


def qkv_proj(x, mod, mod_of_block, g, w, hn, isn, rope_tabs, out_dtype, tn):
    r, d = x.shape
    n = w.shape[1]
    tm = ROW_TILE
    rope = rope_tabs is not None
    in_specs = [pl.BlockSpec((tm, d), lambda i, j: (i, 0)),
                pl.BlockSpec((1, 6, d), lambda i, j: (mod_of_block(i), 0, 0)),
                pl.BlockSpec((1, d), lambda i, j: (0, 0)),
                pl.BlockSpec((d, tn), lambda i, j: (0, j)),
                pl.BlockSpec((2 * LANES, 2 * LANES), lambda i, j: (0, 0)),
                pl.BlockSpec((1, tn), lambda i, j: (0, j)),
                pl.BlockSpec((1, tn), lambda i, j: (0, j))]
    args = [x, mod, g.reshape(1, d), w, _seg_matrix(), hn, isn]
    if rope:
        cos_t, sin_t = rope_tabs
        nblk = cos_t.shape[0] // tm
        in_specs += [pl.BlockSpec((tm, tn), lambda i, j: (i % nblk, 0)),
                     pl.BlockSpec((tm, tn), lambda i, j: (i % nblk, 0))]
        args += [cos_t, sin_t]
    return pl.pallas_call(
        functools.partial(_qkv_kernel, rope=rope),
        grid=(r // tm, n // tn),
        in_specs=in_specs,
        out_specs=pl.BlockSpec((tm, tn), lambda i, j: (i, j)),
        out_shape=jax.ShapeDtypeStruct((r, n), out_dtype),
        scratch_shapes=[pltpu.VMEM((tm, d), BF16)],
        compiler_params=_cparams(("parallel", "arbitrary")),
    )(*args)


def _rope_tables(t, tn):
    pos = np.arange(t)
    row = (pos // GRID_W).astype(np.float32)
    col = (pos % GRID_W).astype(np.float32)
    freqs = jnp.asarray(ROPE_BASE, F32) ** (-jnp.arange(ROPE_PAIRS_AXIS, dtype=F32) / ROPE_PAIRS_AXIS)
    ang = jnp.concatenate([jnp.asarray(row)[:, None] * freqs, jnp.asarray(col)[:, None] * freqs], axis=-1)
    cos = jnp.repeat(jnp.cos(ang), 2, axis=-1)
    sin = jnp.repeat(jnp.sin(ang), 2, axis=-1)
    sign = jnp.asarray(np.where(np.arange(HEAD_DIM) % 2 == 0, -1.0, 1.0), F32)
    reps = tn // HEAD_DIM
    return jnp.tile(cos, (1, reps)), jnp.tile(sin * sign, (1, reps))


def _attn_kernel(*refs, mode, gqa, t_loc, n_cache, win, band, natten, has_sink, lam_init, n_sub):
    it = iter(refs)
    sink_ref = next(it) if has_sink else None
    q_ref, k_ref, v_ref = next(it), next(it), next(it)
    ck_ref = cv_ref = bp_ref = lam_ref = subln_ref = None
    if n_cache:
        ck_ref, cv_ref = next(it), next(it)
    if natten:
        bp_ref = next(it)
    if mode == "diff":
        lam_ref, subln_ref = next(it), next(it)
    o_ref, kf = next(it), next(it)
    v_scr = [next(it), next(it)] if mode == "pair" else [next(it)]

    tq = ATTN_Q_TILE
    nk = t_loc + n_cache
    rows = t_loc // GRID_W
    qrows = tq // GRID_W
    j = pl.program_id(1)
    qb = pl.program_id(2)

    def place(a):
        if not gqa:
            return a
        half = (j // 2) % 2
        lane_half = (lax.broadcasted_iota(jnp.int32, a.shape, 1) >= HEAD_DIM).astype(jnp.int32)
        return jnp.where(lane_half == half, a, pltpu.roll(a, HEAD_DIM, 1))

    def put_v(lo, hi, a):
        if mode == "pair":
            ln = lax.broadcasted_iota(jnp.int32, a.shape, 1)
            v_scr[0][lo:hi, :] = jnp.where(ln < HEAD_DIM, a, 1.0).astype(BF16)
            v_scr[1][lo:hi, :] = jnp.where(ln >= HEAD_DIM, a, 1.0).astype(BF16)
        else:
            v_scr[0][lo:hi, :] = a.astype(BF16)

    @pl.when(qb == 0)
    def _():
        kf[0:t_loc, :] = place(k_ref[...].astype(F32)).astype(BF16)
        put_v(0, t_loc, place(v_ref[...].astype(F32)))
        if n_cache:
            kf[t_loc:nk, :] = place(ck_ref[0]).astype(BF16)
            put_v(t_loc, nk, place(cv_ref[0]))

    lane = lax.broadcasted_iota(jnp.int32, (tq, LANES), 1)
    if mode == "diff":
        lp = lam_ref[...]
        lam = (jnp.exp(jnp.sum(lp[0:1] * lp[1:2], axis=1, keepdims=True))
               - jnp.exp(jnp.sum(lp[2:3] * lp[3:4], axis=1, keepdims=True)) + lam_init)

    for t in range(n_sub):
        g = qb * n_sub + t
        q2 = q_ref[t * tq:(t + 1) * tq, :].astype(F32) * (QK_SCALE * LOG2E)
        keep = None
        if natten:
            ws_row = jnp.clip(g * qrows - NA_ROWS // 2, 0, rows - NA_KEY_ROWS)
            sl = pl.ds(pl.multiple_of(ws_row * GRID_W, GRID_W), win)
            q_row = lax.broadcasted_iota(jnp.int32, (tq, win), 0) // GRID_W + g * qrows
            r0 = jnp.clip(q_row - NA_ROWS // 2, 0, rows - NA_ROWS)
            k_row = lax.broadcasted_iota(jnp.int32, (tq, win), 1) // GRID_W + ws_row
            keep = (k_row >= r0) & (k_row < r0 + NA_ROWS)
        elif band:
            ws = pl.multiple_of(jnp.clip(g * tq - WINDOW, 0, t_loc - win), WINDOW)
            sl = pl.ds(ws, win)
            qpos = g * tq + lax.broadcasted_iota(jnp.int32, (tq, win), 0)
            kpos = ws + lax.broadcasted_iota(jnp.int32, (tq, win), 1)
            keep = jnp.abs(qpos - kpos) <= WINDOW
        else:
            sl = slice(0, win)
        k_loc = kf[sl, :]

        chains = []
        for hh in range(2):
            in_half = (lane < HEAD_DIM) if hh == 0 else (lane >= HEAD_DIM)
            qh = jnp.where(in_half, q2, 0.0).astype(BF16)
            s = _dot_nt(qh, k_loc)
            if natten:
                blocks = []
                for a in range(qrows):
                    first = ws_row - (g * qrows + a) + NA_ROWS
                    blocks.append(jnp.concatenate(
                        [bp_ref[hh, pl.ds(jnp.clip(first + 2 * p, 0, 2 * NA_ROWS - 1), 1)][0]
                         for p in range(win // LANES)], axis=1))
                s = s + jnp.concatenate(blocks, axis=0)
            if keep is not None:
                s = jnp.where(keep, s, NEG_INF)
            m = jnp.max(s, axis=1, keepdims=True)
            if n_cache:
                s_c = _dot_nt(qh, kf[t_loc:nk, :])
                m = jnp.maximum(m, jnp.max(s_c, axis=1, keepdims=True))
            if has_sink:
                sink = sink_ref[2 * j + hh] * LOG2E
                m = jnp.maximum(m, sink)
            e = jnp.exp2(s - m)
            e_c = jnp.exp2(s_c - m) if n_cache else None
            if mode == "pair":
                o = _dot(e.astype(BF16), v_scr[hh][sl, :])
                if n_cache:
                    o = o + _dot(e_c.astype(BF16), v_scr[hh][t_loc:nk, :])
                den = pltpu.roll(o, HEAD_DIM, 1)
                if has_sink:
                    den = den + jnp.exp2(sink - m)
                chains.append(o * (1.0 / den))
            else:
                den = jnp.sum(e, axis=1, keepdims=True)
                if n_cache:
                    den = den + jnp.sum(e_c, axis=1, keepdims=True)
                chains.append((e, e_c, 1.0 / den))

        if mode == "pair":
            out = jnp.where(lane < HEAD_DIM, chains[0], chains[1])
        else:
            (e0, ec0, w0), (e1, ec1, w1) = chains
            w1 = w1 * lam
            out = _dot((e0 * w0 - e1 * w1).astype(BF16), v_scr[0][sl, :])
            if n_cache:
                out = out + _dot((ec0 * w0 - ec1 * w1).astype(BF16), v_scr[0][t_loc:nk, :])
            ms = jnp.mean(out * out, axis=-1, keepdims=True)
            out = out * lax.rsqrt(ms + EPS) * subln_ref[...] * (1.0 - lam_init)
        o_ref[t * tq:(t + 1) * tq, :] = out.astype(o_ref.dtype)


def attention(qkv, nb, t_loc, *, mode, gqa, k_col, v_col, n_sub, cache=None, band=False, bias_pairs=None,
              sink=None, lam=None, subln=None, lam_init=0.0, d_model):
    tq = ATTN_Q_TILE * n_sub
    n_qlb = d_model // LANES
    natten = bias_pairs is not None
    n_cache = 0 if cache is None else cache[0].shape[1]
    if natten:
        win = NA_KEY_ROWS * GRID_W
    elif band:
        win = ATTN_Q_TILE + 2 * WINDOW
    else:
        win = t_loc
    kmap = (lambda j: j // 4) if gqa else (lambda j: j)
    kc, vc = k_col // LANES, v_col // LANES
    nqb = t_loc // tq
    nk = t_loc + n_cache

    in_specs = []
    args = []
    if sink is not None:
        in_specs.append(pl.BlockSpec(memory_space=pltpu.SMEM))
        args.append(sink)
    in_specs += [pl.BlockSpec((tq, LANES), lambda b, j, i: (b * nqb + i, j)),
                 pl.BlockSpec((t_loc, LANES), lambda b, j, i: (b, kc + kmap(j))),
                 pl.BlockSpec((t_loc, LANES), lambda b, j, i: (b, vc + kmap(j)))]
    args += [qkv, qkv, qkv]
    if cache is not None:
        in_specs += [pl.BlockSpec((1, n_cache, LANES), lambda b, j, i: (b, 0, kmap(j))),
                     pl.BlockSpec((1, n_cache, LANES), lambda b, j, i: (b, 0, kmap(j)))]
        args += list(cache)
    if natten:
        in_specs.append(pl.BlockSpec((2, 2 * NA_ROWS, GRID_W, LANES), lambda b, j, i: (j, 0, 0, 0)))
        args.append(bias_pairs)
    if mode == "diff":
        in_specs += [pl.BlockSpec((8, LANES), lambda b, j, i: (0, 0)),
                     pl.BlockSpec((1, LANES), lambda b, j, i: (0, 0))]
        args += [lam, subln]
    kern = functools.partial(
        _attn_kernel, mode=mode, gqa=gqa, t_loc=t_loc, n_cache=n_cache, win=win, band=band,
        natten=natten, has_sink=sink is not None, lam_init=lam_init, n_sub=n_sub)
    n_v = 2 if mode == "pair" else 1
    return pl.pallas_call(
        kern,
        grid=(nb, n_qlb, nqb),
        in_specs=in_specs,
        out_specs=pl.BlockSpec((tq, LANES), lambda b, j, i: (b * nqb + i, j)),
        out_shape=jax.ShapeDtypeStruct((nb * t_loc, d_model), BF16),
        scratch_shapes=[pltpu.VMEM((nk, LANES), BF16) for _ in range(1 + n_v)],
        compiler_params=_cparams(("parallel", "arbitrary", "arbitrary")),
    )(*args)


def _natten_pair_table(rpb):
    h = rpb.shape[0]
    pad = GRID_W - NA_COLS
    ext = jnp.concatenate([jnp.repeat(rpb[..., :1], pad, axis=-1), rpb,
                           jnp.repeat(rpb[..., -1:], pad, axis=-1)], axis=-1).astype(F32)
    b = jnp.stack([ext[..., GRID_W - 1 - qc:2 * GRID_W - 1 - qc] for qc in range(GRID_W)], axis=2)
    col = np.arange(GRID_W)
    cstart = np.clip(col - NA_COLS // 2, 0, GRID_W - NA_COLS)
    cmask = (col[None, :] >= cstart[:, None]) & (col[None, :] < cstart[:, None] + NA_COLS)
    b = jnp.where(jnp.asarray(cmask), b * LOG2E, NEG_INF)
    neg = jnp.full((h, 1, GRID_W, GRID_W), NEG_INF, F32)
    bpad = jnp.concatenate([neg, b, neg], axis=1)
    return jnp.concatenate([bpad[:, :-1], bpad[:, 1:]], axis=-1)


def _proj_res_kernel(a_ref, w_ref, x_ref, mod_ref, o_ref):
    o_ref[...] = x_ref[...] + mod_ref[0, 2:3, :] * _dot(a_ref[...], w_ref[...])


def proj_residual(a, w, x, mod, mod_of_block):
    r, d = x.shape
    tm = ROW_TILE
    return pl.pallas_call(
        _proj_res_kernel,
        grid=(r // tm,),
        in_specs=[pl.BlockSpec((tm, a.shape[1]), lambda i: (i, 0)),
                  pl.BlockSpec(w.shape, lambda i: (0, 0)),
                  pl.BlockSpec((tm, d), lambda i: (i, 0)),
                  pl.BlockSpec((1, 6, d), lambda i: (mod_of_block(i), 0, 0))],
        out_specs=pl.BlockSpec((tm, d), lambda i: (i, 0)),
        out_shape=jax.ShapeDtypeStruct((r, d), F32),
        compiler_params=_cparams(("parallel",)),
    )(a, w, x, mod)


def _ffn_kernel(x_ref, mod_ref, g_ref, wg_ref, wu_ref, wd_ref, o_ref, h_scr, acc_scr):
    f = pl.program_id(1)

    @pl.when(f == 0)
    def _():
        h = _norm_mod(x_ref[...], g_ref[...], mod_ref[0, 3:4, :], mod_ref[0, 4:5, :])
        h_scr[...] = h.astype(BF16)

    h = h_scr[...]
    t = (_silu(_dot(h, wg_ref[...])) * _dot(h, wu_ref[...])).astype(BF16)
    y = _dot(t, wd_ref[...])

    @pl.when(f == 0)
    def _():
        acc_scr[...] = y

    @pl.when(f > 0)
    def _():
        acc_scr[...] += y

    @pl.when(f == pl.num_programs(1) - 1)
    def _():
        o_ref[...] = x_ref[...] + mod_ref[0, 5:6, :] * acc_scr[...]


def dense_ffn(x, mod, mod_of_block, g, wg, wu, wd):
    r, d = x.shape
    ff = wg.shape[1]
    tm = ROW_TILE
    nf = 2
    tf = ff // nf
    return pl.pallas_call(
        _ffn_kernel,
        grid=(r // tm, nf),
        in_specs=[pl.BlockSpec((tm, d), lambda i, f: (i, 0)),
                  pl.BlockSpec((1, 6, d), lambda i, f: (mod_of_block(i), 0, 0)),
                  pl.BlockSpec((1, d), lambda i, f: (0, 0)),
                  pl.BlockSpec((d, tf), lambda i, f: (0, f)),
                  pl.BlockSpec((d, tf), lambda i, f: (0, f)),
                  pl.BlockSpec((tf, d), lambda i, f: (f, 0))],
        out_specs=pl.BlockSpec((tm, d), lambda i, f: (i, 0)),
        out_shape=jax.ShapeDtypeStruct((r, d), F32),
        scratch_shapes=[pltpu.VMEM((tm, d), BF16), pltpu.VMEM((tm, d), F32)],
        compiler_params=_cparams(("parallel", "arbitrary")),
    )(x, mod, g.reshape(1, d), wg, wu, wd)


def _router_kernel(x_ref, mod_ref, g_ref, wr_ref, h_ref, rank_row_ref, gate_row_ref, rank_col_ref, cnt_ref):
    h = _norm_mod(x_ref[...], g_ref[...], mod_ref[0, 3:4, :], mod_ref[0, 4:5, :])
    h_hi = h.astype(BF16)
    h_ref[...] = h_hi
    h_lo = (h - h_hi.astype(F32)).astype(BF16)
    wr = wr_ref[...]
    w_hi = wr.astype(BF16)
    w_lo = (wr - w_hi.astype(F32)).astype(BF16)
    logits = _dot(h_hi, w_hi) + (_dot(h_lo, w_hi) + _dot(h_hi, w_lo))
    tb = logits.shape[0]
    lane = lax.broadcasted_iota(jnp.int32, logits.shape, 1)
    lane_f = lane.astype(F32)
    logits = jnp.where(lane < N_EXPERTS, logits, -jnp.inf)
    m1 = jnp.max(logits, axis=1, keepdims=True)
    i1 = jnp.min(jnp.where(logits == m1, lane_f, float(LANES)), axis=1, keepdims=True)
    rest = jnp.where(lane_f == i1, -jnp.inf, logits)
    m2 = jnp.max(rest, axis=1, keepdims=True)
    i2 = jnp.min(jnp.where(rest == m2, lane_f, float(LANES)), axis=1, keepdims=True)
    e2 = jnp.exp(m2 - m1)
    inv = 1.0 / (1.0 + e2)
    is1 = lane_f == i1
    is2 = lane_f == i2
    gates = jnp.where(is1, inv, 0.0) + jnp.where(is2, e2 * inv, 0.0)
    sel = jnp.where(is1 | is2, 1.0, 0.0)
    sel_b = sel.astype(BF16)
    chunk = 256
    parts = []
    for c in range(tb // chunk):
        ri = lax.broadcasted_iota(jnp.int32, (chunk, tb), 0) + c * chunk
        ci = lax.broadcasted_iota(jnp.int32, (chunk, tb), 1)
        parts.append(_dot(jnp.where(ci < ri, 1.0, 0.0).astype(BF16), sel_b))
    rank = jnp.concatenate(parts, axis=0)
    rank_sel = jnp.where(sel > 0.5, rank, -1.0)
    rank_col_ref[...] = rank_sel
    rank_row_ref[0] = rank_sel.T[:N_EXPERTS]
    gate_row_ref[0] = gates.T[:N_EXPERTS]
    cnt_ref[0] = jnp.broadcast_to(jnp.sum(sel, axis=0, keepdims=True), (8, LANES)).astype(jnp.int32)


def _expert_kernel(cnt_ref, h_ref, rank_row_ref, gate_row_ref, rank_col_ref, wg_ref, wu_ref, wd_ref,
                   x_ref, mod_ref, o_ref, xg, yacc, gs):
    b, e, f = pl.program_id(0), pl.program_id(1), pl.program_id(2)
    nf = pl.num_programs(2)
    tb = h_ref.shape[0]
    m = MOE_SUB
    n_rows = cnt_ref[b * N_EXPERTS + e]
    nsub = (n_rows + m - 1) // m

    @pl.when((e == 0) & (f == 0))
    def _():
        o_ref[...] = jnp.zeros_like(o_ref)

    @pl.when(f == 0)
    def _():
        rank_e = rank_row_ref[0, pl.ds(e, 1), :]
        gate_e = gate_row_ref[0, pl.ds(e, 1), :]

        def gather(i, carry):
            base = pl.multiple_of(i * m, m)
            want = (lax.broadcasted_iota(jnp.int32, (m, tb), 0) + base).astype(F32)
            p = jnp.where(rank_e == want, 1.0, 0.0)
            xg[pl.ds(base, m), :] = _dot(p.astype(BF16), h_ref[...]).astype(BF16)
            gs[pl.ds(base, m), :] = jnp.broadcast_to(jnp.sum(p * gate_e, axis=1, keepdims=True), (m, LANES))
            return carry

        lax.fori_loop(0, nsub, gather, 0)

    def ffn(i, carry):
        base = pl.multiple_of(i * m, m)
        xr = xg[pl.ds(base, m), :]
        t = (_silu(_dot(xr, wg_ref[0])) * _dot(xr, wu_ref[0])).astype(BF16)
        y = _dot(t, wd_ref[0])

        @pl.when(f == 0)
        def _():
            yacc[pl.ds(base, m), :] = y

        @pl.when(f > 0)
        def _():
            yacc[pl.ds(base, m), :] += y

        return carry

    lax.fori_loop(0, nsub, ffn, 0)

    @pl.when(f == nf - 1)
    def _():
        lane = lax.broadcasted_iota(jnp.int32, (tb, LANES), 1)
        rank_c = jnp.sum(jnp.where(lane == e, rank_col_ref[...], 0.0), axis=1, keepdims=True)

        def scatter(i, carry):
            base = pl.multiple_of(i * m, m)
            want = (lax.broadcasted_iota(jnp.int32, (tb, m), 1) + base).astype(F32)
            pt = jnp.where(rank_c == want, 1.0, 0.0).astype(BF16)
            yg = (yacc[pl.ds(base, m), :] * gs[pl.ds(base, m), 0:1]).astype(BF16)
            o_ref[...] += _dot(pt, yg)
            return carry

        lax.fori_loop(0, nsub, scatter, 0)

    @pl.when((e == N_EXPERTS - 1) & (f == nf - 1))
    def _():
        o_ref[...] = x_ref[...] + mod_ref[0, 5:6, :] * o_ref[...]


def moe_ffn(x, mod, mod_of_block, g, w_router, wg, wu, wd):
    r, d = x.shape
    ff = wg.shape[2]
    tb = MOE_BLOCK
    nb = r // tb
    wr = jnp.pad(w_router, ((0, 0), (0, LANES - N_EXPERTS)))
    h, rank_row, gate_row, rank_col, cnt = pl.pallas_call(
        _router_kernel,
        grid=(nb,),
        in_specs=[pl.BlockSpec((tb, d), lambda i: (i, 0)),
                  pl.BlockSpec((1, 6, d), lambda i: (mod_of_block(i), 0, 0)),
                  pl.BlockSpec((1, d), lambda i: (0, 0)),
                  pl.BlockSpec((d, LANES), lambda i: (0, 0))],
        out_specs=[pl.BlockSpec((tb, d), lambda i: (i, 0)),
                   pl.BlockSpec((1, N_EXPERTS, tb), lambda i: (i, 0, 0)),
                   pl.BlockSpec((1, N_EXPERTS, tb), lambda i: (i, 0, 0)),
                   pl.BlockSpec((tb, LANES), lambda i: (i, 0)),
                   pl.BlockSpec((1, 8, LANES), lambda i: (i, 0, 0))],
        out_shape=[jax.ShapeDtypeStruct((r, d), BF16),
                   jax.ShapeDtypeStruct((nb, N_EXPERTS, tb), F32),
                   jax.ShapeDtypeStruct((nb, N_EXPERTS, tb), F32),
                   jax.ShapeDtypeStruct((r, LANES), F32),
                   jax.ShapeDtypeStruct((nb, 8, LANES), jnp.int32)],
        compiler_params=_cparams(("parallel",)),
    )(x, mod, g.reshape(1, d), wr)
    counts = cnt[:, 0, :N_EXPERTS].reshape(nb * N_EXPERTS)

    nf = 2
    tf = ff // nf
    grid_spec = pltpu.PrefetchScalarGridSpec(
        num_scalar_prefetch=1,
        grid=(nb, N_EXPERTS, nf),
        in_specs=[pl.BlockSpec((tb, d), lambda i, e, f, c: (i, 0)),
                  pl.BlockSpec((1, N_EXPERTS, tb), lambda i, e, f, c: (i, 0, 0)),
                  pl.BlockSpec((1, N_EXPERTS, tb), lambda i, e, f, c: (i, 0, 0)),
                  pl.BlockSpec((tb, LANES), lambda i, e, f, c: (i, 0)),
                  pl.BlockSpec((1, d, tf), lambda i, e, f, c: (e, 0, f)),
                  pl.BlockSpec((1, d, tf), lambda i, e, f, c: (e, 0, f)),
                  pl.BlockSpec((1, tf, d), lambda i, e, f, c: (e, f, 0)),
                  pl.BlockSpec((tb, d), lambda i, e, f, c: (i, 0)),
                  pl.BlockSpec((1, 6, d), lambda i, e, f, c: (mod_of_block(i), 0, 0))],
        out_specs=pl.BlockSpec((tb, d), lambda i, e, f, c: (i, 0)),
        scratch_shapes=[pltpu.VMEM((tb, d), BF16), pltpu.VMEM((tb, d), F32), pltpu.VMEM((tb, LANES), F32)],
    )
    return pl.pallas_call(
        _expert_kernel,
        grid_spec=grid_spec,
        out_shape=jax.ShapeDtypeStruct((r, d), F32),
        compiler_params=_cparams(("parallel", "arbitrary", "arbitrary")),
    )(counts, h, rank_row, gate_row, rank_col, wg, wu, wd, x, mod)


def _diff_lambda_init(layer):
    return 0.8 - 0.6 * math.exp(-0.3 * layer)


def _head_gain_rows(q_norm, k_norm, n_q, n_k, n_v):
    hn = jnp.concatenate([jnp.tile(q_norm.astype(F32), n_q // HEAD_DIM),
                          jnp.tile(k_norm.astype(F32), n_k // HEAD_DIM),
                          jnp.ones((n_v,), F32)]).reshape(1, -1)
    isn = jnp.concatenate([jnp.ones((n_q + n_k,), F32), jnp.zeros((n_v,), F32)]).reshape(1, -1)
    return hn, isn


def kernel(x_prompt, x_sample, cache_k_0, cache_v_0, cache_k_1, cache_v_1, cache_k_2, cache_v_2, cache_k_3, cache_v_3, c, c_ctx, norm1_0, w_ada_0, b_ada_0, w_qkv_0, q_norm_0, k_norm_0, rpb_0, w_o_0, norm2_0, w_gate_0, w_up_0, w_down_0, norm1_1, w_ada_1, b_ada_1, w_qkv_1, q_norm_1, k_norm_1, lam_q1_1, lam_k1_1, lam_q2_1, lam_k2_1, subln_1, w_o_1, norm2_1, w_router_1, w_egate_1, w_eup_1, w_edown_1, norm1_2, w_ada_2, b_ada_2, w_qkv_2, q_norm_2, k_norm_2, sink_2, w_o_2, norm2_2, w_gate_2, w_up_2, w_down_2, norm1_3, w_ada_3, b_ada_3, w_qkv_3, q_norm_3, k_norm_3, w_o_3, norm2_3, w_router_3, w_egate_3, w_eup_3, w_edown_3):
    nbc, tc, d = x_prompt.shape
    nbl, tl, _ = x_sample.shape
    n_past = cache_k_0.shape[1]
    assert d % (2 * LANES) == 0 and tc % ATTN_Q_TILE == 0 and tl % MOE_BLOCK == 0
    assert (nbc * tc) % MOE_BLOCK == 0 and tl // GRID_W >= NA_KEY_ROWS and nbl <= 7

    norm1 = (norm1_0, norm1_1, norm1_2, norm1_3)
    norm2 = (norm2_0, norm2_1, norm2_2, norm2_3)
    ada_p = ((w_ada_0, b_ada_0), (w_ada_1, b_ada_1), (w_ada_2, b_ada_2), (w_ada_3, b_ada_3))
    w_qkv = (w_qkv_0, w_qkv_1, w_qkv_2, w_qkv_3)
    qk_norm = ((q_norm_0, k_norm_0), (q_norm_1, k_norm_1), (q_norm_2, k_norm_2), (q_norm_3, k_norm_3))
    w_out = (w_o_0, w_o_1, w_o_2, w_o_3)
    ffn_p = ((w_gate_0, w_up_0, w_down_0), (w_router_1, w_egate_1, w_eup_1, w_edown_1),
             (w_gate_2, w_up_2, w_down_2), (w_router_3, w_egate_3, w_eup_3, w_edown_3))
    caches = ((cache_k_0, cache_v_0), (cache_k_1, cache_v_1), (cache_k_2, cache_v_2), (cache_k_3, cache_v_3))

    xc = x_prompt.reshape(nbc * tc, d)
    xl = x_sample.reshape(nbl * tl, d)
    cond = jnp.concatenate([c_ctx[None, :], c, jnp.zeros((7 - nbl, d), F32)], axis=0)
    cond16 = jnp.concatenate([cond, cond], axis=0)

    def ctx_mod(blk_rows):
        return lambda i: 0

    def lat_mod(blk_rows):
        per = tl // blk_rows
        return lambda i: 1 + i // per

    lam_rows = jnp.zeros((8, LANES), F32)
    for r_, v_ in enumerate((lam_q1_1, lam_k1_1, lam_q2_1, lam_k2_1)):
        lam_rows = lam_rows.at[r_, :HEAD_DIM].set(v_.astype(F32))
    bias_pairs = _natten_pair_table(rpb_0)

    states = []
    for l in range(4):
        mixer = l % 4
        mod = ada_mod(cond16, *ada_p[l])
        wq = w_qkv[l].astype(BF16)
        n = wq.shape[1]
        n_q = d
        n_k = d if mixer < 2 else (n - d) // 2
        n_v = n - n_q - n_k
        tn = 1024 if n % 1024 == 0 else 512
        hn, isn = _head_gain_rows(*qk_norm[l], n_q, n_k, n_v)
        rope = None if mixer == 0 else _rope_tables(tl, tn)

        qkv_c = qkv_proj(xc, mod, ctx_mod(ROW_TILE), norm1[l], wq, hn, isn, None, F32, tn)
        qkv_l = qkv_proj(xl, mod, lat_mod(ROW_TILE), norm1[l], wq, hn, isn, rope, BF16, tn)
        states.append((qkv_c[:, n_q:n_q + n_k], qkv_c[:, n_q + n_k:]))

        ck, cv = caches[l]
        cache = (ck.reshape(nbl, n_past, n_k), cv.reshape(nbl, n_past, n_v))
        common = dict(k_col=n_q, v_col=n_q + n_k, d_model=d)
        if mixer == 0:
            o_c = attention(qkv_c, nbc, tc, mode="pair", gqa=False, n_sub=1, **common)
            o_l = attention(qkv_l, nbl, tl, mode="pair", gqa=False, n_sub=2, cache=cache, bias_pairs=bias_pairs,
                            **common)
        elif mixer == 1:
            li = _diff_lambda_init(l)
            sub = subln_1.astype(F32).reshape(1, LANES)
            o_c = attention(qkv_c, nbc, tc, mode="diff", gqa=False, n_sub=1, lam=lam_rows, subln=sub, lam_init=li,
                            **common)
            o_l = attention(qkv_l, nbl, tl, mode="diff", gqa=False, n_sub=2, cache=cache, lam=lam_rows, subln=sub,
                            lam_init=li, **common)
        elif mixer == 2:
            sk = sink_2.astype(F32)
            o_c = attention(qkv_c, nbc, tc, mode="pair", gqa=True, n_sub=1, sink=sk, **common)
            o_l = attention(qkv_l, nbl, tl, mode="pair", gqa=True, n_sub=2, cache=cache, band=True, sink=sk, **common)
        else:
            o_c = attention(qkv_c, nbc, tc, mode="pair", gqa=True, n_sub=1, **common)
            o_l = attention(qkv_l, nbl, tl, mode="pair", gqa=True, n_sub=2, cache=cache, **common)

        wo = w_out[l].astype(BF16)
        xc = proj_residual(o_c, wo, xc, mod, ctx_mod(ROW_TILE))
        xl = proj_residual(o_l, wo, xl, mod, lat_mod(ROW_TILE))

        if l % 2 == 0:
            wg, wu, wd = (w.astype(BF16) for w in ffn_p[l])
            xc = dense_ffn(xc, mod, ctx_mod(ROW_TILE), norm2[l], wg, wu, wd)
            xl = dense_ffn(xl, mod, lat_mod(ROW_TILE), norm2[l], wg, wu, wd)
        else:
            wr = ffn_p[l][0]
            wg, wu, wd = (w.astype(BF16) for w in ffn_p[l][1:])
            xc = moe_ffn(xc, mod, ctx_mod(MOE_BLOCK), norm2[l], wr, wg, wu, wd)
            xl = moe_ffn(xl, mod, lat_mod(MOE_BLOCK), norm2[l], wr, wg, wu, wd)

    nh = d // HEAD_DIM
    k0, v0 = states[0]
    k1, v1 = states[1]
    k2, v2 = states[2]
    k3, v3 = states[3]
    nkv = k2.shape[1] // HEAD_DIM
    return (xc.reshape(nbc, tc, d), xl.reshape(nbl, tl, d),
            k0.reshape(nbc, tc, nh, HEAD_DIM), v0.reshape(nbc, tc, nh, HEAD_DIM),
            k1.reshape(nbc, tc, nh // 2, 2, HEAD_DIM), v1.reshape(nbc, tc, nh // 2, 2 * HEAD_DIM),
            k2.reshape(nbc, tc, nkv, HEAD_DIM), v2.reshape(nbc, tc, nkv, HEAD_DIM),
            k3.reshape(nbc, tc, nkv, HEAD_DIM), v3.reshape(nbc, tc, nkv, HEAD_DIM))
```

```python
import functools
import math

import numpy as np
import jax
import jax.numpy as jnp
from jax import lax
from jax.experimental import pallas as pl
from jax.experimental.pallas import tpu as pltpu

F32 = jnp.float32
BF16 = jnp.bfloat16

HEAD_DIM = 64
GRID_W = 64
NA_ROWS = 8
NA_COLS = 16
WINDOW = 128
ROPE_BASE = 10000.0
ROPE_PAIRS_AXIS = HEAD_DIM // 4
N_EXPERTS = 8
EPS = 1e-6
NEG_INF = -1e30
QK_SCALE = HEAD_DIM ** -0.5
LOG2E = math.log2(math.e)

LANES = 128
V7X_VMEM_BYTES = 64 * 1024 * 1024
VMEM_LIMIT = V7X_VMEM_BYTES - 8 * 1024 * 1024

ROW_TILE = 512
ATTN_Q_TILE = 256
NA_KEY_ROWS = 12
CTX_LANE_BLOCKS = 8
MOE_BLOCK = 1024
MOE_SUB = 288


def _cparams(sem):
    return pltpu.CompilerParams(dimension_semantics=sem, vmem_limit_bytes=VMEM_LIMIT)


def _silu(x):
    return x * (1.0 / (1.0 + jnp.exp(-x)))


def _norm_mod(x, g, shift, scale):
    ms = jnp.mean(x * x, axis=-1, keepdims=True)
    y = x * lax.rsqrt(ms + EPS) * g
    return y * (1.0 + scale) + shift


def _dot(a, b):
    return jnp.dot(a, b, preferred_element_type=F32)


def _dot_nt(a, b):
    return lax.dot_general(a, b, (((1,), (1,)), ((), ())), preferred_element_type=F32)


def _ada_kernel(c_ref, w_ref, b_ref, o_ref):
    a = _silu(c_ref[...])
    a_hi = a.astype(BF16)
    a_lo = a - a_hi.astype(F32)
    row = lax.broadcasted_iota(jnp.int32, a.shape, 0)
    lhs = jnp.where(row < 8, a_hi.astype(F32), a_lo).astype(BF16)
    w = w_ref[...]
    w_hi = w.astype(BF16)
    w_lo = (w - w_hi.astype(F32)).astype(BF16)
    r = _dot(lhs, w_hi)
    r2 = _dot(a_hi, w_lo)
    o_ref[...] = r[:8] + r[8:] + r2[:8] + b_ref[...]


def ada_mod(cond16, w_ada, b_ada):
    d, n = w_ada.shape
    tn = 1024
    out = pl.pallas_call(
        _ada_kernel,
        grid=(n // tn,),
        in_specs=[pl.BlockSpec((16, d), lambda j: (0, 0)),
                  pl.BlockSpec((d, tn), lambda j: (0, j)),
                  pl.BlockSpec((1, tn), lambda j: (0, j))],
        out_specs=pl.BlockSpec((8, tn), lambda j: (0, j)),
        out_shape=jax.ShapeDtypeStruct((8, n), F32),
        compiler_params=_cparams(("arbitrary",)),
    )(cond16, w_ada, b_ada.reshape(1, n))
    return out.reshape(8, 6, d)


def _qkv_kernel(*refs, rope):
    if rope:
        x_ref, mod_ref, g_ref, w_ref, seg_ref, hn_ref, isn_ref, cos_ref, sin_ref, o_ref, h_scr = refs
    else:
        x_ref, mod_ref, g_ref, w_ref, seg_ref, hn_ref, isn_ref, o_ref, h_scr = refs

    @pl.when(pl.program_id(1) == 0)
    def _():
        h = _norm_mod(x_ref[...], g_ref[...], mod_ref[0, 0:1, :], mod_ref[0, 1:2, :])
        h_scr[...] = h.astype(BF16)

    acc = _dot(h_scr[...], w_ref[...])
    tn = acc.shape[1]
    sq = (acc * acc).astype(BF16)
    seg = seg_ref[...]
    ssum = jnp.concatenate(
        [_dot(sq[:, c * 2 * LANES:(c + 1) * 2 * LANES], seg) for c in range(tn // (2 * LANES))], axis=1)
    y = acc * lax.rsqrt(ssum * (1.0 / HEAD_DIM) + EPS) * hn_ref[...]
    if rope:
        lane = lax.broadcasted_iota(jnp.int32, (acc.shape[0], LANES), 1)
        even = (lane % 2) == 0
        cos, sin = cos_ref[...], sin_ref[...]
        parts = []
        for c in range(tn // LANES):
            yc = y[:, c * LANES:(c + 1) * LANES]
            nxt = pltpu.roll(yc, LANES - 1, 1)
            prv = pltpu.roll(yc, 1, 1)
            parts.append(yc * cos + jnp.where(even, nxt, prv) * sin)
        y = jnp.concatenate(parts, axis=1)
    o_ref[...] = jnp.where(isn_ref[...] > 0.5, y, acc).astype(o_ref.dtype)


def _seg_matrix():
    i = np.arange(2 * LANES)
    return jnp.asarray((i[:, None] // HEAD_DIM) == (i[None, :] // HEAD_DIM), BF16)


def qkv_proj(x, mod, mod_of_block, g, w, hn, isn, rope_tabs, out_dtype, tn):
    r, d = x.shape
    n = w.shape[1]
    tm = ROW_TILE
    rope = rope_tabs is not None
    in_specs = [pl.BlockSpec((tm, d), lambda i, j: (i, 0)),
                pl.BlockSpec((1, 6, d), lambda i, j: (mod_of_block(i), 0, 0)),
                pl.BlockSpec((1, d), lambda i, j: (0, 0)),
                pl.BlockSpec((d, tn), lambda i, j: (0, j)),
                pl.BlockSpec((2 * LANES, 2 * LANES), lambda i, j: (0, 0)),
                pl.BlockSpec((1, tn), lambda i, j: (0, j)),
                pl.BlockSpec((1, tn), lambda i, j: (0, j))]
    args = [x, mod, g.reshape(1, d), w, _seg_matrix(), hn, isn]
    if rope:
        cos_t, sin_t = rope_tabs
        nblk = cos_t.shape[0] // tm
        in_specs += [pl.BlockSpec((tm, LANES), lambda i, j: (i % nblk, 0)),
                     pl.BlockSpec((tm, LANES), lambda i, j: (i % nblk, 0))]
        args += [cos_t, sin_t]
    return pl.pallas_call(
        functools.partial(_qkv_kernel, rope=rope),
        grid=(r // tm, n // tn),
        in_specs=in_specs,
        out_specs=pl.BlockSpec((tm, tn), lambda i, j: (i, j)),
        out_shape=jax.ShapeDtypeStruct((r, n), out_dtype),
        scratch_shapes=[pltpu.VMEM((tm, d), BF16)],
        compiler_params=_cparams(("parallel", "arbitrary")),
    )(*args)


def _rope_tables(t):
    pos = np.arange(t)
    row = (pos // GRID_W).astype(np.float32)
    col = (pos % GRID_W).astype(np.float32)
    freqs = jnp.asarray(ROPE_BASE, F32) ** (-jnp.arange(ROPE_PAIRS_AXIS, dtype=F32) / ROPE_PAIRS_AXIS)
    ang = jnp.concatenate([jnp.asarray(row)[:, None] * freqs, jnp.asarray(col)[:, None] * freqs], axis=-1)
    cos = jnp.repeat(jnp.cos(ang), 2, axis=-1)
    sin = jnp.repeat(jnp.sin(ang), 2, axis=-1)
    sign = jnp.asarray(np.where(np.arange(HEAD_DIM) % 2 == 0, -1.0, 1.0), F32)
    reps = LANES // HEAD_DIM
    return jnp.tile(cos, (1, reps)), jnp.tile(sin * sign, (1, reps))


def _attn_kernel(*refs, n_lb, kmap, mode, n_cache, natten, has_sink, **static):
    it = iter(refs)
    sink_ref = next(it) if has_sink else None
    q_ref, k_ref, v_ref = next(it), next(it), next(it)
    ck_ref, cv_ref = (next(it), next(it)) if n_cache else (None, None)
    bp_ref = next(it) if natten else None
    lam_refs = [next(it), next(it)] if mode == "diff" else []
    o_ref, kf = next(it), next(it)
    v_scr = list(it)
    for jl in range(n_lb):
        q_cols = pl.ds(jl * LANES, LANES)
        k_cols = pl.ds(kmap(jl) * LANES, LANES)
        views = [sink_ref] if has_sink else []
        views += [q_ref.at[:, q_cols], k_ref.at[:, k_cols], v_ref.at[:, k_cols]]
        if n_cache:
            views += [ck_ref.at[:, :, k_cols], cv_ref.at[:, :, k_cols]]
        if natten:
            views.append(bp_ref.at[pl.ds(2 * jl, 2)])
        views += lam_refs + [o_ref.at[:, q_cols], kf.at[jl]] + [v.at[jl] for v in v_scr]
        _attn_lane_block(views, pl.program_id(1) * n_lb + jl, pl.program_id(2), mode=mode, n_cache=n_cache,
                         natten=natten, has_sink=has_sink, **static)


def _attn_lane_block(refs, j, qb, *, mode, gqa, t_loc, n_cache, win, band, natten, has_sink, lam_init, n_sub):
    it = iter(refs)
    sink_ref = next(it) if has_sink else None
    q_ref, k_ref, v_ref = next(it), next(it), next(it)
    ck_ref = cv_ref = bp_ref = lam_ref = subln_ref = None
    if n_cache:
        ck_ref, cv_ref = next(it), next(it)
    if natten:
        bp_ref = next(it)
    if mode == "diff":
        lam_ref, subln_ref = next(it), next(it)
    o_ref, kf = next(it), next(it)
    v_scr = [next(it), next(it)] if mode == "pair" else [next(it)]

    tq = ATTN_Q_TILE
    nk = t_loc + n_cache
    rows = t_loc // GRID_W
    qrows = tq // GRID_W

    def place(a):
        if not gqa:
            return a
        half = (j // 2) % 2
        lane_half = (lax.broadcasted_iota(jnp.int32, a.shape, 1) >= HEAD_DIM).astype(jnp.int32)
        return jnp.where(lane_half == half, a, pltpu.roll(a, HEAD_DIM, 1))

    def put_v(lo, hi, a):
        if mode == "pair":
            ln = lax.broadcasted_iota(jnp.int32, a.shape, 1)
            v_scr[0][lo:hi, :] = jnp.where(ln < HEAD_DIM, a, 1.0).astype(BF16)
            v_scr[1][lo:hi, :] = jnp.where(ln >= HEAD_DIM, a, 1.0).astype(BF16)
        else:
            v_scr[0][lo:hi, :] = a.astype(BF16)

    @pl.when(qb == 0)
    def _():
        kf[0:t_loc, :] = place(k_ref[...].astype(F32)).astype(BF16)
        put_v(0, t_loc, place(v_ref[...].astype(F32)))
        if n_cache:
            kf[t_loc:nk, :] = place(ck_ref[0]).astype(BF16)
            put_v(t_loc, nk, place(cv_ref[0]))

    lane = lax.broadcasted_iota(jnp.int32, (tq, LANES), 1)
    if mode == "diff":
        lp = lam_ref[...]
        lam = (jnp.exp(jnp.sum(lp[0:1] * lp[1:2], axis=1, keepdims=True))
               - jnp.exp(jnp.sum(lp[2:3] * lp[3:4], axis=1, keepdims=True)) + lam_init)

    for t in range(n_sub):
        g = qb * n_sub + t
        q2 = q_ref[t * tq:(t + 1) * tq, :].astype(F32) * (QK_SCALE * LOG2E)
        keep = None
        if natten:
            ws_row = jnp.clip(g * qrows - NA_ROWS // 2, 0, rows - NA_KEY_ROWS)
            sl = pl.ds(pl.multiple_of(ws_row * GRID_W, GRID_W), win)
            q_row = lax.broadcasted_iota(jnp.int32, (tq, win), 0) // GRID_W + g * qrows
            r0 = jnp.clip(q_row - NA_ROWS // 2, 0, rows - NA_ROWS)
            k_row = lax.broadcasted_iota(jnp.int32, (tq, win), 1) // GRID_W + ws_row
            keep = (k_row >= r0) & (k_row < r0 + NA_ROWS)
        elif band:
            ws = pl.multiple_of(jnp.clip(g * tq - WINDOW, 0, t_loc - win), WINDOW)
            sl = pl.ds(ws, win)
            qpos = g * tq + lax.broadcasted_iota(jnp.int32, (tq, win), 0)
            kpos = ws + lax.broadcasted_iota(jnp.int32, (tq, win), 1)
            keep = jnp.abs(qpos - kpos) <= WINDOW
        else:
            sl = slice(0, win)
        k_loc = kf[sl, :]

        chains = []
        for hh in range(2):
            in_half = (lane < HEAD_DIM) if hh == 0 else (lane >= HEAD_DIM)
            qh = jnp.where(in_half, q2, 0.0).astype(BF16)
            s = _dot_nt(qh, k_loc)
            if natten:
                blocks = []
                for a in range(qrows):
                    first = ws_row - (g * qrows + a) + NA_ROWS
                    blocks.append(jnp.concatenate(
                        [bp_ref[hh, pl.ds(jnp.clip(first + 2 * p, 0, 2 * NA_ROWS - 1), 1)][0]
                         for p in range(win // LANES)], axis=1))
                s = s + jnp.concatenate(blocks, axis=0)
            if keep is not None:
                s = jnp.where(keep, s, NEG_INF)
            m = jnp.max(s, axis=1, keepdims=True)
            if n_cache:
                s_c = _dot_nt(qh, kf[t_loc:nk, :])
                m = jnp.maximum(m, jnp.max(s_c, axis=1, keepdims=True))
            if has_sink:
                sink = sink_ref[2 * j + hh] * LOG2E
                m = jnp.maximum(m, sink)
            e = jnp.exp2(s - m)
            e_c = jnp.exp2(s_c - m) if n_cache else None
            if mode == "pair":
                o = _dot(e.astype(BF16), v_scr[hh][sl, :])
                if n_cache:
                    o = o + _dot(e_c.astype(BF16), v_scr[hh][t_loc:nk, :])
                den = pltpu.roll(o, HEAD_DIM, 1)
                if has_sink:
                    den = den + jnp.exp2(sink - m)
                chains.append(o * (1.0 / den))
            else:
                den = jnp.sum(e, axis=1, keepdims=True)
                if n_cache:
                    den = den + jnp.sum(e_c, axis=1, keepdims=True)
                chains.append((e, e_c, 1.0 / den))

        if mode == "pair":
            out = jnp.where(lane < HEAD_DIM, chains[0], chains[1])
        else:
            (e0, ec0, w0), (e1, ec1, w1) = chains
            w1 = w1 * lam
            out = _dot((e0 * w0 - e1 * w1).astype(BF16), v_scr[0][sl, :])
            if n_cache:
                out = out + _dot((ec0 * w0 - ec1 * w1).astype(BF16), v_scr[0][t_loc:nk, :])
            ms = jnp.mean(out * out, axis=-1, keepdims=True)
            out = out * lax.rsqrt(ms + EPS) * subln_ref[...] * (1.0 - lam_init)
        o_ref[t * tq:(t + 1) * tq, :] = out.astype(o_ref.dtype)


def attention(qkv, nb, t_loc, *, mode, gqa, k_col, v_col, n_sub, n_lb, cache=None, band=False, bias_pairs=None,
              sink=None, lam=None, subln=None, lam_init=0.0, d_model):
    tq = ATTN_Q_TILE * n_sub
    n_qlb = d_model // LANES
    natten = bias_pairs is not None
    n_cache = 0 if cache is None else cache[0].shape[1]
    if natten:
        win = NA_KEY_ROWS * GRID_W
    elif band:
        win = ATTN_Q_TILE + 2 * WINDOW
    else:
        win = t_loc
    if gqa:
        assert n_lb == 1 or n_lb % 4 == 0
        k_lb = max(n_lb // 4, 1)
        kgrp = (lambda jg: jg // 4) if n_lb == 1 else (lambda jg: jg)
        kmap = lambda jl: jl // 4
    else:
        k_lb = n_lb
        kgrp = lambda jg: jg
        kmap = lambda jl: jl
    qw, kw = n_lb * LANES, k_lb * LANES
    assert k_col % kw == 0 and v_col % kw == 0 and n_qlb % n_lb == 0
    kc, vc = k_col // kw, v_col // kw
    nqb = t_loc // tq
    nk = t_loc + n_cache

    in_specs = []
    args = []
    if sink is not None:
        in_specs.append(pl.BlockSpec(memory_space=pltpu.SMEM))
        args.append(sink)
    in_specs += [pl.BlockSpec((tq, qw), lambda b, j, i: (b * nqb + i, j)),
                 pl.BlockSpec((t_loc, kw), lambda b, j, i: (b, kc + kgrp(j))),
                 pl.BlockSpec((t_loc, kw), lambda b, j, i: (b, vc + kgrp(j)))]
    args += [qkv, qkv, qkv]
    if cache is not None:
        in_specs += [pl.BlockSpec((1, n_cache, kw), lambda b, j, i: (b, 0, kgrp(j))),
                     pl.BlockSpec((1, n_cache, kw), lambda b, j, i: (b, 0, kgrp(j)))]
        args += list(cache)
    if natten:
        in_specs.append(pl.BlockSpec((2 * n_lb, 2 * NA_ROWS, GRID_W, LANES), lambda b, j, i: (j, 0, 0, 0)))
        args.append(bias_pairs)
    if mode == "diff":
        in_specs += [pl.BlockSpec((8, LANES), lambda b, j, i: (0, 0)),
                     pl.BlockSpec((1, LANES), lambda b, j, i: (0, 0))]
        args += [lam, subln]
    kern = functools.partial(
        _attn_kernel, n_lb=n_lb, kmap=kmap, mode=mode, gqa=gqa, t_loc=t_loc, n_cache=n_cache, win=win, band=band,
        natten=natten, has_sink=sink is not None, lam_init=lam_init, n_sub=n_sub)
    n_v = 2 if mode == "pair" else 1
    return pl.pallas_call(
        kern,
        grid=(nb, n_qlb // n_lb, nqb),
        in_specs=in_specs,
        out_specs=pl.BlockSpec((tq, qw), lambda b, j, i: (b * nqb + i, j)),
        out_shape=jax.ShapeDtypeStruct((nb * t_loc, d_model), BF16),
        scratch_shapes=[pltpu.VMEM((n_lb, nk, LANES), BF16) for _ in range(1 + n_v)],
        compiler_params=_cparams(("parallel", "arbitrary", "arbitrary")),
    )(*args)


def _natten_pair_table(rpb):
    h = rpb.shape[0]
    pad = GRID_W - NA_COLS
    ext = jnp.concatenate([jnp.repeat(rpb[..., :1], pad, axis=-1), rpb,
                           jnp.repeat(rpb[..., -1:], pad, axis=-1)], axis=-1).astype(F32)
    b = jnp.stack([ext[..., GRID_W - 1 - qc:2 * GRID_W - 1 - qc] for qc in range(GRID_W)], axis=2)
    col = np.arange(GRID_W)
    cstart = np.clip(col - NA_COLS // 2, 0, GRID_W - NA_COLS)
    cmask = (col[None, :] >= cstart[:, None]) & (col[None, :] < cstart[:, None] + NA_COLS)
    b = jnp.where(jnp.asarray(cmask), b * LOG2E, NEG_INF)
    neg = jnp.full((h, 1, GRID_W, GRID_W), NEG_INF, F32)
    bpad = jnp.concatenate([neg, b, neg], axis=1)
    return jnp.concatenate([bpad[:, :-1], bpad[:, 1:]], axis=-1)


def _mixer_residual(a_ref, wo_ref, x_ref, mod_ref):
    return x_ref[...] + mod_ref[0, 2:3, :] * _dot(a_ref[...], wo_ref[...])


def _ffn_kernel(a_ref, wo_ref, x_ref, mod_ref, g_ref, wg_ref, wu_ref, wd_ref, o_ref, h_scr, acc_scr, x1_scr):
    f = pl.program_id(1)

    @pl.when(f == 0)
    def _():
        x1 = _mixer_residual(a_ref, wo_ref, x_ref, mod_ref)
        x1_scr[...] = x1
        h = _norm_mod(x1, g_ref[...], mod_ref[0, 3:4, :], mod_ref[0, 4:5, :])
        h_scr[...] = h.astype(BF16)

    h = h_scr[...]
    t = (_silu(_dot(h, wg_ref[...])) * _dot(h, wu_ref[...])).astype(BF16)
    y = _dot(t, wd_ref[...])

    @pl.when(f == 0)
    def _():
        acc_scr[...] = y

    @pl.when(f > 0)
    def _():
        acc_scr[...] += y

    @pl.when(f == pl.num_programs(1) - 1)
    def _():
        o_ref[...] = x1_scr[...] + mod_ref[0, 5:6, :] * acc_scr[...]


def dense_ffn(a, wo, x, mod, mod_of_block, g, wg, wu, wd):
    r, d = x.shape
    ff = wg.shape[1]
    tm = ROW_TILE
    nf = 2
    tf = ff // nf
    return pl.pallas_call(
        _ffn_kernel,
        grid=(r // tm, nf),
        in_specs=[pl.BlockSpec((tm, d), lambda i, f: (i, 0)),
                  pl.BlockSpec((d, d), lambda i, f: (0, 0)),
                  pl.BlockSpec((tm, d), lambda i, f: (i, 0)),
                  pl.BlockSpec((1, 6, d), lambda i, f: (mod_of_block(i), 0, 0)),
                  pl.BlockSpec((1, d), lambda i, f: (0, 0)),
                  pl.BlockSpec((d, tf), lambda i, f: (0, f)),
                  pl.BlockSpec((d, tf), lambda i, f: (0, f)),
                  pl.BlockSpec((tf, d), lambda i, f: (f, 0))],
        out_specs=pl.BlockSpec((tm, d), lambda i, f: (i, 0)),
        out_shape=jax.ShapeDtypeStruct((r, d), F32),
        scratch_shapes=[pltpu.VMEM((tm, d), BF16), pltpu.VMEM((tm, d), F32), pltpu.VMEM((tm, d), F32)],
        compiler_params=_cparams(("parallel", "arbitrary")),
    )(a, wo, x, mod, g.reshape(1, d), wg, wu, wd)


def _router_kernel(a_ref, wo_ref, x_ref, mod_ref, g_ref, wr_ref,
                   x1_ref, h_ref, rank_row_ref, gate_row_ref, rank_col_ref, cnt_ref):
    x1 = _mixer_residual(a_ref, wo_ref, x_ref, mod_ref)
    x1_ref[...] = x1
    h = _norm_mod(x1, g_ref[...], mod_ref[0, 3:4, :], mod_ref[0, 4:5, :])
    h_hi = h.astype(BF16)
    h_ref[...] = h_hi
    h_lo = (h - h_hi.astype(F32)).astype(BF16)
    wr = wr_ref[...]
    w_hi = wr.astype(BF16)
    w_lo = (wr - w_hi.astype(F32)).astype(BF16)
    logits = _dot(h_hi, w_hi) + (_dot(h_lo, w_hi) + _dot(h_hi, w_lo))
    tb = logits.shape[0]
    lane = lax.broadcasted_iota(jnp.int32, logits.shape, 1)
    lane_f = lane.astype(F32)
    logits = jnp.where(lane < N_EXPERTS, logits, -jnp.inf)
    m1 = jnp.max(logits, axis=1, keepdims=True)
    i1 = jnp.min(jnp.where(logits == m1, lane_f, float(LANES)), axis=1, keepdims=True)
    rest = jnp.where(lane_f == i1, -jnp.inf, logits)
    m2 = jnp.max(rest, axis=1, keepdims=True)
    i2 = jnp.min(jnp.where(rest == m2, lane_f, float(LANES)), axis=1, keepdims=True)
    e2 = jnp.exp(m2 - m1)
    inv = 1.0 / (1.0 + e2)
    is1 = lane_f == i1
    is2 = lane_f == i2
    gates = jnp.where(is1, inv, 0.0) + jnp.where(is2, e2 * inv, 0.0)
    sel = jnp.where(is1 | is2, 1.0, 0.0)
    sel_b = sel.astype(BF16)
    chunk = 256
    parts = []
    for c in range(tb // chunk):
        ri = lax.broadcasted_iota(jnp.int32, (chunk, tb), 0) + c * chunk
        ci = lax.broadcasted_iota(jnp.int32, (chunk, tb), 1)
        parts.append(_dot(jnp.where(ci < ri, 1.0, 0.0).astype(BF16), sel_b))
    rank = jnp.concatenate(parts, axis=0)
    rank_sel = jnp.where(sel > 0.5, rank, -1.0)
    rank_col_ref[...] = rank_sel
    rank_row_ref[0] = rank_sel.T[:N_EXPERTS]
    gate_row_ref[0] = gates.T[:N_EXPERTS]
    cnt_ref[0] = jnp.broadcast_to(jnp.sum(sel, axis=0, keepdims=True), (8, LANES)).astype(jnp.int32)


def _expert_kernel(cnt_ref, h_ref, rank_row_ref, gate_row_ref, rank_col_ref, wg_ref, wu_ref, wd_ref,
                   x_ref, mod_ref, o_ref, xg, yacc, gs):
    b, e, f = pl.program_id(0), pl.program_id(1), pl.program_id(2)
    nf = pl.num_programs(2)
    tb = h_ref.shape[0]
    m = MOE_SUB
    n_rows = cnt_ref[b * N_EXPERTS + e]
    nsub = (n_rows + m - 1) // m

    @pl.when((e == 0) & (f == 0))
    def _():
        o_ref[...] = jnp.zeros_like(o_ref)

    @pl.when(f == 0)
    def _():
        rank_e = rank_row_ref[0, pl.ds(e, 1), :]
        gate_e = gate_row_ref[0, pl.ds(e, 1), :]

        def gather(i, carry):
            base = pl.multiple_of(i * m, m)
            want = (lax.broadcasted_iota(jnp.int32, (m, tb), 0) + base).astype(F32)
            p = jnp.where(rank_e == want, 1.0, 0.0)
            xg[pl.ds(base, m), :] = _dot(p.astype(BF16), h_ref[...]).astype(BF16)
            gs[pl.ds(base, m), :] = jnp.broadcast_to(jnp.sum(p * gate_e, axis=1, keepdims=True), (m, LANES))
            return carry

        lax.fori_loop(0, nsub, gather, 0)

    def ffn(i, carry):
        base = pl.multiple_of(i * m, m)
        xr = xg[pl.ds(base, m), :]
        t = (_silu(_dot(xr, wg_ref[0])) * _dot(xr, wu_ref[0])).astype(BF16)
        y = _dot(t, wd_ref[0])

        @pl.when(f == 0)
        def _():
            yacc[pl.ds(base, m), :] = y

        @pl.when(f > 0)
        def _():
            yacc[pl.ds(base, m), :] += y

        return carry

    lax.fori_loop(0, nsub, ffn, 0)

    @pl.when(f == nf - 1)
    def _():
        lane = lax.broadcasted_iota(jnp.int32, (tb, LANES), 1)
        rank_c = jnp.sum(jnp.where(lane == e, rank_col_ref[...], 0.0), axis=1, keepdims=True)

        def scatter(i, carry):
            base = pl.multiple_of(i * m, m)
            want = (lax.broadcasted_iota(jnp.int32, (tb, m), 1) + base).astype(F32)
            pt = jnp.where(rank_c == want, 1.0, 0.0).astype(BF16)
            yg = (yacc[pl.ds(base, m), :] * gs[pl.ds(base, m), 0:1]).astype(BF16)
            o_ref[...] += _dot(pt, yg)
            return carry

        lax.fori_loop(0, nsub, scatter, 0)

    @pl.when((e == N_EXPERTS - 1) & (f == nf - 1))
    def _():
        o_ref[...] = x_ref[...] + mod_ref[0, 5:6, :] * o_ref[...]


def moe_ffn(a, wo, x, mod, mod_of_block, g, w_router, wg, wu, wd):
    r, d = x.shape
    ff = wg.shape[2]
    tb = MOE_BLOCK
    nb = r // tb
    wr = jnp.pad(w_router, ((0, 0), (0, LANES - N_EXPERTS)))
    x, h, rank_row, gate_row, rank_col, cnt = pl.pallas_call(
        _router_kernel,
        grid=(nb,),
        in_specs=[pl.BlockSpec((tb, d), lambda i: (i, 0)),
                  pl.BlockSpec((d, d), lambda i: (0, 0)),
                  pl.BlockSpec((tb, d), lambda i: (i, 0)),
                  pl.BlockSpec((1, 6, d), lambda i: (mod_of_block(i), 0, 0)),
                  pl.BlockSpec((1, d), lambda i: (0, 0)),
                  pl.BlockSpec((d, LANES), lambda i: (0, 0))],
        out_specs=[pl.BlockSpec((tb, d), lambda i: (i, 0)),
                   pl.BlockSpec((tb, d), lambda i: (i, 0)),
                   pl.BlockSpec((1, N_EXPERTS, tb), lambda i: (i, 0, 0)),
                   pl.BlockSpec((1, N_EXPERTS, tb), lambda i: (i, 0, 0)),
                   pl.BlockSpec((tb, LANES), lambda i: (i, 0)),
                   pl.BlockSpec((1, 8, LANES), lambda i: (i, 0, 0))],
        out_shape=[jax.ShapeDtypeStruct((r, d), F32),
                   jax.ShapeDtypeStruct((r, d), BF16),
                   jax.ShapeDtypeStruct((nb, N_EXPERTS, tb), F32),
                   jax.ShapeDtypeStruct((nb, N_EXPERTS, tb), F32),
                   jax.ShapeDtypeStruct((r, LANES), F32),
                   jax.ShapeDtypeStruct((nb, 8, LANES), jnp.int32)],
        compiler_params=_cparams(("parallel",)),
    )(a, wo, x, mod, g.reshape(1, d), wr)
    counts = cnt[:, 0, :N_EXPERTS].reshape(nb * N_EXPERTS)

    nf = 2
    tf = ff // nf
    sub_rows = -(-tb // MOE_SUB) * MOE_SUB
    grid_spec = pltpu.PrefetchScalarGridSpec(
        num_scalar_prefetch=1,
        grid=(nb, N_EXPERTS, nf),
        in_specs=[pl.BlockSpec((tb, d), lambda i, e, f, c: (i, 0)),
                  pl.BlockSpec((1, N_EXPERTS, tb), lambda i, e, f, c: (i, 0, 0)),
                  pl.BlockSpec((1, N_EXPERTS, tb), lambda i, e, f, c: (i, 0, 0)),
                  pl.BlockSpec((tb, LANES), lambda i, e, f, c: (i, 0)),
                  pl.BlockSpec((1, d, tf), lambda i, e, f, c: (e, 0, f)),
                  pl.BlockSpec((1, d, tf), lambda i, e, f, c: (e, 0, f)),
                  pl.BlockSpec((1, tf, d), lambda i, e, f, c: (e, f, 0)),
                  pl.BlockSpec((tb, d), lambda i, e, f, c: (i, 0)),
                  pl.BlockSpec((1, 6, d), lambda i, e, f, c: (mod_of_block(i), 0, 0))],
        out_specs=pl.BlockSpec((tb, d), lambda i, e, f, c: (i, 0)),
        scratch_shapes=[pltpu.VMEM((sub_rows, d), BF16), pltpu.VMEM((sub_rows, d), F32),
                        pltpu.VMEM((sub_rows, LANES), F32)],
    )
    return pl.pallas_call(
        _expert_kernel,
        grid_spec=grid_spec,
        out_shape=jax.ShapeDtypeStruct((r, d), F32),
        compiler_params=_cparams(("parallel", "arbitrary", "arbitrary")),
    )(counts, h, rank_row, gate_row, rank_col, wg, wu, wd, x, mod)


def _diff_lambda_init(layer):
    return 0.8 - 0.6 * math.exp(-0.3 * layer)


def _head_gain_rows(q_norm, k_norm, n_q, n_k, n_v):
    hn = jnp.concatenate([jnp.tile(q_norm.astype(F32), n_q // HEAD_DIM),
                          jnp.tile(k_norm.astype(F32), n_k // HEAD_DIM),
                          jnp.ones((n_v,), F32)]).reshape(1, -1)
    isn = jnp.concatenate([jnp.ones((n_q + n_k,), F32), jnp.zeros((n_v,), F32)]).reshape(1, -1)
    return hn, isn


def kernel(x_prompt, x_sample, cache_k_0, cache_v_0, cache_k_1, cache_v_1, cache_k_2, cache_v_2, cache_k_3, cache_v_3, c, c_ctx, norm1_0, w_ada_0, b_ada_0, w_qkv_0, q_norm_0, k_norm_0, rpb_0, w_o_0, norm2_0, w_gate_0, w_up_0, w_down_0, norm1_1, w_ada_1, b_ada_1, w_qkv_1, q_norm_1, k_norm_1, lam_q1_1, lam_k1_1, lam_q2_1, lam_k2_1, subln_1, w_o_1, norm2_1, w_router_1, w_egate_1, w_eup_1, w_edown_1, norm1_2, w_ada_2, b_ada_2, w_qkv_2, q_norm_2, k_norm_2, sink_2, w_o_2, norm2_2, w_gate_2, w_up_2, w_down_2, norm1_3, w_ada_3, b_ada_3, w_qkv_3, q_norm_3, k_norm_3, w_o_3, norm2_3, w_router_3, w_egate_3, w_eup_3, w_edown_3):
    nbc, tc, d = x_prompt.shape
    nbl, tl, _ = x_sample.shape
    n_past = cache_k_0.shape[1]
    assert d % (2 * LANES) == 0 and tc % ATTN_Q_TILE == 0 and tl % MOE_BLOCK == 0
    assert (nbc * tc) % MOE_BLOCK == 0 and tl // GRID_W >= NA_KEY_ROWS and nbl <= 7

    norm1 = (norm1_0, norm1_1, norm1_2, norm1_3)
    norm2 = (norm2_0, norm2_1, norm2_2, norm2_3)
    ada_p = ((w_ada_0, b_ada_0), (w_ada_1, b_ada_1), (w_ada_2, b_ada_2), (w_ada_3, b_ada_3))
    w_qkv = (w_qkv_0, w_qkv_1, w_qkv_2, w_qkv_3)
    qk_norm = ((q_norm_0, k_norm_0), (q_norm_1, k_norm_1), (q_norm_2, k_norm_2), (q_norm_3, k_norm_3))
    w_out = (w_o_0, w_o_1, w_o_2, w_o_3)
    ffn_p = ((w_gate_0, w_up_0, w_down_0), (w_router_1, w_egate_1, w_eup_1, w_edown_1),
             (w_gate_2, w_up_2, w_down_2), (w_router_3, w_egate_3, w_eup_3, w_edown_3))
    caches = ((cache_k_0, cache_v_0), (cache_k_1, cache_v_1), (cache_k_2, cache_v_2), (cache_k_3, cache_v_3))

    xc = x_prompt.reshape(nbc * tc, d)
    xl = x_sample.reshape(nbl * tl, d)
    cond = jnp.concatenate([c_ctx[None, :], c, jnp.zeros((7 - nbl, d), F32)], axis=0)
    cond16 = jnp.concatenate([cond, cond], axis=0)

    def ctx_mod(blk_rows):
        return lambda i: 0

    def lat_mod(blk_rows):
        per = tl // blk_rows
        return lambda i: 1 + i // per

    lam_rows = jnp.zeros((8, LANES), F32)
    for r_, v_ in enumerate((lam_q1_1, lam_k1_1, lam_q2_1, lam_k2_1)):
        lam_rows = lam_rows.at[r_, :HEAD_DIM].set(v_.astype(F32))
    bias_pairs = _natten_pair_table(rpb_0)

    states = []
    for l in range(4):
        mixer = l % 4
        mod = ada_mod(cond16, *ada_p[l])
        wq = w_qkv[l].astype(BF16)
        n = wq.shape[1]
        n_q = d
        n_k = d if mixer < 2 else (n - d) // 2
        n_v = n - n_q - n_k
        tn = 1024 if n % 1024 == 0 else 512
        hn, isn = _head_gain_rows(*qk_norm[l], n_q, n_k, n_v)
        rope = None if mixer == 0 else _rope_tables(tl)

        qkv_c = qkv_proj(xc, mod, ctx_mod(ROW_TILE), norm1[l], wq, hn, isn, None, F32, tn)
        qkv_l = qkv_proj(xl, mod, lat_mod(ROW_TILE), norm1[l], wq, hn, isn, rope, BF16, tn)
        states.append((qkv_c[:, n_q:n_q + n_k], qkv_c[:, n_q + n_k:]))

        ck, cv = caches[l]
        cache = (ck.reshape(nbl, n_past, n_k), cv.reshape(nbl, n_past, n_v))
        common = dict(k_col=n_q, v_col=n_q + n_k, d_model=d)
        ctx = dict(n_sub=1, n_lb=CTX_LANE_BLOCKS, **common)
        lat = dict(n_sub=2, n_lb=1, cache=cache, **common)
        if mixer == 0:
            o_c = attention(qkv_c, nbc, tc, mode="pair", gqa=False, **ctx)
            o_l = attention(qkv_l, nbl, tl, mode="pair", gqa=False, bias_pairs=bias_pairs, **lat)
        elif mixer == 1:
            diff = dict(mode="diff", gqa=False, lam=lam_rows, subln=subln_1.astype(F32).reshape(1, LANES),
                        lam_init=_diff_lambda_init(l))
            o_c = attention(qkv_c, nbc, tc, **diff, **ctx)
            o_l = attention(qkv_l, nbl, tl, **diff, **lat)
        elif mixer == 2:
            sk = sink_2.astype(F32)
            o_c = attention(qkv_c, nbc, tc, mode="pair", gqa=True, sink=sk, **ctx)
            o_l = attention(qkv_l, nbl, tl, mode="pair", gqa=True, band=True, sink=sk, **lat)
        else:
            o_c = attention(qkv_c, nbc, tc, mode="pair", gqa=True, **ctx)
            o_l = attention(qkv_l, nbl, tl, mode="pair", gqa=True, **lat)

        wo = w_out[l].astype(BF16)
        if l % 2 == 0:
            wg, wu, wd = (w.astype(BF16) for w in ffn_p[l])
            xc = dense_ffn(o_c, wo, xc, mod, ctx_mod(ROW_TILE), norm2[l], wg, wu, wd)
            xl = dense_ffn(o_l, wo, xl, mod, lat_mod(ROW_TILE), norm2[l], wg, wu, wd)
        else:
            wr = ffn_p[l][0]
            wg, wu, wd = (w.astype(BF16) for w in ffn_p[l][1:])
            xc = moe_ffn(o_c, wo, xc, mod, ctx_mod(MOE_BLOCK), norm2[l], wr, wg, wu, wd)
            xl = moe_ffn(o_l, wo, xl, mod, lat_mod(MOE_BLOCK), norm2[l], wr, wg, wu, wd)

    nh = d // HEAD_DIM
    k0, v0 = states[0]
    k1, v1 = states[1]
    k2, v2 = states[2]
    k3, v3 = states[3]
    nkv = k2.shape[1] // HEAD_DIM
    return (xc.reshape(nbc, tc, d), xl.reshape(nbl, tl, d),
            k0.reshape(nbc, tc, nh, HEAD_DIM), v0.reshape(nbc, tc, nh, HEAD_DIM),
            k1.reshape(nbc, tc, nh // 2, 2, HEAD_DIM), v1.reshape(nbc, tc, nh // 2, 2 * HEAD_DIM),
            k2.reshape(nbc, tc, nkv, HEAD_DIM), v2.reshape(nbc, tc, nkv, HEAD_DIM),
            k3.reshape(nbc, tc, nkv, HEAD_DIM), v3.reshape(nbc, tc, nkv, HEAD_DIM))
```

```python
import functools
import math

import numpy as np
import jax
import jax.numpy as jnp
from jax import lax
from jax.experimental import pallas as pl
from jax.experimental.pallas import tpu as pltpu

F32 = jnp.float32
BF16 = jnp.bfloat16

HEAD_DIM = 64
GRID_W = 64
NA_ROWS = 8
NA_COLS = 16
WINDOW = 128
ROPE_BASE = 10000.0
ROPE_PAIRS_AXIS = HEAD_DIM // 4
N_EXPERTS = 8
EPS = 1e-6
NEG_INF = -1e30
QK_SCALE = HEAD_DIM ** -0.5
LOG2E = math.log2(math.e)

LANES = 128
V7X_VMEM_BYTES = 64 * 1024 * 1024
VMEM_LIMIT = V7X_VMEM_BYTES - 8 * 1024 * 1024

ROW_TILE = 512
ATTN_Q_TILE = 256
NA_KEY_ROWS = 12
CTX_LANE_BLOCKS = 8
MOE_BLOCK = 1024
MOE_SUB = 256


def _cparams(sem):
    return pltpu.CompilerParams(dimension_semantics=sem, vmem_limit_bytes=VMEM_LIMIT)


def _silu(x):
    return x * (1.0 / (1.0 + jnp.exp(-x)))


def _norm_mod(x, g, shift, scale):
    ms = jnp.mean(x * x, axis=-1, keepdims=True)
    y = x * lax.rsqrt(ms + EPS) * g
    return y * (1.0 + scale) + shift


def _dot(a, b):
    return jnp.dot(a, b, preferred_element_type=F32)


def _dot_nt(a, b):
    return lax.dot_general(a, b, (((1,), (1,)), ((), ())), preferred_element_type=F32)


def _ada_kernel(c_ref, w_ref, b_ref, o_ref):
    a = _silu(c_ref[...])
    a_hi = a.astype(BF16)
    a_lo = a - a_hi.astype(F32)
    row = lax.broadcasted_iota(jnp.int32, a.shape, 0)
    lhs = jnp.where(row < 8, a_hi.astype(F32), a_lo).astype(BF16)
    w = w_ref[...]
    w_hi = w.astype(BF16)
    w_lo = (w - w_hi.astype(F32)).astype(BF16)
    r = _dot(lhs, w_hi)
    r2 = _dot(a_hi, w_lo)
    o_ref[...] = r[:8] + r[8:] + r2[:8] + b_ref[...]


def ada_mod(cond16, w_ada, b_ada):
    d, n = w_ada.shape
    tn = 1024
    out = pl.pallas_call(
        _ada_kernel,
        grid=(n // tn,),
        in_specs=[pl.BlockSpec((16, d), lambda j: (0, 0)),
                  pl.BlockSpec((d, tn), lambda j: (0, j)),
                  pl.BlockSpec((1, tn), lambda j: (0, j))],
        out_specs=pl.BlockSpec((8, tn), lambda j: (0, j)),
        out_shape=jax.ShapeDtypeStruct((8, n), F32),
        compiler_params=_cparams(("arbitrary",)),
    )(cond16, w_ada, b_ada.reshape(1, n))
    return out.reshape(8, 6, d)


def _qkv_kernel(*refs, rope):
    if rope:
        x_ref, mod_ref, g_ref, w_ref, seg_ref, hn_ref, isn_ref, cos_ref, sin_ref, o_ref, h_scr = refs
    else:
        x_ref, mod_ref, g_ref, w_ref, seg_ref, hn_ref, isn_ref, o_ref, h_scr = refs

    @pl.when(pl.program_id(1) == 0)
    def _():
        h = _norm_mod(x_ref[...], g_ref[...], mod_ref[0, 0:1, :], mod_ref[0, 1:2, :])
        h_scr[...] = h.astype(BF16)

    acc = _dot(h_scr[...], w_ref[...])
    tn = acc.shape[1]
    sq = (acc * acc).astype(BF16)
    seg = seg_ref[...]
    ssum = jnp.concatenate(
        [_dot(sq[:, c * 2 * LANES:(c + 1) * 2 * LANES], seg) for c in range(tn // (2 * LANES))], axis=1)
    y = acc * lax.rsqrt(ssum * (1.0 / HEAD_DIM) + EPS) * hn_ref[...]
    if rope:
        lane = lax.broadcasted_iota(jnp.int32, (acc.shape[0], LANES), 1)
        even = (lane % 2) == 0
        cos, sin = cos_ref[...], sin_ref[...]
        parts = []
        for c in range(tn // LANES):
            yc = y[:, c * LANES:(c + 1) * LANES]
            nxt = pltpu.roll(yc, LANES - 1, 1)
            prv = pltpu.roll(yc, 1, 1)
            parts.append(yc * cos + jnp.where(even, nxt, prv) * sin)
        y = jnp.concatenate(parts, axis=1)
    o_ref[...] = jnp.where(isn_ref[...] > 0.5, y, acc).astype(o_ref.dtype)


def _seg_matrix():
    i = np.arange(2 * LANES)
    return jnp.asarray((i[:, None] // HEAD_DIM) == (i[None, :] // HEAD_DIM), BF16)


def qkv_proj(x, mod, mod_of_block, g, w, hn, isn, rope_tabs, out_dtype, tn):
    r, d = x.shape
    n = w.shape[1]
    tm = ROW_TILE
    rope = rope_tabs is not None
    in_specs = [pl.BlockSpec((tm, d), lambda i, j: (i, 0)),
                pl.BlockSpec((1, 6, d), lambda i, j: (mod_of_block(i), 0, 0)),
                pl.BlockSpec((1, d), lambda i, j: (0, 0)),
                pl.BlockSpec((d, tn), lambda i, j: (0, j)),
                pl.BlockSpec((2 * LANES, 2 * LANES), lambda i, j: (0, 0)),
                pl.BlockSpec((1, tn), lambda i, j: (0, j)),
                pl.BlockSpec((1, tn), lambda i, j: (0, j))]
    args = [x, mod, g.reshape(1, d), w, _seg_matrix(), hn, isn]
    if rope:
        cos_t, sin_t = rope_tabs
        nblk = cos_t.shape[0] // tm
        in_specs += [pl.BlockSpec((tm, LANES), lambda i, j: (i % nblk, 0)),
                     pl.BlockSpec((tm, LANES), lambda i, j: (i % nblk, 0))]
        args += [cos_t, sin_t]
    return pl.pallas_call(
        functools.partial(_qkv_kernel, rope=rope),
        grid=(r // tm, n // tn),
        in_specs=in_specs,
        out_specs=pl.BlockSpec((tm, tn), lambda i, j: (i, j)),
        out_shape=jax.ShapeDtypeStruct((r, n), out_dtype),
        scratch_shapes=[pltpu.VMEM((tm, d), BF16)],
        compiler_params=_cparams(("parallel", "arbitrary")),
    )(*args)


def _rope_tables(t):
    pos = np.arange(t)
    row = (pos // GRID_W).astype(np.float32)
    col = (pos % GRID_W).astype(np.float32)
    freqs = jnp.asarray(ROPE_BASE, F32) ** (-jnp.arange(ROPE_PAIRS_AXIS, dtype=F32) / ROPE_PAIRS_AXIS)
    ang = jnp.concatenate([jnp.asarray(row)[:, None] * freqs, jnp.asarray(col)[:, None] * freqs], axis=-1)
    cos = jnp.repeat(jnp.cos(ang), 2, axis=-1)
    sin = jnp.repeat(jnp.sin(ang), 2, axis=-1)
    sign = jnp.asarray(np.where(np.arange(HEAD_DIM) % 2 == 0, -1.0, 1.0), F32)
    reps = LANES // HEAD_DIM
    return jnp.tile(cos, (1, reps)), jnp.tile(sin * sign, (1, reps))


def _in_waves(gens):
    while gens:
        gens = [g for g in gens if next(g, StopIteration) is not StopIteration]
        yield


def _attn_kernel(*refs, n_lb, kmap, mode, n_cache, natten, has_sink, **static):
    it = iter(refs)
    sink_ref = next(it) if has_sink else None
    q_ref, k_ref, v_ref = next(it), next(it), next(it)
    ck_ref, cv_ref = (next(it), next(it)) if n_cache else (None, None)
    bp_ref = next(it) if natten else None
    lam_refs = [next(it), next(it)] if mode == "diff" else []
    o_ref, kf = next(it), next(it)
    v_scr = list(it)
    gens = []
    for jl in range(n_lb):
        q_cols = pl.ds(jl * LANES, LANES)
        k_cols = pl.ds(kmap(jl) * LANES, LANES)
        views = [sink_ref] if has_sink else []
        views += [q_ref.at[:, q_cols], k_ref.at[:, k_cols], v_ref.at[:, k_cols]]
        if n_cache:
            views += [ck_ref.at[:, :, k_cols], cv_ref.at[:, :, k_cols]]
        if natten:
            views.append(bp_ref.at[pl.ds(2 * jl, 2)])
        views += lam_refs + [o_ref.at[:, q_cols], kf.at[jl]] + [v.at[jl] for v in v_scr]
        gens.append(_attn_lane_block(views, pl.program_id(1) * n_lb + jl, pl.program_id(2), mode=mode,
                                     n_cache=n_cache, natten=natten, has_sink=has_sink, **static))
    for _ in _in_waves(gens):
        pass


def _attn_lane_block(refs, j, qb, *, mode, gqa, t_loc, n_cache, win, band, natten, has_sink, lam_init, n_sub):
    it = iter(refs)
    sink_ref = next(it) if has_sink else None
    q_ref, k_ref, v_ref = next(it), next(it), next(it)
    ck_ref = cv_ref = bp_ref = lam_ref = subln_ref = None
    if n_cache:
        ck_ref, cv_ref = next(it), next(it)
    if natten:
        bp_ref = next(it)
    if mode == "diff":
        lam_ref, subln_ref = next(it), next(it)
    o_ref, kf = next(it), next(it)
    v_scr = [next(it), next(it)] if mode == "pair" else [next(it)]

    tq = ATTN_Q_TILE
    nk = t_loc + n_cache
    rows = t_loc // GRID_W
    qrows = tq // GRID_W

    def place(a):
        if not gqa:
            return a
        half = (j // 2) % 2
        lane_half = (lax.broadcasted_iota(jnp.int32, a.shape, 1) >= HEAD_DIM).astype(jnp.int32)
        return jnp.where(lane_half == half, a, pltpu.roll(a, HEAD_DIM, 1))

    def put_v(lo, hi, a):
        if mode == "pair":
            ln = lax.broadcasted_iota(jnp.int32, a.shape, 1)
            v_scr[0][lo:hi, :] = jnp.where(ln < HEAD_DIM, a, 1.0).astype(BF16)
            v_scr[1][lo:hi, :] = jnp.where(ln >= HEAD_DIM, a, 1.0).astype(BF16)
        else:
            v_scr[0][lo:hi, :] = a.astype(BF16)

    @pl.when(qb == 0)
    def _():
        kf[0:t_loc, :] = place(k_ref[...].astype(F32)).astype(BF16)
        put_v(0, t_loc, place(v_ref[...].astype(F32)))
        if n_cache:
            kf[t_loc:nk, :] = place(ck_ref[0].astype(F32)).astype(BF16)
            put_v(t_loc, nk, place(cv_ref[0].astype(F32)))

    lane = lax.broadcasted_iota(jnp.int32, (tq, LANES), 1)
    if mode == "diff":
        lp = lam_ref[...]
        lam = (jnp.exp(jnp.sum(lp[0:1] * lp[1:2], axis=1, keepdims=True))
               - jnp.exp(jnp.sum(lp[2:3] * lp[3:4], axis=1, keepdims=True)) + lam_init)

    def chain(g, hh, q2, sl, k_loc, keep, ws_row, done):
        in_half = (lane < HEAD_DIM) if hh == 0 else (lane >= HEAD_DIM)
        qh = jnp.where(in_half, q2, 0.0).astype(BF16)
        s = _dot_nt(qh, k_loc)
        if n_cache:
            s_c = _dot_nt(qh, kf[t_loc:nk, :])
        yield
        if natten:
            blocks = []
            for a in range(qrows):
                first = ws_row - (g * qrows + a) + NA_ROWS
                blocks.append(jnp.concatenate(
                    [bp_ref[hh, pl.ds(jnp.clip(first + 2 * p, 0, 2 * NA_ROWS - 1), 1)][0]
                     for p in range(win // LANES)], axis=1))
            s = s + jnp.concatenate(blocks, axis=0)
        if keep is not None:
            s = jnp.where(keep, s, NEG_INF)
        m = jnp.max(s, axis=1, keepdims=True)
        if n_cache:
            m = jnp.maximum(m, jnp.max(s_c, axis=1, keepdims=True))
        if has_sink:
            sink = sink_ref[2 * j + hh] * LOG2E
            m = jnp.maximum(m, sink)
        e = jnp.exp2(s - m)
        e_c = jnp.exp2(s_c - m) if n_cache else None
        yield
        if mode == "pair":
            o = _dot(e.astype(BF16), v_scr[hh][sl, :])
            if n_cache:
                o = o + _dot(e_c.astype(BF16), v_scr[hh][t_loc:nk, :])
            yield
            den = pltpu.roll(o, HEAD_DIM, 1)
            if has_sink:
                den = den + jnp.exp2(sink - m)
            done[hh] = o * (1.0 / den)
        else:
            den = jnp.sum(e, axis=1, keepdims=True)
            if n_cache:
                den = den + jnp.sum(e_c, axis=1, keepdims=True)
            done[hh] = (e, e_c, 1.0 / den)

    def sub_tile(t):
        g = qb * n_sub + t
        q2 = q_ref[t * tq:(t + 1) * tq, :].astype(F32) * (QK_SCALE * LOG2E)
        keep = ws_row = None
        if natten:
            ws_row = jnp.clip(g * qrows - NA_ROWS // 2, 0, rows - NA_KEY_ROWS)
            sl = pl.ds(pl.multiple_of(ws_row * GRID_W, GRID_W), win)
            q_row = lax.broadcasted_iota(jnp.int32, (tq, win), 0) // GRID_W + g * qrows
            r0 = jnp.clip(q_row - NA_ROWS // 2, 0, rows - NA_ROWS)
            k_row = lax.broadcasted_iota(jnp.int32, (tq, win), 1) // GRID_W + ws_row
            keep = (k_row >= r0) & (k_row < r0 + NA_ROWS)
        elif band:
            ws = pl.multiple_of(jnp.clip(g * tq - WINDOW, 0, t_loc - win), WINDOW)
            sl = pl.ds(ws, win)
            qpos = g * tq + lax.broadcasted_iota(jnp.int32, (tq, win), 0)
            kpos = ws + lax.broadcasted_iota(jnp.int32, (tq, win), 1)
            keep = jnp.abs(qpos - kpos) <= WINDOW
        else:
            sl = slice(0, win)
        k_loc = kf[sl, :]
        done = [None, None]
        yield from _in_waves([chain(g, hh, q2, sl, k_loc, keep, ws_row, done) for hh in range(2)])
        if mode == "pair":
            out = jnp.where(lane < HEAD_DIM, done[0], done[1])
        else:
            (e0, ec0, w0), (e1, ec1, w1) = done
            w1 = w1 * lam
            out = _dot((e0 * w0 - e1 * w1).astype(BF16), v_scr[0][sl, :])
            if n_cache:
                out = out + _dot((ec0 * w0 - ec1 * w1).astype(BF16), v_scr[0][t_loc:nk, :])
            yield
            ms = jnp.mean(out * out, axis=-1, keepdims=True)
            out = out * lax.rsqrt(ms + EPS) * subln_ref[...] * (1.0 - lam_init)
        o_ref[t * tq:(t + 1) * tq, :] = out.astype(o_ref.dtype)

    yield from _in_waves([sub_tile(t) for t in range(n_sub)])


def attention(qkv, nb, t_loc, *, mode, gqa, k_col, v_col, n_sub, n_lb, cache=None, band=False, bias_pairs=None,
              sink=None, lam=None, subln=None, lam_init=0.0, d_model):
    tq = ATTN_Q_TILE * n_sub
    n_qlb = d_model // LANES
    natten = bias_pairs is not None
    n_cache = 0 if cache is None else cache[0].shape[1]
    if natten:
        win = NA_KEY_ROWS * GRID_W
    elif band:
        win = ATTN_Q_TILE + 2 * WINDOW
    else:
        win = t_loc
    if gqa:
        assert n_lb == 1 or n_lb % 4 == 0
        k_lb = max(n_lb // 4, 1)
        kgrp = (lambda jg: jg // 4) if n_lb == 1 else (lambda jg: jg)
        kmap = lambda jl: jl // 4
    else:
        k_lb = n_lb
        kgrp = lambda jg: jg
        kmap = lambda jl: jl
    qw, kw = n_lb * LANES, k_lb * LANES
    assert k_col % kw == 0 and v_col % kw == 0 and n_qlb % n_lb == 0
    kc, vc = k_col // kw, v_col // kw
    nqb = t_loc // tq
    nk = t_loc + n_cache

    in_specs = []
    args = []
    if sink is not None:
        in_specs.append(pl.BlockSpec(memory_space=pltpu.SMEM))
        args.append(sink)
    in_specs += [pl.BlockSpec((tq, qw), lambda b, j, i: (b * nqb + i, j)),
                 pl.BlockSpec((t_loc, kw), lambda b, j, i: (b, kc + kgrp(j))),
                 pl.BlockSpec((t_loc, kw), lambda b, j, i: (b, vc + kgrp(j)))]
    args += [qkv, qkv, qkv]
    if cache is not None:
        in_specs += [pl.BlockSpec((1, n_cache, kw), lambda b, j, i: (b, 0, kgrp(j))),
                     pl.BlockSpec((1, n_cache, kw), lambda b, j, i: (b, 0, kgrp(j)))]
        args += list(cache)
    if natten:
        in_specs.append(pl.BlockSpec((2 * n_lb, 2 * NA_ROWS, GRID_W, LANES), lambda b, j, i: (j, 0, 0, 0)))
        args.append(bias_pairs)
    if mode == "diff":
        in_specs += [pl.BlockSpec((8, LANES), lambda b, j, i: (0, 0)),
                     pl.BlockSpec((1, LANES), lambda b, j, i: (0, 0))]
        args += [lam, subln]
    kern = functools.partial(
        _attn_kernel, n_lb=n_lb, kmap=kmap, mode=mode, gqa=gqa, t_loc=t_loc, n_cache=n_cache, win=win, band=band,
        natten=natten, has_sink=sink is not None, lam_init=lam_init, n_sub=n_sub)
    n_v = 2 if mode == "pair" else 1
    return pl.pallas_call(
        kern,
        grid=(nb, n_qlb // n_lb, nqb),
        in_specs=in_specs,
        out_specs=pl.BlockSpec((tq, qw), lambda b, j, i: (b * nqb + i, j)),
        out_shape=jax.ShapeDtypeStruct((nb * t_loc, d_model), BF16),
        scratch_shapes=[pltpu.VMEM((n_lb, nk, LANES), BF16) for _ in range(1 + n_v)],
        compiler_params=_cparams(("parallel", "arbitrary", "arbitrary")),
    )(*args)


def _natten_pair_table(rpb):
    h = rpb.shape[0]
    pad = GRID_W - NA_COLS
    ext = jnp.concatenate([jnp.repeat(rpb[..., :1], pad, axis=-1), rpb,
                           jnp.repeat(rpb[..., -1:], pad, axis=-1)], axis=-1).astype(F32)
    b = jnp.stack([ext[..., GRID_W - 1 - qc:2 * GRID_W - 1 - qc] for qc in range(GRID_W)], axis=2)
    col = np.arange(GRID_W)
    cstart = np.clip(col - NA_COLS // 2, 0, GRID_W - NA_COLS)
    cmask = (col[None, :] >= cstart[:, None]) & (col[None, :] < cstart[:, None] + NA_COLS)
    b = jnp.where(jnp.asarray(cmask), b * LOG2E, NEG_INF)
    neg = jnp.full((h, 1, GRID_W, GRID_W), NEG_INF, F32)
    bpad = jnp.concatenate([neg, b, neg], axis=1)
    return jnp.concatenate([bpad[:, :-1], bpad[:, 1:]], axis=-1)


def _mixer_residual(a_ref, wo_ref, x_ref, mod_ref):
    return x_ref[...] + mod_ref[0, 2:3, :] * _dot(a_ref[...], wo_ref[...])


def _ffn_kernel(a_ref, wo_ref, x_ref, mod_ref, g_ref, wg_ref, wu_ref, wd_ref, o_ref, h_scr, acc_scr, x1_scr):
    f = pl.program_id(1)

    @pl.when(f == 0)
    def _():
        x1 = _mixer_residual(a_ref, wo_ref, x_ref, mod_ref)
        x1_scr[...] = x1
        h = _norm_mod(x1, g_ref[...], mod_ref[0, 3:4, :], mod_ref[0, 4:5, :])
        h_scr[...] = h.astype(BF16)

    h = h_scr[...]
    t = (_silu(_dot(h, wg_ref[...])) * _dot(h, wu_ref[...])).astype(BF16)
    y = _dot(t, wd_ref[...])

    @pl.when(f == 0)
    def _():
        acc_scr[...] = y

    @pl.when(f > 0)
    def _():
        acc_scr[...] += y

    @pl.when(f == pl.num_programs(1) - 1)
    def _():
        o_ref[...] = x1_scr[...] + mod_ref[0, 5:6, :] * acc_scr[...]


def dense_ffn(a, wo, x, mod, mod_of_block, g, wg, wu, wd):
    r, d = x.shape
    ff = wg.shape[1]
    tm = ROW_TILE
    nf = 2
    tf = ff // nf
    return pl.pallas_call(
        _ffn_kernel,
        grid=(r // tm, nf),
        in_specs=[pl.BlockSpec((tm, d), lambda i, f: (i, 0)),
                  pl.BlockSpec((d, d), lambda i, f: (0, 0)),
                  pl.BlockSpec((tm, d), lambda i, f: (i, 0)),
                  pl.BlockSpec((1, 6, d), lambda i, f: (mod_of_block(i), 0, 0)),
                  pl.BlockSpec((1, d), lambda i, f: (0, 0)),
                  pl.BlockSpec((d, tf), lambda i, f: (0, f)),
                  pl.BlockSpec((d, tf), lambda i, f: (0, f)),
                  pl.BlockSpec((tf, d), lambda i, f: (f, 0))],
        out_specs=pl.BlockSpec((tm, d), lambda i, f: (i, 0)),
        out_shape=jax.ShapeDtypeStruct((r, d), F32),
        scratch_shapes=[pltpu.VMEM((tm, d), BF16), pltpu.VMEM((tm, d), F32), pltpu.VMEM((tm, d), F32)],
        compiler_params=_cparams(("parallel", "arbitrary")),
    )(a, wo, x, mod, g.reshape(1, d), wg, wu, wd)


def _router_kernel(a_ref, wo_ref, x_ref, mod_ref, g_ref, wr_ref,
                   x1_ref, h_ref, rank_row_ref, gate_row_ref, rank_col_ref, cnt_ref):
    x1 = _mixer_residual(a_ref, wo_ref, x_ref, mod_ref)
    x1_ref[...] = x1
    h = _norm_mod(x1, g_ref[...], mod_ref[0, 3:4, :], mod_ref[0, 4:5, :])
    h_hi = h.astype(BF16)
    h_ref[...] = h_hi
    h_lo = (h - h_hi.astype(F32)).astype(BF16)
    wr = wr_ref[...]
    w_hi = wr.astype(BF16)
    w_lo = (wr - w_hi.astype(F32)).astype(BF16)
    logits = _dot(h_hi, w_hi) + (_dot(h_lo, w_hi) + _dot(h_hi, w_lo))
    tb = logits.shape[0]
    lane = lax.broadcasted_iota(jnp.int32, logits.shape, 1)
    lane_f = lane.astype(F32)
    logits = jnp.where(lane < N_EXPERTS, logits, -jnp.inf)
    m1 = jnp.max(logits, axis=1, keepdims=True)
    i1 = jnp.min(jnp.where(logits == m1, lane_f, float(LANES)), axis=1, keepdims=True)
    rest = jnp.where(lane_f == i1, -jnp.inf, logits)
    m2 = jnp.max(rest, axis=1, keepdims=True)
    i2 = jnp.min(jnp.where(rest == m2, lane_f, float(LANES)), axis=1, keepdims=True)
    e2 = jnp.exp(m2 - m1)
    inv = 1.0 / (1.0 + e2)
    is1 = lane_f == i1
    is2 = lane_f == i2
    gates = jnp.where(is1, inv, 0.0) + jnp.where(is2, e2 * inv, 0.0)
    sel = jnp.where(is1 | is2, 1.0, 0.0)
    sel_b = sel.astype(BF16)
    chunk = 256
    parts = []
    for c in range(tb // chunk):
        ri = lax.broadcasted_iota(jnp.int32, (chunk, tb), 0) + c * chunk
        ci = lax.broadcasted_iota(jnp.int32, (chunk, tb), 1)
        parts.append(_dot(jnp.where(ci < ri, 1.0, 0.0).astype(BF16), sel_b))
    rank = jnp.concatenate(parts, axis=0)
    rank_sel = jnp.where(sel > 0.5, rank, -1.0)
    rank_col_ref[...] = rank_sel
    rank_row_ref[0] = rank_sel.T[:N_EXPERTS]
    gate_row_ref[0] = gates.T[:N_EXPERTS]
    cnt_ref[0] = jnp.broadcast_to(jnp.sum(sel, axis=0, keepdims=True), (8, LANES)).astype(jnp.int32)


def _expert_kernel(cnt_ref, h_ref, rank_row_ref, gate_row_ref, rank_col_ref, wg_ref, wu_ref, wd_ref,
                   x_ref, mod_ref, o_ref, xg, yacc, gs):
    b, e, f = pl.program_id(0), pl.program_id(1), pl.program_id(2)
    nf = pl.num_programs(2)
    tb = h_ref.shape[0]
    m = MOE_SUB
    n_rows = cnt_ref[b * N_EXPERTS + e]
    nsub = (n_rows + m - 1) // m

    @pl.when((e == 0) & (f == 0))
    def _():
        o_ref[...] = jnp.zeros_like(o_ref)

    @pl.when(f == 0)
    def _():
        rank_e = rank_row_ref[0, pl.ds(e, 1), :]
        gate_e = gate_row_ref[0, pl.ds(e, 1), :]

        def gather(i, carry):
            base = pl.multiple_of(i * m, m)
            want = (lax.broadcasted_iota(jnp.int32, (m, tb), 0) + base).astype(F32)
            p = jnp.where(rank_e == want, 1.0, 0.0)
            xg[pl.ds(base, m), :] = _dot(p.astype(BF16), h_ref[...]).astype(BF16)
            gs[pl.ds(base, m), :] = jnp.broadcast_to(jnp.sum(p * gate_e, axis=1, keepdims=True), (m, LANES))
            return carry

        lax.fori_loop(0, nsub, gather, 0)

    def ffn(i, carry):
        base = pl.multiple_of(i * m, m)
        xr = xg[pl.ds(base, m), :]
        t = (_silu(_dot(xr, wg_ref[0])) * _dot(xr, wu_ref[0])).astype(BF16)
        y = _dot(t, wd_ref[0])

        @pl.when(f == 0)
        def _():
            yacc[pl.ds(base, m), :] = y

        @pl.when(f > 0)
        def _():
            yacc[pl.ds(base, m), :] += y

        return carry

    lax.fori_loop(0, nsub, ffn, 0)

    @pl.when(f == nf - 1)
    def _():
        lane = lax.broadcasted_iota(jnp.int32, (tb, LANES), 1)
        rank_c = jnp.sum(jnp.where(lane == e, rank_col_ref[...], 0.0), axis=1, keepdims=True)

        def scatter(i, carry):
            base = pl.multiple_of(i * m, m)
            want = (lax.broadcasted_iota(jnp.int32, (tb, m), 1) + base).astype(F32)
            pt = jnp.where(rank_c == want, 1.0, 0.0).astype(BF16)
            yg = (yacc[pl.ds(base, m), :] * gs[pl.ds(base, m), 0:1]).astype(BF16)
            o_ref[...] += _dot(pt, yg)
            return carry

        lax.fori_loop(0, nsub, scatter, 0)

    @pl.when((e == N_EXPERTS - 1) & (f == nf - 1))
    def _():
        o_ref[...] = x_ref[...] + mod_ref[0, 5:6, :] * o_ref[...]


def moe_ffn(a, wo, x, mod, mod_of_block, g, w_router, wg, wu, wd):
    r, d = x.shape
    ff = wg.shape[2]
    tb = MOE_BLOCK
    nb = r // tb
    wr = jnp.pad(w_router, ((0, 0), (0, LANES - N_EXPERTS)))
    x, h, rank_row, gate_row, rank_col, cnt = pl.pallas_call(
        _router_kernel,
        grid=(nb,),
        in_specs=[pl.BlockSpec((tb, d), lambda i: (i, 0)),
                  pl.BlockSpec((d, d), lambda i: (0, 0)),
                  pl.BlockSpec((tb, d), lambda i: (i, 0)),
                  pl.BlockSpec((1, 6, d), lambda i: (mod_of_block(i), 0, 0)),
                  pl.BlockSpec((1, d), lambda i: (0, 0)),
                  pl.BlockSpec((d, LANES), lambda i: (0, 0))],
        out_specs=[pl.BlockSpec((tb, d), lambda i: (i, 0)),
                   pl.BlockSpec((tb, d), lambda i: (i, 0)),
                   pl.BlockSpec((1, N_EXPERTS, tb), lambda i: (i, 0, 0)),
                   pl.BlockSpec((1, N_EXPERTS, tb), lambda i: (i, 0, 0)),
                   pl.BlockSpec((tb, LANES), lambda i: (i, 0)),
                   pl.BlockSpec((1, 8, LANES), lambda i: (i, 0, 0))],
        out_shape=[jax.ShapeDtypeStruct((r, d), F32),
                   jax.ShapeDtypeStruct((r, d), BF16),
                   jax.ShapeDtypeStruct((nb, N_EXPERTS, tb), F32),
                   jax.ShapeDtypeStruct((nb, N_EXPERTS, tb), F32),
                   jax.ShapeDtypeStruct((r, LANES), F32),
                   jax.ShapeDtypeStruct((nb, 8, LANES), jnp.int32)],
        compiler_params=_cparams(("parallel",)),
    )(a, wo, x, mod, g.reshape(1, d), wr)
    counts = cnt[:, 0, :N_EXPERTS].reshape(nb * N_EXPERTS)

    nf = 2
    tf = ff // nf
    sub_rows = -(-tb // MOE_SUB) * MOE_SUB
    grid_spec = pltpu.PrefetchScalarGridSpec(
        num_scalar_prefetch=1,
        grid=(nb, N_EXPERTS, nf),
        in_specs=[pl.BlockSpec((tb, d), lambda i, e, f, c: (i, 0)),
                  pl.BlockSpec((1, N_EXPERTS, tb), lambda i, e, f, c: (i, 0, 0)),
                  pl.BlockSpec((1, N_EXPERTS, tb), lambda i, e, f, c: (i, 0, 0)),
                  pl.BlockSpec((tb, LANES), lambda i, e, f, c: (i, 0)),
                  pl.BlockSpec((1, d, tf), lambda i, e, f, c: (e, 0, f)),
                  pl.BlockSpec((1, d, tf), lambda i, e, f, c: (e, 0, f)),
                  pl.BlockSpec((1, tf, d), lambda i, e, f, c: (e, f, 0)),
                  pl.BlockSpec((tb, d), lambda i, e, f, c: (i, 0)),
                  pl.BlockSpec((1, 6, d), lambda i, e, f, c: (mod_of_block(i), 0, 0))],
        out_specs=pl.BlockSpec((tb, d), lambda i, e, f, c: (i, 0)),
        scratch_shapes=[pltpu.VMEM((sub_rows, d), BF16), pltpu.VMEM((sub_rows, d), F32),
                        pltpu.VMEM((sub_rows, LANES), F32)],
    )
    return pl.pallas_call(
        _expert_kernel,
        grid_spec=grid_spec,
        out_shape=jax.ShapeDtypeStruct((r, d), F32),
        compiler_params=_cparams(("parallel", "arbitrary", "arbitrary")),
    )(counts, h, rank_row, gate_row, rank_col, wg, wu, wd, x, mod)


def _diff_lambda_init(layer):
    return 0.8 - 0.6 * math.exp(-0.3 * layer)


def _head_gain_rows(q_norm, k_norm, n_q, n_k, n_v):
    hn = jnp.concatenate([jnp.tile(q_norm.astype(F32), n_q // HEAD_DIM),
                          jnp.tile(k_norm.astype(F32), n_k // HEAD_DIM),
                          jnp.ones((n_v,), F32)]).reshape(1, -1)
    isn = jnp.concatenate([jnp.ones((n_q + n_k,), F32), jnp.zeros((n_v,), F32)]).reshape(1, -1)
    return hn, isn


def kernel(x_prompt, x_sample, cache_k_0, cache_v_0, cache_k_1, cache_v_1, cache_k_2, cache_v_2, cache_k_3, cache_v_3, c, c_ctx, norm1_0, w_ada_0, b_ada_0, w_qkv_0, q_norm_0, k_norm_0, rpb_0, w_o_0, norm2_0, w_gate_0, w_up_0, w_down_0, norm1_1, w_ada_1, b_ada_1, w_qkv_1, q_norm_1, k_norm_1, lam_q1_1, lam_k1_1, lam_q2_1, lam_k2_1, subln_1, w_o_1, norm2_1, w_router_1, w_egate_1, w_eup_1, w_edown_1, norm1_2, w_ada_2, b_ada_2, w_qkv_2, q_norm_2, k_norm_2, sink_2, w_o_2, norm2_2, w_gate_2, w_up_2, w_down_2, norm1_3, w_ada_3, b_ada_3, w_qkv_3, q_norm_3, k_norm_3, w_o_3, norm2_3, w_router_3, w_egate_3, w_eup_3, w_edown_3):
    nbc, tc, d = x_prompt.shape
    nbl, tl, _ = x_sample.shape
    n_past = cache_k_0.shape[1]
    assert d % (2 * LANES) == 0 and tc % ATTN_Q_TILE == 0 and tl % MOE_BLOCK == 0
    assert (nbc * tc) % MOE_BLOCK == 0 and tl // GRID_W >= NA_KEY_ROWS and nbl <= 7

    norm1 = (norm1_0, norm1_1, norm1_2, norm1_3)
    norm2 = (norm2_0, norm2_1, norm2_2, norm2_3)
    ada_p = ((w_ada_0, b_ada_0), (w_ada_1, b_ada_1), (w_ada_2, b_ada_2), (w_ada_3, b_ada_3))
    w_qkv = (w_qkv_0, w_qkv_1, w_qkv_2, w_qkv_3)
    qk_norm = ((q_norm_0, k_norm_0), (q_norm_1, k_norm_1), (q_norm_2, k_norm_2), (q_norm_3, k_norm_3))
    w_out = (w_o_0, w_o_1, w_o_2, w_o_3)
    ffn_p = ((w_gate_0, w_up_0, w_down_0), (w_router_1, w_egate_1, w_eup_1, w_edown_1),
             (w_gate_2, w_up_2, w_down_2), (w_router_3, w_egate_3, w_eup_3, w_edown_3))
    caches = ((cache_k_0, cache_v_0), (cache_k_1, cache_v_1), (cache_k_2, cache_v_2), (cache_k_3, cache_v_3))

    xc = x_prompt.reshape(nbc * tc, d)
    xl = x_sample.reshape(nbl * tl, d)
    cond = jnp.concatenate([c_ctx[None, :], c, jnp.zeros((7 - nbl, d), F32)], axis=0)
    cond16 = jnp.concatenate([cond, cond], axis=0)

    def ctx_mod(blk_rows):
        return lambda i: 0

    def lat_mod(blk_rows):
        per = tl // blk_rows
        return lambda i: 1 + i // per

    lam_rows = jnp.zeros((8, LANES), F32)
    for r_, v_ in enumerate((lam_q1_1, lam_k1_1, lam_q2_1, lam_k2_1)):
        lam_rows = lam_rows.at[r_, :HEAD_DIM].set(v_.astype(F32))
    bias_pairs = _natten_pair_table(rpb_0)

    states = []
    for l in range(4):
        mixer = l % 4
        mod = ada_mod(cond16, *ada_p[l])
        wq = w_qkv[l].astype(BF16)
        n = wq.shape[1]
        n_q = d
        n_k = d if mixer < 2 else (n - d) // 2
        n_v = n - n_q - n_k
        tn = 1024 if n % 1024 == 0 else 512
        hn, isn = _head_gain_rows(*qk_norm[l], n_q, n_k, n_v)
        rope = None if mixer == 0 else _rope_tables(tl)

        qkv_c = qkv_proj(xc, mod, ctx_mod(ROW_TILE), norm1[l], wq, hn, isn, None, F32, tn)
        qkv_l = qkv_proj(xl, mod, lat_mod(ROW_TILE), norm1[l], wq, hn, isn, rope, BF16, tn)
        states.append((qkv_c[:, n_q:n_q + n_k], qkv_c[:, n_q + n_k:]))

        ck, cv = caches[l]
        cache = (ck.astype(BF16).reshape(nbl, n_past, n_k), cv.astype(BF16).reshape(nbl, n_past, n_v))
        common = dict(k_col=n_q, v_col=n_q + n_k, d_model=d)
        ctx = dict(n_sub=1, n_lb=CTX_LANE_BLOCKS, **common)
        lat = dict(n_sub=4, n_lb=1, cache=cache, **common)
        if mixer == 0:
            o_c = attention(qkv_c, nbc, tc, mode="pair", gqa=False, **ctx)
            o_l = attention(qkv_l, nbl, tl, mode="pair", gqa=False, bias_pairs=bias_pairs, **lat)
        elif mixer == 1:
            diff = dict(mode="diff", gqa=False, lam=lam_rows, subln=subln_1.astype(F32).reshape(1, LANES),
                        lam_init=_diff_lambda_init(l))
            o_c = attention(qkv_c, nbc, tc, **diff, **ctx)
            o_l = attention(qkv_l, nbl, tl, **diff, **lat)
        elif mixer == 2:
            sk = sink_2.astype(F32)
            o_c = attention(qkv_c, nbc, tc, mode="pair", gqa=True, sink=sk, **ctx)
            o_l = attention(qkv_l, nbl, tl, mode="pair", gqa=True, band=True, sink=sk, **lat)
        else:
            o_c = attention(qkv_c, nbc, tc, mode="pair", gqa=True, **ctx)
            o_l = attention(qkv_l, nbl, tl, mode="pair", gqa=True, **lat)

        wo = w_out[l].astype(BF16)
        if l % 2 == 0:
            wg, wu, wd = (w.astype(BF16) for w in ffn_p[l])
            xc = dense_ffn(o_c, wo, xc, mod, ctx_mod(ROW_TILE), norm2[l], wg, wu, wd)
            xl = dense_ffn(o_l, wo, xl, mod, lat_mod(ROW_TILE), norm2[l], wg, wu, wd)
        else:
            wr = ffn_p[l][0]
            wg, wu, wd = (w.astype(BF16) for w in ffn_p[l][1:])
            xc = moe_ffn(o_c, wo, xc, mod, ctx_mod(MOE_BLOCK), norm2[l], wr, wg, wu, wd)
            xl = moe_ffn(o_l, wo, xl, mod, lat_mod(MOE_BLOCK), norm2[l], wr, wg, wu, wd)

    nh = d // HEAD_DIM
    k0, v0 = states[0]
    k1, v1 = states[1]
    k2, v2 = states[2]
    k3, v3 = states[3]
    nkv = k2.shape[1] // HEAD_DIM
    return (xc.reshape(nbc, tc, d), xl.reshape(nbl, tl, d),
            k0.reshape(nbc, tc, nh, HEAD_DIM), v0.reshape(nbc, tc, nh, HEAD_DIM),
            k1.reshape(nbc, tc, nh // 2, 2, HEAD_DIM), v1.reshape(nbc, tc, nh // 2, 2 * HEAD_DIM),
            k2.reshape(nbc, tc, nkv, HEAD_DIM), v2.reshape(nbc, tc, nkv, HEAD_DIM),
            k3.reshape(nbc, tc, nkv, HEAD_DIM), v3.reshape(nbc, tc, nkv, HEAD_DIM))
```

```python
import functools
import math

import numpy as np
import jax
import jax.numpy as jnp
from jax import lax
from jax.experimental import pallas as pl
from jax.experimental.pallas import tpu as pltpu

F32 = jnp.float32
BF16 = jnp.bfloat16

HEAD_DIM = 64
GRID_W = 64
NA_ROWS = 8
NA_COLS = 16
WINDOW = 128
ROPE_BASE = 10000.0
ROPE_PAIRS_AXIS = HEAD_DIM // 4
N_EXPERTS = 8
EPS = 1e-6
NEG_INF = -1e30
QK_SCALE = HEAD_DIM ** -0.5
LOG2E = math.log2(math.e)

LANES = 128
V7X_VMEM_BYTES = 64 * 1024 * 1024
VMEM_LIMIT = V7X_VMEM_BYTES - 8 * 1024 * 1024

ROW_TILE = 512
ATTN_Q_TILE = 256
NA_KEY_ROWS = 12
CTX_LANE_BLOCKS = 8
MOE_BLOCK = 1024
MOE_SUB = 256
MOE_FF_SPLIT = 1


def _cparams(sem):
    return pltpu.CompilerParams(dimension_semantics=sem, vmem_limit_bytes=VMEM_LIMIT)


def _silu(x):
    return x * (1.0 / (1.0 + jnp.exp(-x)))


def _norm_mod(x, g, shift, scale):
    ms = jnp.mean(x * x, axis=-1, keepdims=True)
    y = x * lax.rsqrt(ms + EPS) * g
    return y * (1.0 + scale) + shift


def _dot(a, b):
    return jnp.dot(a, b, preferred_element_type=F32)


def _dot_nt(a, b):
    return lax.dot_general(a, b, (((1,), (1,)), ((), ())), preferred_element_type=F32)


def _ada_kernel(c_ref, w_ref, b_ref, o_ref):
    a = _silu(c_ref[...])
    a_hi = a.astype(BF16)
    a_lo = a - a_hi.astype(F32)
    row = lax.broadcasted_iota(jnp.int32, a.shape, 0)
    lhs = jnp.where(row < 8, a_hi.astype(F32), a_lo).astype(BF16)
    w = w_ref[...]
    w_hi = w.astype(BF16)
    w_lo = (w - w_hi.astype(F32)).astype(BF16)
    r = _dot(lhs, w_hi)
    r2 = _dot(a_hi, w_lo)
    o_ref[...] = r[:8] + r[8:] + r2[:8] + b_ref[...]


def ada_mod(cond16, w_ada, b_ada):
    d, n = w_ada.shape
    tn = 1024
    out = pl.pallas_call(
        _ada_kernel,
        grid=(n // tn,),
        in_specs=[pl.BlockSpec((16, d), lambda j: (0, 0)),
                  pl.BlockSpec((d, tn), lambda j: (0, j)),
                  pl.BlockSpec((1, tn), lambda j: (0, j))],
        out_specs=pl.BlockSpec((8, tn), lambda j: (0, j)),
        out_shape=jax.ShapeDtypeStruct((8, n), F32),
        compiler_params=_cparams(("arbitrary",)),
    )(cond16, w_ada, b_ada.reshape(1, n))
    return out.reshape(8, 6, d)


def _qkv_kernel(*refs, rope, n_plain_tiles):
    if rope:
        x_ref, mod_ref, g_ref, w_ref, seg_ref, hn_ref, isn_ref, cos_ref, sin_ref, o_ref, h_scr = refs
    else:
        x_ref, mod_ref, g_ref, w_ref, seg_ref, hn_ref, isn_ref, o_ref, h_scr = refs

    @pl.when(pl.program_id(1) == 0)
    def _():
        h = _norm_mod(x_ref[...], g_ref[...], mod_ref[0, 0:1, :], mod_ref[0, 1:2, :])
        h_scr[...] = h.astype(BF16)

    acc = _dot(h_scr[...], w_ref[...])
    tn = acc.shape[1]

    def normed():
        sq = (acc * acc).astype(BF16)
        seg = seg_ref[...]
        ssum = jnp.concatenate(
            [_dot(sq[:, c * 2 * LANES:(c + 1) * 2 * LANES], seg) for c in range(tn // (2 * LANES))], axis=1)
        y = acc * lax.rsqrt(ssum * (1.0 / HEAD_DIM) + EPS) * hn_ref[...]
        if rope:
            lane = lax.broadcasted_iota(jnp.int32, (acc.shape[0], LANES), 1)
            even = (lane % 2) == 0
            cos, sin = cos_ref[...], sin_ref[...]
            parts = []
            for c in range(tn // LANES):
                yc = y[:, c * LANES:(c + 1) * LANES]
                nxt = pltpu.roll(yc, LANES - 1, 1)
                prv = pltpu.roll(yc, 1, 1)
                parts.append(yc * cos + jnp.where(even, nxt, prv) * sin)
            y = jnp.concatenate(parts, axis=1)
        o_ref[...] = jnp.where(isn_ref[...] > 0.5, y, acc).astype(o_ref.dtype)

    if n_plain_tiles == 0:
        normed()
    else:
        first_plain = pl.num_programs(1) - n_plain_tiles
        pl.when(pl.program_id(1) < first_plain)(normed)

        @pl.when(pl.program_id(1) >= first_plain)
        def _():
            o_ref[...] = acc.astype(o_ref.dtype)


def _seg_matrix():
    i = np.arange(2 * LANES)
    return jnp.asarray((i[:, None] // HEAD_DIM) == (i[None, :] // HEAD_DIM), BF16)


def qkv_proj(x, mod, mod_of_block, g, w, hn, isn, rope_tabs, out_dtype, tn, n_plain):
    r, d = x.shape
    n = w.shape[1]
    tm = ROW_TILE
    rope = rope_tabs is not None
    in_specs = [pl.BlockSpec((tm, d), lambda i, j: (i, 0)),
                pl.BlockSpec((1, 6, d), lambda i, j: (mod_of_block(i), 0, 0)),
                pl.BlockSpec((1, d), lambda i, j: (0, 0)),
                pl.BlockSpec((d, tn), lambda i, j: (0, j)),
                pl.BlockSpec((2 * LANES, 2 * LANES), lambda i, j: (0, 0)),
                pl.BlockSpec((1, tn), lambda i, j: (0, j)),
                pl.BlockSpec((1, tn), lambda i, j: (0, j))]
    args = [x, mod, g.reshape(1, d), w, _seg_matrix(), hn, isn]
    if rope:
        cos_t, sin_t = rope_tabs
        nblk = cos_t.shape[0] // tm
        in_specs += [pl.BlockSpec((tm, LANES), lambda i, j: (i % nblk, 0)),
                     pl.BlockSpec((tm, LANES), lambda i, j: (i % nblk, 0))]
        args += [cos_t, sin_t]
    return pl.pallas_call(
        functools.partial(_qkv_kernel, rope=rope, n_plain_tiles=n_plain // tn),
        grid=(r // tm, n // tn),
        in_specs=in_specs,
        out_specs=pl.BlockSpec((tm, tn), lambda i, j: (i, j)),
        out_shape=jax.ShapeDtypeStruct((r, n), out_dtype),
        scratch_shapes=[pltpu.VMEM((tm, d), BF16)],
        compiler_params=_cparams(("parallel", "arbitrary")),
    )(*args)


def _rope_tables(t):
    pos = np.arange(t)
    row = (pos // GRID_W).astype(np.float32)
    col = (pos % GRID_W).astype(np.float32)
    freqs = jnp.asarray(ROPE_BASE, F32) ** (-jnp.arange(ROPE_PAIRS_AXIS, dtype=F32) / ROPE_PAIRS_AXIS)
    ang = jnp.concatenate([jnp.asarray(row)[:, None] * freqs, jnp.asarray(col)[:, None] * freqs], axis=-1)
    cos = jnp.repeat(jnp.cos(ang), 2, axis=-1)
    sin = jnp.repeat(jnp.sin(ang), 2, axis=-1)
    sign = jnp.asarray(np.where(np.arange(HEAD_DIM) % 2 == 0, -1.0, 1.0), F32)
    reps = LANES // HEAD_DIM
    return jnp.tile(cos, (1, reps)), jnp.tile(sin * sign, (1, reps))


def _in_waves(gens):
    while gens:
        gens = [g for g in gens if next(g, StopIteration) is not StopIteration]
        yield


def _attn_kernel(*refs, n_lb, kmap, mode, n_cache, natten, has_sink, **static):
    it = iter(refs)
    sink_ref = next(it) if has_sink else None
    q_ref, k_ref, v_ref = next(it), next(it), next(it)
    ck_ref, cv_ref = (next(it), next(it)) if n_cache else (None, None)
    bp_ref = next(it) if natten else None
    lam_refs = [next(it), next(it)] if mode == "diff" else []
    o_ref, kf = next(it), next(it)
    v_scr = list(it)
    gens = []
    for jl in range(n_lb):
        q_cols = pl.ds(jl * LANES, LANES)
        k_cols = pl.ds(kmap(jl) * LANES, LANES)
        views = [sink_ref] if has_sink else []
        views += [q_ref.at[:, q_cols], k_ref.at[:, k_cols], v_ref.at[:, k_cols]]
        if n_cache:
            views += [ck_ref.at[:, :, k_cols], cv_ref.at[:, :, k_cols]]
        if natten:
            views.append(bp_ref.at[pl.ds(2 * jl, 2)])
        views += lam_refs + [o_ref.at[:, q_cols], kf.at[jl]] + [v.at[jl] for v in v_scr]
        gens.append(_attn_lane_block(views, pl.program_id(1) * n_lb + jl, pl.program_id(2), mode=mode,
                                     n_cache=n_cache, natten=natten, has_sink=has_sink, **static))
    for _ in _in_waves(gens):
        pass


def _attn_lane_block(refs, j, qb, *, mode, gqa, t_loc, n_cache, win, band, natten, has_sink, lam_init, n_sub):
    it = iter(refs)
    sink_ref = next(it) if has_sink else None
    q_ref, k_ref, v_ref = next(it), next(it), next(it)
    ck_ref = cv_ref = bp_ref = lam_ref = subln_ref = None
    if n_cache:
        ck_ref, cv_ref = next(it), next(it)
    if natten:
        bp_ref = next(it)
    if mode == "diff":
        lam_ref, subln_ref = next(it), next(it)
    o_ref, kf = next(it), next(it)
    v_scr = [next(it), next(it)] if mode == "pair" else [next(it)]

    tq = ATTN_Q_TILE
    nk = t_loc + n_cache
    rows = t_loc // GRID_W
    qrows = tq // GRID_W

    def place(a):
        if not gqa:
            return a
        half = (j // 2) % 2
        lane_half = (lax.broadcasted_iota(jnp.int32, a.shape, 1) >= HEAD_DIM).astype(jnp.int32)
        return jnp.where(lane_half == half, a, pltpu.roll(a, HEAD_DIM, 1))

    def put_v(lo, hi, a):
        if mode == "pair":
            ln = lax.broadcasted_iota(jnp.int32, a.shape, 1)
            v_scr[0][lo:hi, :] = jnp.where(ln < HEAD_DIM, a, 1.0).astype(BF16)
            v_scr[1][lo:hi, :] = jnp.where(ln >= HEAD_DIM, a, 1.0).astype(BF16)
        else:
            v_scr[0][lo:hi, :] = a.astype(BF16)

    @pl.when(qb == 0)
    def _():
        kf[0:t_loc, :] = place(k_ref[...].astype(F32)).astype(BF16)
        put_v(0, t_loc, place(v_ref[...].astype(F32)))
        if n_cache:
            kf[t_loc:nk, :] = place(ck_ref[0].astype(F32)).astype(BF16)
            put_v(t_loc, nk, place(cv_ref[0].astype(F32)))

    lane = lax.broadcasted_iota(jnp.int32, (tq, LANES), 1)
    if mode == "diff":
        lp = lam_ref[...]
        lam = (jnp.exp(jnp.sum(lp[0:1] * lp[1:2], axis=1, keepdims=True))
               - jnp.exp(jnp.sum(lp[2:3] * lp[3:4], axis=1, keepdims=True)) + lam_init)

    def chain(g, hh, q2, sl, k_loc, keep, ws_row, done):
        in_half = (lane < HEAD_DIM) if hh == 0 else (lane >= HEAD_DIM)
        qh = jnp.where(in_half, q2, 0.0).astype(BF16)
        s = _dot_nt(qh, k_loc)
        if n_cache:
            s_c = _dot_nt(qh, kf[t_loc:nk, :])
        yield
        if natten:
            blocks = []
            for a in range(qrows):
                first = ws_row - (g * qrows + a) + NA_ROWS
                blocks.append(jnp.concatenate(
                    [bp_ref[hh, pl.ds(jnp.clip(first + 2 * p, 0, 2 * NA_ROWS - 1), 1)][0]
                     for p in range(win // LANES)], axis=1))
            s = s + jnp.concatenate(blocks, axis=0)
        if keep is not None:
            s = jnp.where(keep, s, NEG_INF)
        m = jnp.max(s, axis=1, keepdims=True)
        if n_cache:
            m = jnp.maximum(m, jnp.max(s_c, axis=1, keepdims=True))
        if has_sink:
            sink = sink_ref[2 * j + hh] * LOG2E
            m = jnp.maximum(m, sink)
        e = jnp.exp2(s - m)
        e_c = jnp.exp2(s_c - m) if n_cache else None
        yield
        if mode == "pair":
            o = _dot(e.astype(BF16), v_scr[hh][sl, :])
            if n_cache:
                o = o + _dot(e_c.astype(BF16), v_scr[hh][t_loc:nk, :])
            yield
            den = pltpu.roll(o, HEAD_DIM, 1)
            if has_sink:
                den = den + jnp.exp2(sink - m)
            done[hh] = o * (1.0 / den)
        else:
            den = jnp.sum(e, axis=1, keepdims=True)
            if n_cache:
                den = den + jnp.sum(e_c, axis=1, keepdims=True)
            done[hh] = (e, e_c, 1.0 / den)

    def sub_tile(t):
        g = qb * n_sub + t
        q2 = q_ref[t * tq:(t + 1) * tq, :].astype(F32) * (QK_SCALE * LOG2E)
        keep = ws_row = None
        if natten:
            ws_row = jnp.clip(g * qrows - NA_ROWS // 2, 0, rows - NA_KEY_ROWS)
            sl = pl.ds(pl.multiple_of(ws_row * GRID_W, GRID_W), win)
            q_row = lax.broadcasted_iota(jnp.int32, (tq, win), 0) // GRID_W + g * qrows
            r0 = jnp.clip(q_row - NA_ROWS // 2, 0, rows - NA_ROWS)
            k_row = lax.broadcasted_iota(jnp.int32, (tq, win), 1) // GRID_W + ws_row
            keep = (k_row >= r0) & (k_row < r0 + NA_ROWS)
        elif band:
            ws = pl.multiple_of(jnp.clip(g * tq - WINDOW, 0, t_loc - win), WINDOW)
            sl = pl.ds(ws, win)
            qpos = g * tq + lax.broadcasted_iota(jnp.int32, (tq, win), 0)
            kpos = ws + lax.broadcasted_iota(jnp.int32, (tq, win), 1)
            keep = jnp.abs(qpos - kpos) <= WINDOW
        else:
            sl = slice(0, win)
        k_loc = kf[sl, :]
        done = [None, None]
        yield from _in_waves([chain(g, hh, q2, sl, k_loc, keep, ws_row, done) for hh in range(2)])
        if mode == "pair":
            out = jnp.where(lane < HEAD_DIM, done[0], done[1])
        else:
            (e0, ec0, w0), (e1, ec1, w1) = done
            w1 = w1 * lam
            out = _dot((e0 * w0 - e1 * w1).astype(BF16), v_scr[0][sl, :])
            if n_cache:
                out = out + _dot((ec0 * w0 - ec1 * w1).astype(BF16), v_scr[0][t_loc:nk, :])
            yield
            ms = jnp.mean(out * out, axis=-1, keepdims=True)
            out = out * lax.rsqrt(ms + EPS) * subln_ref[...] * (1.0 - lam_init)
        o_ref[t * tq:(t + 1) * tq, :] = out.astype(o_ref.dtype)

    yield from _in_waves([sub_tile(t) for t in range(n_sub)])


def attention(qkv, nb, t_loc, *, mode, gqa, k_col, v_col, n_sub, n_lb, cache=None, band=False, bias_pairs=None,
              sink=None, lam=None, subln=None, lam_init=0.0, d_model):
    tq = ATTN_Q_TILE * n_sub
    n_qlb = d_model // LANES
    natten = bias_pairs is not None
    n_cache = 0 if cache is None else cache[0].shape[1]
    if natten:
        win = NA_KEY_ROWS * GRID_W
    elif band:
        win = ATTN_Q_TILE + 2 * WINDOW
    else:
        win = t_loc
    if gqa:
        assert n_lb == 1 or n_lb % 4 == 0
        k_lb = max(n_lb // 4, 1)
        kgrp = (lambda jg: jg // 4) if n_lb == 1 else (lambda jg: jg)
        kmap = lambda jl: jl // 4
    else:
        k_lb = n_lb
        kgrp = lambda jg: jg
        kmap = lambda jl: jl
    qw, kw = n_lb * LANES, k_lb * LANES
    assert k_col % kw == 0 and v_col % kw == 0 and n_qlb % n_lb == 0
    kc, vc = k_col // kw, v_col // kw
    nqb = t_loc // tq
    nk = t_loc + n_cache

    in_specs = []
    args = []
    if sink is not None:
        in_specs.append(pl.BlockSpec(memory_space=pltpu.SMEM))
        args.append(sink)
    in_specs += [pl.BlockSpec((tq, qw), lambda b, j, i: (b * nqb + i, j)),
                 pl.BlockSpec((t_loc, kw), lambda b, j, i: (b, kc + kgrp(j))),
                 pl.BlockSpec((t_loc, kw), lambda b, j, i: (b, vc + kgrp(j)))]
    args += [qkv, qkv, qkv]
    if cache is not None:
        in_specs += [pl.BlockSpec((1, n_cache, kw), lambda b, j, i: (b, 0, kgrp(j))),
                     pl.BlockSpec((1, n_cache, kw), lambda b, j, i: (b, 0, kgrp(j)))]
        args += list(cache)
    if natten:
        in_specs.append(pl.BlockSpec((2 * n_lb, 2 * NA_ROWS, GRID_W, LANES), lambda b, j, i: (j, 0, 0, 0)))
        args.append(bias_pairs)
    if mode == "diff":
        in_specs += [pl.BlockSpec((8, LANES), lambda b, j, i: (0, 0)),
                     pl.BlockSpec((1, LANES), lambda b, j, i: (0, 0))]
        args += [lam, subln]
    kern = functools.partial(
        _attn_kernel, n_lb=n_lb, kmap=kmap, mode=mode, gqa=gqa, t_loc=t_loc, n_cache=n_cache, win=win, band=band,
        natten=natten, has_sink=sink is not None, lam_init=lam_init, n_sub=n_sub)
    n_v = 2 if mode == "pair" else 1
    return pl.pallas_call(
        kern,
        grid=(nb, n_qlb // n_lb, nqb),
        in_specs=in_specs,
        out_specs=pl.BlockSpec((tq, qw), lambda b, j, i: (b * nqb + i, j)),
        out_shape=jax.ShapeDtypeStruct((nb * t_loc, d_model), BF16),
        scratch_shapes=[pltpu.VMEM((n_lb, nk, LANES), BF16) for _ in range(1 + n_v)],
        compiler_params=_cparams(("parallel", "arbitrary", "arbitrary")),
    )(*args)


def _natten_pair_table(rpb):
    h, n_dr, n_dc = rpb.shape
    pad = GRID_W - NA_COLS
    ext = jnp.concatenate([jnp.repeat(rpb[..., :1], pad, axis=-1), rpb,
                           jnp.repeat(rpb[..., -1:], pad + 1, axis=-1)], axis=-1).astype(F32) * LOG2E
    neg = jnp.full((h, 1, LANES), NEG_INF, F32)
    ext = jnp.concatenate([neg, ext, neg], axis=1)
    return pl.pallas_call(
        _pair_table_kernel,
        grid=(h,),
        in_specs=[pl.BlockSpec((1, n_dr + 2, LANES), lambda i: (i, 0, 0))],
        out_specs=pl.BlockSpec((1, n_dr + 1, GRID_W, LANES), lambda i: (i, 0, 0, 0)),
        out_shape=jax.ShapeDtypeStruct((h, n_dr + 1, GRID_W, LANES), F32),
        compiler_params=_cparams(("parallel",)),
    )(ext)


def _pair_table_kernel(ext_ref, o_ref):
    qc = lax.broadcasted_iota(jnp.int32, (GRID_W, LANES), 0)
    ln = lax.broadcasted_iota(jnp.int32, (GRID_W, LANES), 1)
    kc = ln % GRID_W
    cstart = jnp.clip(qc - NA_COLS // 2, 0, GRID_W - NA_COLS)
    in_window = (kc >= cstart) & (kc < cstart + NA_COLS)
    for d in range(o_ref.shape[1]):
        lo = jnp.broadcast_to(ext_ref[0, d:d + 1, :], (GRID_W, LANES))
        hi = jnp.broadcast_to(ext_ref[0, d + 1:d + 2, :], (GRID_W, LANES))
        lo = pltpu.roll(lo, GRID_W + 1, 1, stride=1, stride_axis=0)
        hi = pltpu.roll(hi, 1, 1, stride=1, stride_axis=0)
        o_ref[0, d] = jnp.where(in_window, jnp.where(ln < GRID_W, lo, hi), NEG_INF)


def _mixer_residual(a_ref, wo_ref, x_ref, mod_ref):
    return x_ref[...] + mod_ref[0, 2:3, :] * _dot(a_ref[...], wo_ref[...])


def _ffn_kernel(a_ref, wo_ref, x_ref, mod_ref, g_ref, wg_ref, wu_ref, wd_ref, o_ref, h_scr, acc_scr, x1_scr):
    f = pl.program_id(1)

    @pl.when(f == 0)
    def _():
        x1 = _mixer_residual(a_ref, wo_ref, x_ref, mod_ref)
        x1_scr[...] = x1
        h = _norm_mod(x1, g_ref[...], mod_ref[0, 3:4, :], mod_ref[0, 4:5, :])
        h_scr[...] = h.astype(BF16)

    h = h_scr[...]
    t = (_silu(_dot(h, wg_ref[...])) * _dot(h, wu_ref[...])).astype(BF16)
    y = _dot(t, wd_ref[...])

    @pl.when(f == 0)
    def _():
        acc_scr[...] = y

    @pl.when(f > 0)
    def _():
        acc_scr[...] += y

    @pl.when(f == pl.num_programs(1) - 1)
    def _():
        o_ref[...] = x1_scr[...] + mod_ref[0, 5:6, :] * acc_scr[...]


def dense_ffn(a, wo, x, mod, mod_of_block, g, wg, wu, wd):
    r, d = x.shape
    ff = wg.shape[1]
    tm = ROW_TILE
    nf = 2
    tf = ff // nf
    return pl.pallas_call(
        _ffn_kernel,
        grid=(r // tm, nf),
        in_specs=[pl.BlockSpec((tm, d), lambda i, f: (i, 0)),
                  pl.BlockSpec((d, d), lambda i, f: (0, 0)),
                  pl.BlockSpec((tm, d), lambda i, f: (i, 0)),
                  pl.BlockSpec((1, 6, d), lambda i, f: (mod_of_block(i), 0, 0)),
                  pl.BlockSpec((1, d), lambda i, f: (0, 0)),
                  pl.BlockSpec((d, tf), lambda i, f: (0, f)),
                  pl.BlockSpec((d, tf), lambda i, f: (0, f)),
                  pl.BlockSpec((tf, d), lambda i, f: (f, 0))],
        out_specs=pl.BlockSpec((tm, d), lambda i, f: (i, 0)),
        out_shape=jax.ShapeDtypeStruct((r, d), F32),
        scratch_shapes=[pltpu.VMEM((tm, d), BF16), pltpu.VMEM((tm, d), F32), pltpu.VMEM((tm, d), F32)],
        compiler_params=_cparams(("parallel", "arbitrary")),
    )(a, wo, x, mod, g.reshape(1, d), wg, wu, wd)


def _router_kernel(a_ref, wo_ref, x_ref, mod_ref, g_ref, wr_ref,
                   x1_ref, h_ref, rank_row_ref, gate_row_ref, rank_col_ref, cnt_ref):
    x1 = _mixer_residual(a_ref, wo_ref, x_ref, mod_ref)
    x1_ref[...] = x1
    h = _norm_mod(x1, g_ref[...], mod_ref[0, 3:4, :], mod_ref[0, 4:5, :])
    h_hi = h.astype(BF16)
    h_ref[...] = h_hi
    h_lo = (h - h_hi.astype(F32)).astype(BF16)
    wr = wr_ref[...]
    w_hi = wr.astype(BF16)
    w_lo = (wr - w_hi.astype(F32)).astype(BF16)
    logits = _dot(h_hi, w_hi) + (_dot(h_lo, w_hi) + _dot(h_hi, w_lo))
    tb = logits.shape[0]
    lane = lax.broadcasted_iota(jnp.int32, logits.shape, 1)
    lane_f = lane.astype(F32)
    logits = jnp.where(lane < N_EXPERTS, logits, -jnp.inf)
    m1 = jnp.max(logits, axis=1, keepdims=True)
    i1 = jnp.min(jnp.where(logits == m1, lane_f, float(LANES)), axis=1, keepdims=True)
    rest = jnp.where(lane_f == i1, -jnp.inf, logits)
    m2 = jnp.max(rest, axis=1, keepdims=True)
    i2 = jnp.min(jnp.where(rest == m2, lane_f, float(LANES)), axis=1, keepdims=True)
    e2 = jnp.exp(m2 - m1)
    inv = 1.0 / (1.0 + e2)
    is1 = lane_f == i1
    is2 = lane_f == i2
    gates = jnp.where(is1, inv, 0.0) + jnp.where(is2, e2 * inv, 0.0)
    sel = jnp.where(is1 | is2, 1.0, 0.0)
    sel_b = sel.astype(BF16)
    chunk = 256
    parts = []
    for c in range(tb // chunk):
        ri = lax.broadcasted_iota(jnp.int32, (chunk, tb), 0) + c * chunk
        ci = lax.broadcasted_iota(jnp.int32, (chunk, tb), 1)
        parts.append(_dot(jnp.where(ci < ri, 1.0, 0.0).astype(BF16), sel_b))
    rank = jnp.concatenate(parts, axis=0)
    rank_sel = jnp.where(sel > 0.5, rank, -1.0)
    rank_col_ref[...] = rank_sel
    rank_row_ref[0] = rank_sel.T[:N_EXPERTS]
    gate_row_ref[0] = gates.T[:N_EXPERTS]
    cnt_ref[0] = jnp.broadcast_to(jnp.sum(sel, axis=0, keepdims=True), (8, LANES)).astype(jnp.int32)


def _expert_kernel(cnt_ref, h_ref, rank_row_ref, gate_row_ref, rank_col_ref, wg_ref, wu_ref, wd_ref,
                   x_ref, mod_ref, o_ref, xg, yacc, gs):
    b, e, f = pl.program_id(0), pl.program_id(1), pl.program_id(2)
    nf = pl.num_programs(2)
    tb = h_ref.shape[0]
    m = MOE_SUB
    n_rows = cnt_ref[b * N_EXPERTS + e]
    nsub = (n_rows + m - 1) // m

    @pl.when((e == 0) & (f == 0))
    def _():
        o_ref[...] = jnp.zeros_like(o_ref)

    @pl.when(f == 0)
    def _():
        rank_e = rank_row_ref[0, pl.ds(e, 1), :]
        gate_e = gate_row_ref[0, pl.ds(e, 1), :]

        def gather(i, carry):
            base = pl.multiple_of(i * m, m)
            want = (lax.broadcasted_iota(jnp.int32, (m, tb), 0) + base).astype(F32)
            p = jnp.where(rank_e == want, 1.0, 0.0)
            xg[pl.ds(base, m), :] = _dot(p.astype(BF16), h_ref[...]).astype(BF16)
            gs[pl.ds(base, m), :] = jnp.broadcast_to(jnp.sum(p * gate_e, axis=1, keepdims=True), (m, LANES))
            return carry

        lax.fori_loop(0, nsub, gather, 0)

    def ffn(i, carry):
        base = pl.multiple_of(i * m, m)
        xr = xg[pl.ds(base, m), :]
        t = (_silu(_dot(xr, wg_ref[0])) * _dot(xr, wu_ref[0])).astype(BF16)
        y = _dot(t, wd_ref[0])

        @pl.when(f == 0)
        def _():
            yacc[pl.ds(base, m), :] = y

        @pl.when(f > 0)
        def _():
            yacc[pl.ds(base, m), :] += y

        return carry

    lax.fori_loop(0, nsub, ffn, 0)

    @pl.when(f == nf - 1)
    def _():
        lane = lax.broadcasted_iota(jnp.int32, (tb, LANES), 1)
        rank_c = jnp.sum(jnp.where(lane == e, rank_col_ref[...], 0.0), axis=1, keepdims=True)

        def scatter(i, carry):
            base = pl.multiple_of(i * m, m)
            want = (lax.broadcasted_iota(jnp.int32, (tb, m), 1) + base).astype(F32)
            pt = jnp.where(rank_c == want, 1.0, 0.0).astype(BF16)
            yg = (yacc[pl.ds(base, m), :] * gs[pl.ds(base, m), 0:1]).astype(BF16)
            o_ref[...] += _dot(pt, yg)
            return carry

        lax.fori_loop(0, nsub, scatter, 0)

    @pl.when((e == N_EXPERTS - 1) & (f == nf - 1))
    def _():
        o_ref[...] = x_ref[...] + mod_ref[0, 5:6, :] * o_ref[...]


def moe_ffn(a, wo, x, mod, mod_of_block, g, w_router, wg, wu, wd):
    r, d = x.shape
    ff = wg.shape[2]
    tb = MOE_BLOCK
    nb = r // tb
    wr = jnp.pad(w_router, ((0, 0), (0, LANES - N_EXPERTS)))
    x, h, rank_row, gate_row, rank_col, cnt = pl.pallas_call(
        _router_kernel,
        grid=(nb,),
        in_specs=[pl.BlockSpec((tb, d), lambda i: (i, 0)),
                  pl.BlockSpec((d, d), lambda i: (0, 0)),
                  pl.BlockSpec((tb, d), lambda i: (i, 0)),
                  pl.BlockSpec((1, 6, d), lambda i: (mod_of_block(i), 0, 0)),
                  pl.BlockSpec((1, d), lambda i: (0, 0)),
                  pl.BlockSpec((d, LANES), lambda i: (0, 0))],
        out_specs=[pl.BlockSpec((tb, d), lambda i: (i, 0)),
                   pl.BlockSpec((tb, d), lambda i: (i, 0)),
                   pl.BlockSpec((1, N_EXPERTS, tb), lambda i: (i, 0, 0)),
                   pl.BlockSpec((1, N_EXPERTS, tb), lambda i: (i, 0, 0)),
                   pl.BlockSpec((tb, LANES), lambda i: (i, 0)),
                   pl.BlockSpec((1, 8, LANES), lambda i: (i, 0, 0))],
        out_shape=[jax.ShapeDtypeStruct((r, d), F32),
                   jax.ShapeDtypeStruct((r, d), BF16),
                   jax.ShapeDtypeStruct((nb, N_EXPERTS, tb), F32),
                   jax.ShapeDtypeStruct((nb, N_EXPERTS, tb), F32),
                   jax.ShapeDtypeStruct((r, LANES), F32),
                   jax.ShapeDtypeStruct((nb, 8, LANES), jnp.int32)],
        compiler_params=_cparams(("parallel",)),
    )(a, wo, x, mod, g.reshape(1, d), wr)
    counts = cnt[:, 0, :N_EXPERTS].reshape(nb * N_EXPERTS)

    nf = MOE_FF_SPLIT
    tf = ff // nf
    sub_rows = -(-tb // MOE_SUB) * MOE_SUB
    grid_spec = pltpu.PrefetchScalarGridSpec(
        num_scalar_prefetch=1,
        grid=(nb, N_EXPERTS, nf),
        in_specs=[pl.BlockSpec((tb, d), lambda i, e, f, c: (i, 0)),
                  pl.BlockSpec((1, N_EXPERTS, tb), lambda i, e, f, c: (i, 0, 0)),
                  pl.BlockSpec((1, N_EXPERTS, tb), lambda i, e, f, c: (i, 0, 0)),
                  pl.BlockSpec((tb, LANES), lambda i, e, f, c: (i, 0)),
                  pl.BlockSpec((1, d, tf), lambda i, e, f, c: (e, 0, f)),
                  pl.BlockSpec((1, d, tf), lambda i, e, f, c: (e, 0, f)),
                  pl.BlockSpec((1, tf, d), lambda i, e, f, c: (e, f, 0)),
                  pl.BlockSpec((tb, d), lambda i, e, f, c: (i, 0)),
                  pl.BlockSpec((1, 6, d), lambda i, e, f, c: (mod_of_block(i), 0, 0))],
        out_specs=pl.BlockSpec((tb, d), lambda i, e, f, c: (i, 0)),
        scratch_shapes=[pltpu.VMEM((sub_rows, d), BF16), pltpu.VMEM((sub_rows, d), F32),
                        pltpu.VMEM((sub_rows, LANES), F32)],
    )
    return pl.pallas_call(
        _expert_kernel,
        grid_spec=grid_spec,
        out_shape=jax.ShapeDtypeStruct((r, d), F32),
        compiler_params=_cparams(("parallel", "arbitrary", "arbitrary")),
    )(counts, h, rank_row, gate_row, rank_col, wg, wu, wd, x, mod)


def _diff_lambda_init(layer):
    return 0.8 - 0.6 * math.exp(-0.3 * layer)


def _head_gain_rows(q_norm, k_norm, n_q, n_k, n_v):
    hn = jnp.concatenate([jnp.tile(q_norm.astype(F32), n_q // HEAD_DIM),
                          jnp.tile(k_norm.astype(F32), n_k // HEAD_DIM),
                          jnp.ones((n_v,), F32)]).reshape(1, -1)
    isn = jnp.concatenate([jnp.ones((n_q + n_k,), F32), jnp.zeros((n_v,), F32)]).reshape(1, -1)
    return hn, isn


def kernel(x_prompt, x_sample, cache_k_0, cache_v_0, cache_k_1, cache_v_1, cache_k_2, cache_v_2, cache_k_3, cache_v_3, c, c_ctx, norm1_0, w_ada_0, b_ada_0, w_qkv_0, q_norm_0, k_norm_0, rpb_0, w_o_0, norm2_0, w_gate_0, w_up_0, w_down_0, norm1_1, w_ada_1, b_ada_1, w_qkv_1, q_norm_1, k_norm_1, lam_q1_1, lam_k1_1, lam_q2_1, lam_k2_1, subln_1, w_o_1, norm2_1, w_router_1, w_egate_1, w_eup_1, w_edown_1, norm1_2, w_ada_2, b_ada_2, w_qkv_2, q_norm_2, k_norm_2, sink_2, w_o_2, norm2_2, w_gate_2, w_up_2, w_down_2, norm1_3, w_ada_3, b_ada_3, w_qkv_3, q_norm_3, k_norm_3, w_o_3, norm2_3, w_router_3, w_egate_3, w_eup_3, w_edown_3):
    nbc, tc, d = x_prompt.shape
    nbl, tl, _ = x_sample.shape
    n_past = cache_k_0.shape[1]
    assert d % (2 * LANES) == 0 and tc % ATTN_Q_TILE == 0 and tl % MOE_BLOCK == 0
    assert (nbc * tc) % MOE_BLOCK == 0 and tl // GRID_W >= NA_KEY_ROWS and nbl <= 7

    norm1 = (norm1_0, norm1_1, norm1_2, norm1_3)
    norm2 = (norm2_0, norm2_1, norm2_2, norm2_3)
    ada_p = ((w_ada_0, b_ada_0), (w_ada_1, b_ada_1), (w_ada_2, b_ada_2), (w_ada_3, b_ada_3))
    w_qkv = (w_qkv_0, w_qkv_1, w_qkv_2, w_qkv_3)
    qk_norm = ((q_norm_0, k_norm_0), (q_norm_1, k_norm_1), (q_norm_2, k_norm_2), (q_norm_3, k_norm_3))
    w_out = (w_o_0, w_o_1, w_o_2, w_o_3)
    ffn_p = ((w_gate_0, w_up_0, w_down_0), (w_router_1, w_egate_1, w_eup_1, w_edown_1),
             (w_gate_2, w_up_2, w_down_2), (w_router_3, w_egate_3, w_eup_3, w_edown_3))
    caches = ((cache_k_0, cache_v_0), (cache_k_1, cache_v_1), (cache_k_2, cache_v_2), (cache_k_3, cache_v_3))

    xc = x_prompt.reshape(nbc * tc, d)
    xl = x_sample.reshape(nbl * tl, d)
    cond = jnp.concatenate([c_ctx[None, :], c, jnp.zeros((7 - nbl, d), F32)], axis=0)
    cond16 = jnp.concatenate([cond, cond], axis=0)

    def ctx_mod(blk_rows):
        return lambda i: 0

    def lat_mod(blk_rows):
        per = tl // blk_rows
        return lambda i: 1 + i // per

    lam_rows = jnp.zeros((8, LANES), F32)
    for r_, v_ in enumerate((lam_q1_1, lam_k1_1, lam_q2_1, lam_k2_1)):
        lam_rows = lam_rows.at[r_, :HEAD_DIM].set(v_.astype(F32))
    bias_pairs = _natten_pair_table(rpb_0)
    rope_tabs = _rope_tables(tl)

    states = []
    for l in range(4):
        mixer = l % 4
        mod = ada_mod(cond16, *ada_p[l])
        wq = w_qkv[l].astype(BF16)
        n = wq.shape[1]
        n_q = d
        n_k = d if mixer < 2 else (n - d) // 2
        n_v = n - n_q - n_k
        tn = 1024 if n % 1024 == 0 else 512
        hn, isn = _head_gain_rows(*qk_norm[l], n_q, n_k, n_v)
        rope = None if mixer == 0 else rope_tabs

        qkv_c = qkv_proj(xc, mod, ctx_mod(ROW_TILE), norm1[l], wq, hn, isn, None, F32, tn, n_v)
        qkv_l = qkv_proj(xl, mod, lat_mod(ROW_TILE), norm1[l], wq, hn, isn, rope, BF16, tn, n_v)
        states.append((qkv_c[:, n_q:n_q + n_k], qkv_c[:, n_q + n_k:]))

        ck, cv = caches[l]
        cache = (ck.astype(BF16).reshape(nbl, n_past, n_k), cv.astype(BF16).reshape(nbl, n_past, n_v))
        common = dict(k_col=n_q, v_col=n_q + n_k, d_model=d)
        ctx = dict(n_sub=1, n_lb=CTX_LANE_BLOCKS, **common)
        lat = dict(n_sub=4, n_lb=1, cache=cache, **common)
        if mixer == 0:
            o_c = attention(qkv_c, nbc, tc, mode="pair", gqa=False, **ctx)
            o_l = attention(qkv_l, nbl, tl, mode="pair", gqa=False, bias_pairs=bias_pairs, **lat)
        elif mixer == 1:
            diff = dict(mode="diff", gqa=False, lam=lam_rows, subln=subln_1.astype(F32).reshape(1, LANES),
                        lam_init=_diff_lambda_init(l))
            o_c = attention(qkv_c, nbc, tc, **diff, **ctx)
            o_l = attention(qkv_l, nbl, tl, **diff, **lat)
        elif mixer == 2:
            sk = sink_2.astype(F32)
            o_c = attention(qkv_c, nbc, tc, mode="pair", gqa=True, sink=sk, **ctx)
            o_l = attention(qkv_l, nbl, tl, mode="pair", gqa=True, band=True, sink=sk, **lat)
        else:
            o_c = attention(qkv_c, nbc, tc, mode="pair", gqa=True, **ctx)
            o_l = attention(qkv_l, nbl, tl, mode="pair", gqa=True, **lat)

        wo = w_out[l].astype(BF16)
        if l % 2 == 0:
            wg, wu, wd = (w.astype(BF16) for w in ffn_p[l])
            xc = dense_ffn(o_c, wo, xc, mod, ctx_mod(ROW_TILE), norm2[l], wg, wu, wd)
            xl = dense_ffn(o_l, wo, xl, mod, lat_mod(ROW_TILE), norm2[l], wg, wu, wd)
        else:
            wr = ffn_p[l][0]
            wg, wu, wd = (w.astype(BF16) for w in ffn_p[l][1:])
            xc = moe_ffn(o_c, wo, xc, mod, ctx_mod(MOE_BLOCK), norm2[l], wr, wg, wu, wd)
            xl = moe_ffn(o_l, wo, xl, mod, lat_mod(MOE_BLOCK), norm2[l], wr, wg, wu, wd)

    nh = d // HEAD_DIM
    k0, v0 = states[0]
    k1, v1 = states[1]
    k2, v2 = states[2]
    k3, v3 = states[3]
    nkv = k2.shape[1] // HEAD_DIM
    return (xc.reshape(nbc, tc, d), xl.reshape(nbl, tl, d),
            k0.reshape(nbc, tc, nh, HEAD_DIM), v0.reshape(nbc, tc, nh, HEAD_DIM),
            k1.reshape(nbc, tc, nh // 2, 2, HEAD_DIM), v1.reshape(nbc, tc, nh // 2, 2 * HEAD_DIM),
            k2.reshape(nbc, tc, nkv, HEAD_DIM), v2.reshape(nbc, tc, nkv, HEAD_DIM),
            k3.reshape(nbc, tc, nkv, HEAD_DIM), v3.reshape(nbc, tc, nkv, HEAD_DIM))
```

```python
import functools
import math

import numpy as np
import jax
import jax.numpy as jnp
from jax import lax
from jax.experimental import pallas as pl
from jax.experimental.pallas import tpu as pltpu

F32 = jnp.float32
BF16 = jnp.bfloat16

HEAD_DIM = 64
GRID_W = 64
NA_ROWS = 8
NA_COLS = 16
WINDOW = 128
ROPE_BASE = 10000.0
ROPE_PAIRS_AXIS = HEAD_DIM // 4
N_EXPERTS = 8
EPS = 1e-6
NEG_INF = -1e30
QK_SCALE = HEAD_DIM ** -0.5
LOG2E = math.log2(math.e)

LANES = 128
V7X_VMEM_BYTES = 64 * 1024 * 1024
VMEM_LIMIT = V7X_VMEM_BYTES - 8 * 1024 * 1024

ROW_TILE = 512
ATTN_Q_TILE = 256
NA_KEY_ROWS = 12
CTX_LANE_BLOCKS = 8
MOE_BLOCK = 1024
MOE_SUB = 256
MOE_FF_SPLIT = 1


def _cparams(sem):
    return pltpu.CompilerParams(dimension_semantics=sem, vmem_limit_bytes=VMEM_LIMIT)


def _silu(x):
    return x * (1.0 / (1.0 + jnp.exp(-x)))


def _norm_mod(x, g, shift, scale):
    ms = jnp.mean(x * x, axis=-1, keepdims=True)
    y = x * lax.rsqrt(ms + EPS) * g
    return y * (1.0 + scale) + shift


def _dot(a, b):
    return jnp.dot(a, b, preferred_element_type=F32)


def _dot_nt(a, b):
    return lax.dot_general(a, b, (((1,), (1,)), ((), ())), preferred_element_type=F32)


def _ada_kernel(c_ref, w_ref, b_ref, o_ref):
    a = _silu(c_ref[...])
    a_hi = a.astype(BF16)
    a_lo = a - a_hi.astype(F32)
    row = lax.broadcasted_iota(jnp.int32, a.shape, 0)
    lhs = jnp.where(row < 8, a_hi.astype(F32), a_lo).astype(BF16)
    w = w_ref[...]
    w_hi = w.astype(BF16)
    w_lo = (w - w_hi.astype(F32)).astype(BF16)
    r = _dot(lhs, w_hi)
    r2 = _dot(a_hi, w_lo)
    o_ref[...] = r[:8] + r[8:] + r2[:8] + b_ref[...]


def ada_mod(cond16, w_ada, b_ada):
    d, n = w_ada.shape
    tn = 1024
    out = pl.pallas_call(
        _ada_kernel,
        grid=(n // tn,),
        in_specs=[pl.BlockSpec((16, d), lambda j: (0, 0)),
                  pl.BlockSpec((d, tn), lambda j: (0, j)),
                  pl.BlockSpec((1, tn), lambda j: (0, j))],
        out_specs=pl.BlockSpec((8, tn), lambda j: (0, j)),
        out_shape=jax.ShapeDtypeStruct((8, n), F32),
        compiler_params=_cparams(("arbitrary",)),
    )(cond16, w_ada, b_ada.reshape(1, n))
    return out.reshape(8, 6, d)


def _qkv_kernel(*refs, rope, n_norm):
    if rope:
        x_ref, mod_ref, g_ref, w_ref, seg_ref, hn_ref, cos_ref, sin_ref, o_ref = refs
    else:
        x_ref, mod_ref, g_ref, w_ref, seg_ref, hn_ref, o_ref = refs
    h = _norm_mod(x_ref[...], g_ref[...], mod_ref[0, 0:1, :], mod_ref[0, 1:2, :]).astype(BF16)
    acc = _dot(h, w_ref[...])
    qk = acc[:, :n_norm]
    sq = (qk * qk).astype(BF16)
    seg = seg_ref[...]
    ssum = jnp.concatenate(
        [_dot(sq[:, c * 2 * LANES:(c + 1) * 2 * LANES], seg) for c in range(n_norm // (2 * LANES))], axis=1)
    y = qk * lax.rsqrt(ssum * (1.0 / HEAD_DIM) + EPS) * hn_ref[...]
    if rope:
        lane = lax.broadcasted_iota(jnp.int32, (acc.shape[0], LANES), 1)
        even = (lane % 2) == 0
        cos, sin = cos_ref[...], sin_ref[...]
        parts = []
        for c in range(n_norm // LANES):
            yc = y[:, c * LANES:(c + 1) * LANES]
            nxt = pltpu.roll(yc, LANES - 1, 1)
            prv = pltpu.roll(yc, 1, 1)
            parts.append(yc * cos + jnp.where(even, nxt, prv) * sin)
        y = jnp.concatenate(parts, axis=1)
    o_ref[:, :n_norm] = y.astype(o_ref.dtype)
    o_ref[:, n_norm:] = acc[:, n_norm:].astype(o_ref.dtype)


def _seg_matrix():
    i = np.arange(2 * LANES)
    return jnp.asarray((i[:, None] // HEAD_DIM) == (i[None, :] // HEAD_DIM), BF16)


def qkv_proj(x, mod, mod_of_block, g, w, hn, rope_tabs, out_dtype):
    r, d = x.shape
    n = w.shape[1]
    n_norm = hn.shape[1]
    tm = ROW_TILE
    rope = rope_tabs is not None
    in_specs = [pl.BlockSpec((tm, d), lambda i: (i, 0)),
                pl.BlockSpec((1, 6, d), lambda i: (mod_of_block(i), 0, 0)),
                pl.BlockSpec((1, d), lambda i: (0, 0)),
                pl.BlockSpec((d, n), lambda i: (0, 0)),
                pl.BlockSpec((2 * LANES, 2 * LANES), lambda i: (0, 0)),
                pl.BlockSpec((1, n_norm), lambda i: (0, 0))]
    args = [x, mod, g.reshape(1, d), w, _seg_matrix(), hn]
    if rope:
        cos_t, sin_t = rope_tabs
        nblk = cos_t.shape[0] // tm
        in_specs += [pl.BlockSpec((tm, LANES), lambda i: (i % nblk, 0)),
                     pl.BlockSpec((tm, LANES), lambda i: (i % nblk, 0))]
        args += [cos_t, sin_t]
    return pl.pallas_call(
        functools.partial(_qkv_kernel, rope=rope, n_norm=n_norm),
        grid=(r // tm,),
        in_specs=in_specs,
        out_specs=pl.BlockSpec((tm, n), lambda i: (i, 0)),
        out_shape=jax.ShapeDtypeStruct((r, n), out_dtype),
        compiler_params=_cparams(("parallel",)),
    )(*args)


def _rope_tables(t):
    pos = np.arange(t)
    row = (pos // GRID_W).astype(np.float32)
    col = (pos % GRID_W).astype(np.float32)
    freqs = jnp.asarray(ROPE_BASE, F32) ** (-jnp.arange(ROPE_PAIRS_AXIS, dtype=F32) / ROPE_PAIRS_AXIS)
    ang = jnp.concatenate([jnp.asarray(row)[:, None] * freqs, jnp.asarray(col)[:, None] * freqs], axis=-1)
    cos = jnp.repeat(jnp.cos(ang), 2, axis=-1)
    sin = jnp.repeat(jnp.sin(ang), 2, axis=-1)
    sign = jnp.asarray(np.where(np.arange(HEAD_DIM) % 2 == 0, -1.0, 1.0), F32)
    reps = LANES // HEAD_DIM
    return jnp.tile(cos, (1, reps)), jnp.tile(sin * sign, (1, reps))


def _in_waves(gens):
    while gens:
        gens = [g for g in gens if next(g, StopIteration) is not StopIteration]
        yield


def _attn_kernel(*refs, n_lb, kmap, mode, n_cache, natten, has_sink, **static):
    it = iter(refs)
    sink_ref = next(it) if has_sink else None
    q_ref, k_ref, v_ref = next(it), next(it), next(it)
    ck_ref, cv_ref = (next(it), next(it)) if n_cache else (None, None)
    bp_ref = next(it) if natten else None
    lam_refs = [next(it), next(it)] if mode == "diff" else []
    o_ref, kf = next(it), next(it)
    v_scr = list(it)
    gens = []
    for jl in range(n_lb):
        q_cols = pl.ds(jl * LANES, LANES)
        k_cols = pl.ds(kmap(jl) * LANES, LANES)
        views = [sink_ref] if has_sink else []
        views += [q_ref.at[:, q_cols], k_ref.at[:, k_cols], v_ref.at[:, k_cols]]
        if n_cache:
            views += [ck_ref.at[:, :, k_cols], cv_ref.at[:, :, k_cols]]
        if natten:
            views.append(bp_ref.at[pl.ds(2 * jl, 2)])
        views += lam_refs + [o_ref.at[:, q_cols], kf.at[jl]] + [v.at[jl] for v in v_scr]
        gens.append(_attn_lane_block(views, pl.program_id(1) * n_lb + jl, pl.program_id(2), mode=mode,
                                     n_cache=n_cache, natten=natten, has_sink=has_sink, **static))
    for _ in _in_waves(gens):
        pass


def _attn_lane_block(refs, j, qb, *, mode, gqa, t_loc, n_cache, win, band, natten, has_sink, lam_init, n_sub):
    it = iter(refs)
    sink_ref = next(it) if has_sink else None
    q_ref, k_ref, v_ref = next(it), next(it), next(it)
    ck_ref = cv_ref = bp_ref = lam_ref = subln_ref = None
    if n_cache:
        ck_ref, cv_ref = next(it), next(it)
    if natten:
        bp_ref = next(it)
    if mode == "diff":
        lam_ref, subln_ref = next(it), next(it)
    o_ref, kf = next(it), next(it)
    v_scr = [next(it), next(it)] if mode == "pair" else [next(it)]

    tq = ATTN_Q_TILE
    nk = t_loc + n_cache
    rows = t_loc // GRID_W
    qrows = tq // GRID_W

    def place(a):
        if not gqa:
            return a
        half = (j // 2) % 2
        lane_half = (lax.broadcasted_iota(jnp.int32, a.shape, 1) >= HEAD_DIM).astype(jnp.int32)
        return jnp.where(lane_half == half, a, pltpu.roll(a, HEAD_DIM, 1))

    def put_v(lo, hi, a):
        if mode == "pair":
            ln = lax.broadcasted_iota(jnp.int32, a.shape, 1)
            v_scr[0][lo:hi, :] = jnp.where(ln < HEAD_DIM, a, 1.0).astype(BF16)
            v_scr[1][lo:hi, :] = jnp.where(ln >= HEAD_DIM, a, 1.0).astype(BF16)
        else:
            v_scr[0][lo:hi, :] = a.astype(BF16)

    @pl.when(qb == 0)
    def _():
        kf[0:t_loc, :] = place(k_ref[...].astype(F32)).astype(BF16)
        put_v(0, t_loc, place(v_ref[...].astype(F32)))
        if n_cache:
            kf[t_loc:nk, :] = place(ck_ref[0].astype(F32)).astype(BF16)
            put_v(t_loc, nk, place(cv_ref[0].astype(F32)))

    lane = lax.broadcasted_iota(jnp.int32, (tq, LANES), 1)
    if mode == "diff":
        lp = lam_ref[...]
        lam = (jnp.exp(jnp.sum(lp[0:1] * lp[1:2], axis=1, keepdims=True))
               - jnp.exp(jnp.sum(lp[2:3] * lp[3:4], axis=1, keepdims=True)) + lam_init)

    def chain(g, hh, q2, sl, k_loc, keep, ws_row, done):
        in_half = (lane < HEAD_DIM) if hh == 0 else (lane >= HEAD_DIM)
        qh = jnp.where(in_half, q2, 0.0).astype(BF16)
        s = _dot_nt(qh, k_loc)
        if n_cache:
            s_c = _dot_nt(qh, kf[t_loc:nk, :])
        yield
        if natten:
            blocks = []
            for a in range(qrows):
                first = ws_row - (g * qrows + a) + NA_ROWS
                blocks.append(jnp.concatenate(
                    [bp_ref[hh, pl.ds(jnp.clip(first + 2 * p, 0, 2 * NA_ROWS - 1), 1)][0]
                     for p in range(win // LANES)], axis=1))
            s = s + jnp.concatenate(blocks, axis=0)
        if keep is not None:
            s = jnp.where(keep, s, NEG_INF)
        m = jnp.max(s, axis=1, keepdims=True)
        if n_cache:
            m = jnp.maximum(m, jnp.max(s_c, axis=1, keepdims=True))
        if has_sink:
            sink = sink_ref[2 * j + hh] * LOG2E
            m = jnp.maximum(m, sink)
        e = jnp.exp2(s - m)
        e_c = jnp.exp2(s_c - m) if n_cache else None
        yield
        if mode == "pair":
            o = _dot(e.astype(BF16), v_scr[hh][sl, :])
            if n_cache:
                o = o + _dot(e_c.astype(BF16), v_scr[hh][t_loc:nk, :])
            yield
            den = pltpu.roll(o, HEAD_DIM, 1)
            if has_sink:
                den = den + jnp.exp2(sink - m)
            done[hh] = o * (1.0 / den)
        else:
            den = jnp.sum(e, axis=1, keepdims=True)
            if n_cache:
                den = den + jnp.sum(e_c, axis=1, keepdims=True)
            done[hh] = (e, e_c, 1.0 / den)

    def sub_tile(t):
        g = qb * n_sub + t
        q2 = q_ref[t * tq:(t + 1) * tq, :].astype(F32) * (QK_SCALE * LOG2E)
        keep = ws_row = None
        if natten:
            ws_row = jnp.clip(g * qrows - NA_ROWS // 2, 0, rows - NA_KEY_ROWS)
            sl = pl.ds(pl.multiple_of(ws_row * GRID_W, GRID_W), win)
            q_row = lax.broadcasted_iota(jnp.int32, (tq, win), 0) // GRID_W + g * qrows
            r0 = jnp.clip(q_row - NA_ROWS // 2, 0, rows - NA_ROWS)
            k_row = lax.broadcasted_iota(jnp.int32, (tq, win), 1) // GRID_W + ws_row
            keep = (k_row >= r0) & (k_row < r0 + NA_ROWS)
        elif band:
            ws = pl.multiple_of(jnp.clip(g * tq - WINDOW, 0, t_loc - win), WINDOW)
            sl = pl.ds(ws, win)
            qpos = g * tq + lax.broadcasted_iota(jnp.int32, (tq, win), 0)
            kpos = ws + lax.broadcasted_iota(jnp.int32, (tq, win), 1)
            keep = jnp.abs(qpos - kpos) <= WINDOW
        else:
            sl = slice(0, win)
        k_loc = kf[sl, :]
        done = [None, None]
        yield from _in_waves([chain(g, hh, q2, sl, k_loc, keep, ws_row, done) for hh in range(2)])
        if mode == "pair":
            out = jnp.where(lane < HEAD_DIM, done[0], done[1])
        else:
            (e0, ec0, w0), (e1, ec1, w1) = done
            w1 = w1 * lam
            out = _dot((e0 * w0 - e1 * w1).astype(BF16), v_scr[0][sl, :])
            if n_cache:
                out = out + _dot((ec0 * w0 - ec1 * w1).astype(BF16), v_scr[0][t_loc:nk, :])
            yield
            ms = jnp.mean(out * out, axis=-1, keepdims=True)
            out = out * lax.rsqrt(ms + EPS) * subln_ref[...] * (1.0 - lam_init)
        o_ref[t * tq:(t + 1) * tq, :] = out.astype(o_ref.dtype)

    yield from _in_waves([sub_tile(t) for t in range(n_sub)])


def attention(qkv, nb, t_loc, *, mode, gqa, k_col, v_col, n_sub, n_lb, cache=None, band=False, bias_pairs=None,
              sink=None, lam=None, subln=None, lam_init=0.0, d_model):
    tq = ATTN_Q_TILE * n_sub
    n_qlb = d_model // LANES
    natten = bias_pairs is not None
    n_cache = 0 if cache is None else cache[0].shape[1]
    if natten:
        win = NA_KEY_ROWS * GRID_W
    elif band:
        win = ATTN_Q_TILE + 2 * WINDOW
    else:
        win = t_loc
    if gqa:
        assert n_lb == 1 or n_lb % 4 == 0
        k_lb = max(n_lb // 4, 1)
        kgrp = (lambda jg: jg // 4) if n_lb == 1 else (lambda jg: jg)
        kmap = lambda jl: jl // 4
    else:
        k_lb = n_lb
        kgrp = lambda jg: jg
        kmap = lambda jl: jl
    qw, kw = n_lb * LANES, k_lb * LANES
    assert k_col % kw == 0 and v_col % kw == 0 and n_qlb % n_lb == 0
    kc, vc = k_col // kw, v_col // kw
    nqb = t_loc // tq
    nk = t_loc + n_cache

    in_specs = []
    args = []
    if sink is not None:
        in_specs.append(pl.BlockSpec(memory_space=pltpu.SMEM))
        args.append(sink)
    in_specs += [pl.BlockSpec((tq, qw), lambda b, j, i: (b * nqb + i, j)),
                 pl.BlockSpec((t_loc, kw), lambda b, j, i: (b, kc + kgrp(j))),
                 pl.BlockSpec((t_loc, kw), lambda b, j, i: (b, vc + kgrp(j)))]
    args += [qkv, qkv, qkv]
    if cache is not None:
        in_specs += [pl.BlockSpec((1, n_cache, kw), lambda b, j, i: (b, 0, kgrp(j))),
                     pl.BlockSpec((1, n_cache, kw), lambda b, j, i: (b, 0, kgrp(j)))]
        args += list(cache)
    if natten:
        in_specs.append(pl.BlockSpec((2 * n_lb, 2 * NA_ROWS, GRID_W, LANES), lambda b, j, i: (j, 0, 0, 0)))
        args.append(bias_pairs)
    if mode == "diff":
        in_specs += [pl.BlockSpec((8, LANES), lambda b, j, i: (0, 0)),
                     pl.BlockSpec((1, LANES), lambda b, j, i: (0, 0))]
        args += [lam, subln]
    kern = functools.partial(
        _attn_kernel, n_lb=n_lb, kmap=kmap, mode=mode, gqa=gqa, t_loc=t_loc, n_cache=n_cache, win=win, band=band,
        natten=natten, has_sink=sink is not None, lam_init=lam_init, n_sub=n_sub)
    n_v = 2 if mode == "pair" else 1
    return pl.pallas_call(
        kern,
        grid=(nb, n_qlb // n_lb, nqb),
        in_specs=in_specs,
        out_specs=pl.BlockSpec((tq, qw), lambda b, j, i: (b * nqb + i, j)),
        out_shape=jax.ShapeDtypeStruct((nb * t_loc, d_model), BF16),
        scratch_shapes=[pltpu.VMEM((n_lb, nk, LANES), BF16) for _ in range(1 + n_v)],
        compiler_params=_cparams(("parallel", "arbitrary", "arbitrary")),
    )(*args)


def _natten_pair_table(rpb):
    h, n_dr, n_dc = rpb.shape
    pad = GRID_W - NA_COLS
    ext = jnp.concatenate([jnp.repeat(rpb[..., :1], pad, axis=-1), rpb,
                           jnp.repeat(rpb[..., -1:], pad + 1, axis=-1)], axis=-1).astype(F32) * LOG2E
    neg = jnp.full((h, 1, LANES), NEG_INF, F32)
    ext = jnp.concatenate([neg, ext, neg], axis=1)
    return pl.pallas_call(
        _pair_table_kernel,
        grid=(h,),
        in_specs=[pl.BlockSpec((1, n_dr + 2, LANES), lambda i: (i, 0, 0))],
        out_specs=pl.BlockSpec((1, n_dr + 1, GRID_W, LANES), lambda i: (i, 0, 0, 0)),
        out_shape=jax.ShapeDtypeStruct((h, n_dr + 1, GRID_W, LANES), F32),
        compiler_params=_cparams(("parallel",)),
    )(ext)


def _pair_table_kernel(ext_ref, o_ref):
    qc = lax.broadcasted_iota(jnp.int32, (GRID_W, LANES), 0)
    ln = lax.broadcasted_iota(jnp.int32, (GRID_W, LANES), 1)
    kc = ln % GRID_W
    cstart = jnp.clip(qc - NA_COLS // 2, 0, GRID_W - NA_COLS)
    in_window = (kc >= cstart) & (kc < cstart + NA_COLS)
    for d in range(o_ref.shape[1]):
        lo = jnp.broadcast_to(ext_ref[0, d:d + 1, :], (GRID_W, LANES))
        hi = jnp.broadcast_to(ext_ref[0, d + 1:d + 2, :], (GRID_W, LANES))
        lo = pltpu.roll(lo, GRID_W + 1, 1, stride=1, stride_axis=0)
        hi = pltpu.roll(hi, 1, 1, stride=1, stride_axis=0)
        o_ref[0, d] = jnp.where(in_window, jnp.where(ln < GRID_W, lo, hi), NEG_INF)


def _mixer_residual(a_ref, wo_ref, x_ref, mod_ref):
    return x_ref[...] + mod_ref[0, 2:3, :] * _dot(a_ref[...], wo_ref[...])


def _ffn_kernel(a_ref, wo_ref, x_ref, mod_ref, g_ref, wg_ref, wu_ref, wd_ref, o_ref, h_scr, acc_scr, x1_scr):
    f = pl.program_id(1)

    @pl.when(f == 0)
    def _():
        x1 = _mixer_residual(a_ref, wo_ref, x_ref, mod_ref)
        x1_scr[...] = x1
        h = _norm_mod(x1, g_ref[...], mod_ref[0, 3:4, :], mod_ref[0, 4:5, :])
        h_scr[...] = h.astype(BF16)

    h = h_scr[...]
    t = (_silu(_dot(h, wg_ref[...])) * _dot(h, wu_ref[...])).astype(BF16)
    y = _dot(t, wd_ref[...])

    @pl.when(f == 0)
    def _():
        acc_scr[...] = y

    @pl.when(f > 0)
    def _():
        acc_scr[...] += y

    @pl.when(f == pl.num_programs(1) - 1)
    def _():
        o_ref[...] = x1_scr[...] + mod_ref[0, 5:6, :] * acc_scr[...]


def dense_ffn(a, wo, x, mod, mod_of_block, g, wg, wu, wd):
    r, d = x.shape
    ff = wg.shape[1]
    tm = ROW_TILE
    nf = 2
    tf = ff // nf
    return pl.pallas_call(
        _ffn_kernel,
        grid=(r // tm, nf),
        in_specs=[pl.BlockSpec((tm, d), lambda i, f: (i, 0)),
                  pl.BlockSpec((d, d), lambda i, f: (0, 0)),
                  pl.BlockSpec((tm, d), lambda i, f: (i, 0)),
                  pl.BlockSpec((1, 6, d), lambda i, f: (mod_of_block(i), 0, 0)),
                  pl.BlockSpec((1, d), lambda i, f: (0, 0)),
                  pl.BlockSpec((d, tf), lambda i, f: (0, f)),
                  pl.BlockSpec((d, tf), lambda i, f: (0, f)),
                  pl.BlockSpec((tf, d), lambda i, f: (f, 0))],
        out_specs=pl.BlockSpec((tm, d), lambda i, f: (i, 0)),
        out_shape=jax.ShapeDtypeStruct((r, d), F32),
        scratch_shapes=[pltpu.VMEM((tm, d), BF16), pltpu.VMEM((tm, d), F32), pltpu.VMEM((tm, d), F32)],
        compiler_params=_cparams(("parallel", "arbitrary")),
    )(a, wo, x, mod, g.reshape(1, d), wg, wu, wd)


def _router_kernel(a_ref, wo_ref, x_ref, mod_ref, g_ref, wr_ref,
                   x1_ref, h_ref, rank_row_ref, gate_row_ref, rank_col_ref, cnt_ref):
    x1 = _mixer_residual(a_ref, wo_ref, x_ref, mod_ref)
    x1_ref[...] = x1
    h = _norm_mod(x1, g_ref[...], mod_ref[0, 3:4, :], mod_ref[0, 4:5, :])
    h_hi = h.astype(BF16)
    h_ref[...] = h_hi
    h_lo = (h - h_hi.astype(F32)).astype(BF16)
    wr = wr_ref[...]
    w_hi = wr.astype(BF16)
    w_lo = (wr - w_hi.astype(F32)).astype(BF16)
    logits = _dot(h_hi, w_hi) + (_dot(h_lo, w_hi) + _dot(h_hi, w_lo))
    tb = logits.shape[0]
    lane = lax.broadcasted_iota(jnp.int32, logits.shape, 1)
    lane_f = lane.astype(F32)
    logits = jnp.where(lane < N_EXPERTS, logits, -jnp.inf)
    m1 = jnp.max(logits, axis=1, keepdims=True)
    i1 = jnp.min(jnp.where(logits == m1, lane_f, float(LANES)), axis=1, keepdims=True)
    rest = jnp.where(lane_f == i1, -jnp.inf, logits)
    m2 = jnp.max(rest, axis=1, keepdims=True)
    i2 = jnp.min(jnp.where(rest == m2, lane_f, float(LANES)), axis=1, keepdims=True)
    e2 = jnp.exp(m2 - m1)
    inv = 1.0 / (1.0 + e2)
    is1 = lane_f == i1
    is2 = lane_f == i2
    gates = jnp.where(is1, inv, 0.0) + jnp.where(is2, e2 * inv, 0.0)
    sel = jnp.where(is1 | is2, 1.0, 0.0)
    sel_b = sel.astype(BF16)
    chunk = 256
    parts = []
    for c in range(tb // chunk):
        ri = lax.broadcasted_iota(jnp.int32, (chunk, tb), 0) + c * chunk
        ci = lax.broadcasted_iota(jnp.int32, (chunk, tb), 1)
        parts.append(_dot(jnp.where(ci < ri, 1.0, 0.0).astype(BF16), sel_b))
    rank = jnp.concatenate(parts, axis=0)
    rank_sel = jnp.where(sel > 0.5, rank, -1.0)
    rank_col_ref[...] = rank_sel
    rank_row_ref[0] = rank_sel.T[:N_EXPERTS]
    gate_row_ref[0] = gates.T[:N_EXPERTS]
    cnt_ref[0] = jnp.broadcast_to(jnp.sum(sel, axis=0, keepdims=True), (8, LANES)).astype(jnp.int32)


def _expert_kernel(cnt_ref, h_ref, rank_row_ref, gate_row_ref, rank_col_ref, wg_ref, wu_ref, wd_ref,
                   x_ref, mod_ref, o_ref, xg, yacc, gs):
    b, e, f = pl.program_id(0), pl.program_id(1), pl.program_id(2)
    nf = pl.num_programs(2)
    tb = h_ref.shape[0]
    m = MOE_SUB
    n_rows = cnt_ref[b * N_EXPERTS + e]
    nsub = (n_rows + m - 1) // m

    @pl.when((e == 0) & (f == 0))
    def _():
        o_ref[...] = jnp.zeros_like(o_ref)

    @pl.when(f == 0)
    def _():
        rank_e = rank_row_ref[0, pl.ds(e, 1), :]
        gate_e = gate_row_ref[0, pl.ds(e, 1), :]

        def gather(i, carry):
            base = pl.multiple_of(i * m, m)
            want = (lax.broadcasted_iota(jnp.int32, (m, tb), 0) + base).astype(F32)
            p = jnp.where(rank_e == want, 1.0, 0.0)
            xg[pl.ds(base, m), :] = _dot(p.astype(BF16), h_ref[...]).astype(BF16)
            gs[pl.ds(base, m), :] = jnp.broadcast_to(jnp.sum(p * gate_e, axis=1, keepdims=True), (m, LANES))
            return carry

        lax.fori_loop(0, nsub, gather, 0)

    def ffn(i, carry):
        base = pl.multiple_of(i * m, m)
        xr = xg[pl.ds(base, m), :]
        t = (_silu(_dot(xr, wg_ref[0])) * _dot(xr, wu_ref[0])).astype(BF16)
        y = _dot(t, wd_ref[0])

        @pl.when(f == 0)
        def _():
            yacc[pl.ds(base, m), :] = y

        @pl.when(f > 0)
        def _():
            yacc[pl.ds(base, m), :] += y

        return carry

    lax.fori_loop(0, nsub, ffn, 0)

    @pl.when(f == nf - 1)
    def _():
        lane = lax.broadcasted_iota(jnp.int32, (tb, LANES), 1)
        rank_c = jnp.sum(jnp.where(lane == e, rank_col_ref[...], 0.0), axis=1, keepdims=True)

        def scatter(i, carry):
            base = pl.multiple_of(i * m, m)
            want = (lax.broadcasted_iota(jnp.int32, (tb, m), 1) + base).astype(F32)
            pt = jnp.where(rank_c == want, 1.0, 0.0).astype(BF16)
            yg = (yacc[pl.ds(base, m), :] * gs[pl.ds(base, m), 0:1]).astype(BF16)
            o_ref[...] += _dot(pt, yg)
            return carry

        lax.fori_loop(0, nsub, scatter, 0)

    @pl.when((e == N_EXPERTS - 1) & (f == nf - 1))
    def _():
        o_ref[...] = x_ref[...] + mod_ref[0, 5:6, :] * o_ref[...]


def moe_ffn(a, wo, x, mod, mod_of_block, g, w_router, wg, wu, wd):
    r, d = x.shape
    ff = wg.shape[2]
    tb = MOE_BLOCK
    nb = r // tb
    wr = jnp.pad(w_router, ((0, 0), (0, LANES - N_EXPERTS)))
    x, h, rank_row, gate_row, rank_col, cnt = pl.pallas_call(
        _router_kernel,
        grid=(nb,),
        in_specs=[pl.BlockSpec((tb, d), lambda i: (i, 0)),
                  pl.BlockSpec((d, d), lambda i: (0, 0)),
                  pl.BlockSpec((tb, d), lambda i: (i, 0)),
                  pl.BlockSpec((1, 6, d), lambda i: (mod_of_block(i), 0, 0)),
                  pl.BlockSpec((1, d), lambda i: (0, 0)),
                  pl.BlockSpec((d, LANES), lambda i: (0, 0))],
        out_specs=[pl.BlockSpec((tb, d), lambda i: (i, 0)),
                   pl.BlockSpec((tb, d), lambda i: (i, 0)),
                   pl.BlockSpec((1, N_EXPERTS, tb), lambda i: (i, 0, 0)),
                   pl.BlockSpec((1, N_EXPERTS, tb), lambda i: (i, 0, 0)),
                   pl.BlockSpec((tb, LANES), lambda i: (i, 0)),
                   pl.BlockSpec((1, 8, LANES), lambda i: (i, 0, 0))],
        out_shape=[jax.ShapeDtypeStruct((r, d), F32),
                   jax.ShapeDtypeStruct((r, d), BF16),
                   jax.ShapeDtypeStruct((nb, N_EXPERTS, tb), F32),
                   jax.ShapeDtypeStruct((nb, N_EXPERTS, tb), F32),
                   jax.ShapeDtypeStruct((r, LANES), F32),
                   jax.ShapeDtypeStruct((nb, 8, LANES), jnp.int32)],
        compiler_params=_cparams(("parallel",)),
    )(a, wo, x, mod, g.reshape(1, d), wr)
    counts = cnt[:, 0, :N_EXPERTS].reshape(nb * N_EXPERTS)

    nf = MOE_FF_SPLIT
    tf = ff // nf
    sub_rows = -(-tb // MOE_SUB) * MOE_SUB
    grid_spec = pltpu.PrefetchScalarGridSpec(
        num_scalar_prefetch=1,
        grid=(nb, N_EXPERTS, nf),
        in_specs=[pl.BlockSpec((tb, d), lambda i, e, f, c: (i, 0)),
                  pl.BlockSpec((1, N_EXPERTS, tb), lambda i, e, f, c: (i, 0, 0)),
                  pl.BlockSpec((1, N_EXPERTS, tb), lambda i, e, f, c: (i, 0, 0)),
                  pl.BlockSpec((tb, LANES), lambda i, e, f, c: (i, 0)),
                  pl.BlockSpec((1, d, tf), lambda i, e, f, c: (e, 0, f)),
                  pl.BlockSpec((1, d, tf), lambda i, e, f, c: (e, 0, f)),
                  pl.BlockSpec((1, tf, d), lambda i, e, f, c: (e, f, 0)),
                  pl.BlockSpec((tb, d), lambda i, e, f, c: (i, 0)),
                  pl.BlockSpec((1, 6, d), lambda i, e, f, c: (mod_of_block(i), 0, 0))],
        out_specs=pl.BlockSpec((tb, d), lambda i, e, f, c: (i, 0)),
        scratch_shapes=[pltpu.VMEM((sub_rows, d), BF16), pltpu.VMEM((sub_rows, d), F32),
                        pltpu.VMEM((sub_rows, LANES), F32)],
    )
    return pl.pallas_call(
        _expert_kernel,
        grid_spec=grid_spec,
        out_shape=jax.ShapeDtypeStruct((r, d), F32),
        compiler_params=_cparams(("parallel", "arbitrary", "arbitrary")),
    )(counts, h, rank_row, gate_row, rank_col, wg, wu, wd, x, mod)


def _diff_lambda_init(layer):
    return 0.8 - 0.6 * math.exp(-0.3 * layer)


def _head_gain_row(q_norm, k_norm, n_q, n_k):
    return jnp.concatenate([jnp.tile(q_norm.astype(F32), n_q // HEAD_DIM),
                            jnp.tile(k_norm.astype(F32), n_k // HEAD_DIM)]).reshape(1, -1)


def kernel(x_prompt, x_sample, cache_k_0, cache_v_0, cache_k_1, cache_v_1, cache_k_2, cache_v_2, cache_k_3, cache_v_3, c, c_ctx, norm1_0, w_ada_0, b_ada_0, w_qkv_0, q_norm_0, k_norm_0, rpb_0, w_o_0, norm2_0, w_gate_0, w_up_0, w_down_0, norm1_1, w_ada_1, b_ada_1, w_qkv_1, q_norm_1, k_norm_1, lam_q1_1, lam_k1_1, lam_q2_1, lam_k2_1, subln_1, w_o_1, norm2_1, w_router_1, w_egate_1, w_eup_1, w_edown_1, norm1_2, w_ada_2, b_ada_2, w_qkv_2, q_norm_2, k_norm_2, sink_2, w_o_2, norm2_2, w_gate_2, w_up_2, w_down_2, norm1_3, w_ada_3, b_ada_3, w_qkv_3, q_norm_3, k_norm_3, w_o_3, norm2_3, w_router_3, w_egate_3, w_eup_3, w_edown_3):
    nbc, tc, d = x_prompt.shape
    nbl, tl, _ = x_sample.shape
    n_past = cache_k_0.shape[1]
    assert d % (2 * LANES) == 0 and tc % ATTN_Q_TILE == 0 and tl % MOE_BLOCK == 0
    assert (nbc * tc) % MOE_BLOCK == 0 and tl // GRID_W >= NA_KEY_ROWS and nbl <= 7

    norm1 = (norm1_0, norm1_1, norm1_2, norm1_3)
    norm2 = (norm2_0, norm2_1, norm2_2, norm2_3)
    ada_p = ((w_ada_0, b_ada_0), (w_ada_1, b_ada_1), (w_ada_2, b_ada_2), (w_ada_3, b_ada_3))
    w_qkv = (w_qkv_0, w_qkv_1, w_qkv_2, w_qkv_3)
    qk_norm = ((q_norm_0, k_norm_0), (q_norm_1, k_norm_1), (q_norm_2, k_norm_2), (q_norm_3, k_norm_3))
    w_out = (w_o_0, w_o_1, w_o_2, w_o_3)
    ffn_p = ((w_gate_0, w_up_0, w_down_0), (w_router_1, w_egate_1, w_eup_1, w_edown_1),
             (w_gate_2, w_up_2, w_down_2), (w_router_3, w_egate_3, w_eup_3, w_edown_3))
    caches = ((cache_k_0, cache_v_0), (cache_k_1, cache_v_1), (cache_k_2, cache_v_2), (cache_k_3, cache_v_3))

    xc = x_prompt.reshape(nbc * tc, d)
    xl = x_sample.reshape(nbl * tl, d)
    cond = jnp.concatenate([c_ctx[None, :], c, jnp.zeros((7 - nbl, d), F32)], axis=0)
    cond16 = jnp.concatenate([cond, cond], axis=0)

    def ctx_mod(blk_rows):
        return lambda i: 0

    def lat_mod(blk_rows):
        per = tl // blk_rows
        return lambda i: 1 + i // per

    lam_rows = jnp.zeros((8, LANES), F32)
    for r_, v_ in enumerate((lam_q1_1, lam_k1_1, lam_q2_1, lam_k2_1)):
        lam_rows = lam_rows.at[r_, :HEAD_DIM].set(v_.astype(F32))
    bias_pairs = _natten_pair_table(rpb_0)
    rope_tabs = _rope_tables(tl)

    states = []
    for l in range(4):
        mixer = l % 4
        mod = ada_mod(cond16, *ada_p[l])
        wq = w_qkv[l].astype(BF16)
        n = wq.shape[1]
        n_q = d
        n_k = d if mixer < 2 else (n - d) // 2
        n_v = n - n_q - n_k
        hn = _head_gain_row(*qk_norm[l], n_q, n_k)
        rope = None if mixer == 0 else rope_tabs

        qkv_c = qkv_proj(xc, mod, ctx_mod(ROW_TILE), norm1[l], wq, hn, None, F32)
        qkv_l = qkv_proj(xl, mod, lat_mod(ROW_TILE), norm1[l], wq, hn, rope, BF16)
        states.append((qkv_c[:, n_q:n_q + n_k], qkv_c[:, n_q + n_k:]))

        ck, cv = caches[l]
        cache = (ck.astype(BF16).reshape(nbl, n_past, n_k), cv.astype(BF16).reshape(nbl, n_past, n_v))
        common = dict(k_col=n_q, v_col=n_q + n_k, d_model=d)
        ctx = dict(n_sub=1, n_lb=CTX_LANE_BLOCKS, **common)
        lat = dict(n_sub=4, n_lb=1, cache=cache, **common)
        if mixer == 0:
            o_c = attention(qkv_c, nbc, tc, mode="pair", gqa=False, **ctx)
            o_l = attention(qkv_l, nbl, tl, mode="pair", gqa=False, bias_pairs=bias_pairs, **lat)
        elif mixer == 1:
            diff = dict(mode="diff", gqa=False, lam=lam_rows, subln=subln_1.astype(F32).reshape(1, LANES),
                        lam_init=_diff_lambda_init(l))
            o_c = attention(qkv_c, nbc, tc, **diff, **ctx)
            o_l = attention(qkv_l, nbl, tl, **diff, **lat)
        elif mixer == 2:
            sk = sink_2.astype(F32)
            o_c = attention(qkv_c, nbc, tc, mode="pair", gqa=True, sink=sk, **ctx)
            o_l = attention(qkv_l, nbl, tl, mode="pair", gqa=True, band=True, sink=sk, **lat)
        else:
            o_c = attention(qkv_c, nbc, tc, mode="pair", gqa=True, **ctx)
            o_l = attention(qkv_l, nbl, tl, mode="pair", gqa=True, **lat)

        wo = w_out[l].astype(BF16)
        if l % 2 == 0:
            wg, wu, wd = (w.astype(BF16) for w in ffn_p[l])
            xc = dense_ffn(o_c, wo, xc, mod, ctx_mod(ROW_TILE), norm2[l], wg, wu, wd)
            xl = dense_ffn(o_l, wo, xl, mod, lat_mod(ROW_TILE), norm2[l], wg, wu, wd)
        else:
            wr = ffn_p[l][0]
            wg, wu, wd = (w.astype(BF16) for w in ffn_p[l][1:])
            xc = moe_ffn(o_c, wo, xc, mod, ctx_mod(MOE_BLOCK), norm2[l], wr, wg, wu, wd)
            xl = moe_ffn(o_l, wo, xl, mod, lat_mod(MOE_BLOCK), norm2[l], wr, wg, wu, wd)

    nh = d // HEAD_DIM
    k0, v0 = states[0]
    k1, v1 = states[1]
    k2, v2 = states[2]
    k3, v3 = states[3]
    nkv = k2.shape[1] // HEAD_DIM
    return (xc.reshape(nbc, tc, d), xl.reshape(nbl, tl, d),
            k0.reshape(nbc, tc, nh, HEAD_DIM), v0.reshape(nbc, tc, nh, HEAD_DIM),
            k1.reshape(nbc, tc, nh // 2, 2, HEAD_DIM), v1.reshape(nbc, tc, nh // 2, 2 * HEAD_DIM),
            k2.reshape(nbc, tc, nkv, HEAD_DIM), v2.reshape(nbc, tc, nkv, HEAD_DIM),
            k3.reshape(nbc, tc, nkv, HEAD_DIM), v3.reshape(nbc, tc, nkv, HEAD_DIM))
```

```python
import functools
import math

import numpy as np
import jax
import jax.numpy as jnp
from jax import lax
from jax.experimental import pallas as pl
from jax.experimental.pallas import tpu as pltpu

F32 = jnp.float32
BF16 = jnp.bfloat16

HEAD_DIM = 64
GRID_W = 64
NA_ROWS = 8
NA_COLS = 16
WINDOW = 128
ROPE_BASE = 10000.0
ROPE_PAIRS_AXIS = HEAD_DIM // 4
N_EXPERTS = 8
EPS = 1e-6
NEG_INF = -1e30
QK_SCALE = HEAD_DIM ** -0.5
LOG2E = math.log2(math.e)

LANES = 128
V7X_VMEM_BYTES = 64 * 1024 * 1024
VMEM_LIMIT = V7X_VMEM_BYTES - 8 * 1024 * 1024

ROW_TILE = 512
ATTN_Q_TILE = 256
NA_KEY_ROWS = 12
CTX_LANE_BLOCKS = 8
MOE_BLOCK = 1024
MOE_SUB = 256


def _cparams(sem):
    return pltpu.CompilerParams(dimension_semantics=sem, vmem_limit_bytes=VMEM_LIMIT)


def _silu(x):
    return x * (1.0 / (1.0 + jnp.exp(-x)))


def _norm_mod(x, g, shift, scale):
    ms = jnp.mean(x * x, axis=-1, keepdims=True)
    y = x * lax.rsqrt(ms + EPS) * g
    return y * (1.0 + scale) + shift


def _dot(a, b):
    return jnp.dot(a, b, preferred_element_type=F32)


def _dot_nt(a, b):
    return lax.dot_general(a, b, (((1,), (1,)), ((), ())), preferred_element_type=F32)


def _ada_kernel(c_ref, w_ref, b_ref, o_ref):
    a = _silu(c_ref[...])
    a_hi = a.astype(BF16)
    a_lo = a - a_hi.astype(F32)
    row = lax.broadcasted_iota(jnp.int32, a.shape, 0)
    lhs = jnp.where(row < 8, a_hi.astype(F32), a_lo).astype(BF16)
    w = w_ref[...]
    w_hi = w.astype(BF16)
    w_lo = (w - w_hi.astype(F32)).astype(BF16)
    r = _dot(lhs, w_hi)
    r2 = _dot(a_hi, w_lo)
    o_ref[...] = r[:8] + r[8:] + r2[:8] + b_ref[...]


def ada_mod(cond16, w_ada, b_ada):
    d, n = w_ada.shape
    tn = 1024
    out = pl.pallas_call(
        _ada_kernel,
        grid=(n // tn,),
        in_specs=[pl.BlockSpec((16, d), lambda j: (0, 0)),
                  pl.BlockSpec((d, tn), lambda j: (0, j)),
                  pl.BlockSpec((1, tn), lambda j: (0, j))],
        out_specs=pl.BlockSpec((8, tn), lambda j: (0, j)),
        out_shape=jax.ShapeDtypeStruct((8, n), F32),
        compiler_params=_cparams(("arbitrary",)),
    )(cond16, w_ada, b_ada.reshape(1, n))
    return out.reshape(8, 6, d)


def _qkv_kernel(*refs, rope, n_norm):
    if rope:
        x_ref, mod_ref, g_ref, w_ref, seg_ref, hn_ref, cos_ref, sin_ref, o_ref = refs
    else:
        x_ref, mod_ref, g_ref, w_ref, seg_ref, hn_ref, o_ref = refs
    h = _norm_mod(x_ref[...], g_ref[...], mod_ref[0, 0:1, :], mod_ref[0, 1:2, :]).astype(BF16)
    acc = _dot(h, w_ref[...])
    qk = acc[:, :n_norm]
    sq = (qk * qk).astype(BF16)
    seg = seg_ref[...]
    ssum = jnp.concatenate(
        [_dot(sq[:, c * 2 * LANES:(c + 1) * 2 * LANES], seg) for c in range(n_norm // (2 * LANES))], axis=1)
    y = qk * lax.rsqrt(ssum * (1.0 / HEAD_DIM) + EPS) * hn_ref[...]
    if rope:
        lane = lax.broadcasted_iota(jnp.int32, (acc.shape[0], LANES), 1)
        even = (lane % 2) == 0
        cos, sin = cos_ref[...], sin_ref[...]
        parts = []
        for c in range(n_norm // LANES):
            yc = y[:, c * LANES:(c + 1) * LANES]
            nxt = pltpu.roll(yc, LANES - 1, 1)
            prv = pltpu.roll(yc, 1, 1)
            parts.append(yc * cos + jnp.where(even, nxt, prv) * sin)
        y = jnp.concatenate(parts, axis=1)
    o_ref[:, :n_norm] = y.astype(o_ref.dtype)
    o_ref[:, n_norm:] = acc[:, n_norm:].astype(o_ref.dtype)


def _seg_matrix():
    i = np.arange(2 * LANES)
    return jnp.asarray((i[:, None] // HEAD_DIM) == (i[None, :] // HEAD_DIM), BF16)


def qkv_proj(x, mod, mod_of_block, g, w, hn, rope_tabs, out_dtype):
    r, d = x.shape
    n = w.shape[1]
    n_norm = hn.shape[1]
    tm = ROW_TILE
    rope = rope_tabs is not None
    in_specs = [pl.BlockSpec((tm, d), lambda i: (i, 0)),
                pl.BlockSpec((1, 6, d), lambda i: (mod_of_block(i), 0, 0)),
                pl.BlockSpec((1, d), lambda i: (0, 0)),
                pl.BlockSpec((d, n), lambda i: (0, 0)),
                pl.BlockSpec((2 * LANES, 2 * LANES), lambda i: (0, 0)),
                pl.BlockSpec((1, n_norm), lambda i: (0, 0))]
    args = [x, mod, g.reshape(1, d), w, _seg_matrix(), hn]
    if rope:
        cos_t, sin_t = rope_tabs
        nblk = cos_t.shape[0] // tm
        in_specs += [pl.BlockSpec((tm, LANES), lambda i: (i % nblk, 0)),
                     pl.BlockSpec((tm, LANES), lambda i: (i % nblk, 0))]
        args += [cos_t, sin_t]
    return pl.pallas_call(
        functools.partial(_qkv_kernel, rope=rope, n_norm=n_norm),
        grid=(r // tm,),
        in_specs=in_specs,
        out_specs=pl.BlockSpec((tm, n), lambda i: (i, 0)),
        out_shape=jax.ShapeDtypeStruct((r, n), out_dtype),
        compiler_params=_cparams(("parallel",)),
    )(*args)


def _rope_tables(t):
    pos = np.arange(t)
    row = (pos // GRID_W).astype(np.float32)
    col = (pos % GRID_W).astype(np.float32)
    freqs = jnp.asarray(ROPE_BASE, F32) ** (-jnp.arange(ROPE_PAIRS_AXIS, dtype=F32) / ROPE_PAIRS_AXIS)
    ang = jnp.concatenate([jnp.asarray(row)[:, None] * freqs, jnp.asarray(col)[:, None] * freqs], axis=-1)
    cos = jnp.repeat(jnp.cos(ang), 2, axis=-1)
    sin = jnp.repeat(jnp.sin(ang), 2, axis=-1)
    sign = jnp.asarray(np.where(np.arange(HEAD_DIM) % 2 == 0, -1.0, 1.0), F32)
    reps = LANES // HEAD_DIM
    return jnp.tile(cos, (1, reps)), jnp.tile(sin * sign, (1, reps))


def _in_waves(gens):
    while gens:
        gens = [g for g in gens if next(g, StopIteration) is not StopIteration]
        yield


def _attn_kernel(*refs, n_lb, kmap, mode, n_cache, natten, has_sink, **static):
    it = iter(refs)
    sink_ref = next(it) if has_sink else None
    q_ref, k_ref, v_ref = next(it), next(it), next(it)
    ck_ref, cv_ref = (next(it), next(it)) if n_cache else (None, None)
    bp_ref = next(it) if natten else None
    lam_refs = [next(it), next(it)] if mode == "diff" else []
    o_ref, kf = next(it), next(it)
    v_scr = list(it)
    gens = []
    for jl in range(n_lb):
        q_cols = pl.ds(jl * LANES, LANES)
        k_cols = pl.ds(kmap(jl) * LANES, LANES)
        views = [sink_ref] if has_sink else []
        views += [q_ref.at[:, q_cols], k_ref.at[:, k_cols], v_ref.at[:, k_cols]]
        if n_cache:
            views += [ck_ref.at[:, :, k_cols], cv_ref.at[:, :, k_cols]]
        if natten:
            views.append(bp_ref.at[pl.ds(2 * jl, 2)])
        views += lam_refs + [o_ref.at[:, q_cols], kf.at[jl]] + [v.at[jl] for v in v_scr]
        gens.append(_attn_lane_block(views, pl.program_id(1) * n_lb + jl, pl.program_id(2), mode=mode,
                                     n_cache=n_cache, natten=natten, has_sink=has_sink, **static))
    for _ in _in_waves(gens):
        pass


def _attn_lane_block(refs, j, qb, *, mode, gqa, t_loc, n_cache, win, band, natten, has_sink, lam_init, n_sub):
    it = iter(refs)
    sink_ref = next(it) if has_sink else None
    q_ref, k_ref, v_ref = next(it), next(it), next(it)
    ck_ref = cv_ref = bp_ref = lam_ref = subln_ref = None
    if n_cache:
        ck_ref, cv_ref = next(it), next(it)
    if natten:
        bp_ref = next(it)
    if mode == "diff":
        lam_ref, subln_ref = next(it), next(it)
    o_ref, kf = next(it), next(it)
    v_scr = [next(it), next(it)] if mode == "pair" else [next(it)]

    tq = ATTN_Q_TILE
    nk = t_loc + n_cache
    rows = t_loc // GRID_W
    qrows = tq // GRID_W

    def place(a):
        if not gqa:
            return a
        half = (j // 2) % 2
        lane_half = (lax.broadcasted_iota(jnp.int32, a.shape, 1) >= HEAD_DIM).astype(jnp.int32)
        return jnp.where(lane_half == half, a, pltpu.roll(a, HEAD_DIM, 1))

    def put_v(lo, hi, a):
        if mode == "pair":
            ln = lax.broadcasted_iota(jnp.int32, a.shape, 1)
            v_scr[0][lo:hi, :] = jnp.where(ln < HEAD_DIM, a, 1.0).astype(BF16)
            v_scr[1][lo:hi, :] = jnp.where(ln >= HEAD_DIM, a, 1.0).astype(BF16)
        else:
            v_scr[0][lo:hi, :] = a.astype(BF16)

    @pl.when(qb == 0)
    def _():
        kf[0:t_loc, :] = place(k_ref[...].astype(F32)).astype(BF16)
        put_v(0, t_loc, place(v_ref[...].astype(F32)))
        if n_cache:
            kf[t_loc:nk, :] = place(ck_ref[0].astype(F32)).astype(BF16)
            put_v(t_loc, nk, place(cv_ref[0].astype(F32)))

    lane = lax.broadcasted_iota(jnp.int32, (tq, LANES), 1)
    if mode == "diff":
        lp = lam_ref[...]
        lam = (jnp.exp(jnp.sum(lp[0:1] * lp[1:2], axis=1, keepdims=True))
               - jnp.exp(jnp.sum(lp[2:3] * lp[3:4], axis=1, keepdims=True)) + lam_init)

    def chain(g, hh, q2, sl, k_loc, keep, ws_row, done):
        in_half = (lane < HEAD_DIM) if hh == 0 else (lane >= HEAD_DIM)
        qh = jnp.where(in_half, q2, 0.0).astype(BF16)
        s = _dot_nt(qh, k_loc)
        if n_cache:
            s_c = _dot_nt(qh, kf[t_loc:nk, :])
        yield
        if natten:
            blocks = []
            for a in range(qrows):
                first = ws_row - (g * qrows + a) + NA_ROWS
                blocks.append(jnp.concatenate(
                    [bp_ref[hh, pl.ds(jnp.clip(first + 2 * p, 0, 2 * NA_ROWS - 1), 1)][0]
                     for p in range(win // LANES)], axis=1))
            s = s + jnp.concatenate(blocks, axis=0)
        if keep is not None:
            s = jnp.where(keep, s, NEG_INF)
        m = jnp.max(s, axis=1, keepdims=True)
        if n_cache:
            m = jnp.maximum(m, jnp.max(s_c, axis=1, keepdims=True))
        if has_sink:
            sink = sink_ref[2 * j + hh] * LOG2E
            m = jnp.maximum(m, sink)
        e = jnp.exp2(s - m)
        e_c = jnp.exp2(s_c - m) if n_cache else None
        yield
        if mode == "pair":
            o = _dot(e.astype(BF16), v_scr[hh][sl, :])
            if n_cache:
                o = o + _dot(e_c.astype(BF16), v_scr[hh][t_loc:nk, :])
            yield
            den = pltpu.roll(o, HEAD_DIM, 1)
            if has_sink:
                den = den + jnp.exp2(sink - m)
            done[hh] = o * (1.0 / den)
        else:
            den = jnp.sum(e, axis=1, keepdims=True)
            if n_cache:
                den = den + jnp.sum(e_c, axis=1, keepdims=True)
            done[hh] = (e, e_c, 1.0 / den)

    def sub_tile(t):
        g = qb * n_sub + t
        q2 = q_ref[t * tq:(t + 1) * tq, :].astype(F32) * (QK_SCALE * LOG2E)
        keep = ws_row = None
        if natten:
            ws_row = jnp.clip(g * qrows - NA_ROWS // 2, 0, rows - NA_KEY_ROWS)
            sl = pl.ds(pl.multiple_of(ws_row * GRID_W, GRID_W), win)
            q_row = lax.broadcasted_iota(jnp.int32, (tq, win), 0) // GRID_W + g * qrows
            r0 = jnp.clip(q_row - NA_ROWS // 2, 0, rows - NA_ROWS)
            k_row = lax.broadcasted_iota(jnp.int32, (tq, win), 1) // GRID_W + ws_row
            keep = (k_row >= r0) & (k_row < r0 + NA_ROWS)
        elif band:
            ws = pl.multiple_of(jnp.clip(g * tq - WINDOW, 0, t_loc - win), WINDOW)
            sl = pl.ds(ws, win)
            qpos = g * tq + lax.broadcasted_iota(jnp.int32, (tq, win), 0)
            kpos = ws + lax.broadcasted_iota(jnp.int32, (tq, win), 1)
            keep = jnp.abs(qpos - kpos) <= WINDOW
        else:
            sl = slice(0, win)
        k_loc = kf[sl, :]
        done = [None, None]
        yield from _in_waves([chain(g, hh, q2, sl, k_loc, keep, ws_row, done) for hh in range(2)])
        if mode == "pair":
            out = jnp.where(lane < HEAD_DIM, done[0], done[1])
        else:
            (e0, ec0, w0), (e1, ec1, w1) = done
            w1 = w1 * lam
            out = _dot((e0 * w0 - e1 * w1).astype(BF16), v_scr[0][sl, :])
            if n_cache:
                out = out + _dot((ec0 * w0 - ec1 * w1).astype(BF16), v_scr[0][t_loc:nk, :])
            yield
            ms = jnp.mean(out * out, axis=-1, keepdims=True)
            out = out * lax.rsqrt(ms + EPS) * subln_ref[...] * (1.0 - lam_init)
        o_ref[t * tq:(t + 1) * tq, :] = out.astype(o_ref.dtype)

    yield from _in_waves([sub_tile(t) for t in range(n_sub)])


def attention(qkv, nb, t_loc, *, mode, gqa, k_col, v_col, n_sub, n_lb, cache=None, band=False, bias_pairs=None,
              sink=None, lam=None, subln=None, lam_init=0.0, d_model):
    tq = ATTN_Q_TILE * n_sub
    n_qlb = d_model // LANES
    natten = bias_pairs is not None
    n_cache = 0 if cache is None else cache[0].shape[1]
    if natten:
        win = NA_KEY_ROWS * GRID_W
    elif band:
        win = ATTN_Q_TILE + 2 * WINDOW
    else:
        win = t_loc
    if gqa:
        assert n_lb == 1 or n_lb % 4 == 0
        k_lb = max(n_lb // 4, 1)
        kgrp = (lambda jg: jg // 4) if n_lb == 1 else (lambda jg: jg)
        kmap = lambda jl: jl // 4
    else:
        k_lb = n_lb
        kgrp = lambda jg: jg
        kmap = lambda jl: jl
    qw, kw = n_lb * LANES, k_lb * LANES
    assert k_col % kw == 0 and v_col % kw == 0 and n_qlb % n_lb == 0
    kc, vc = k_col // kw, v_col // kw
    nqb = t_loc // tq
    nk = t_loc + n_cache

    in_specs = []
    args = []
    if sink is not None:
        in_specs.append(pl.BlockSpec(memory_space=pltpu.SMEM))
        args.append(sink)
    in_specs += [pl.BlockSpec((tq, qw), lambda b, j, i: (b * nqb + i, j)),
                 pl.BlockSpec((t_loc, kw), lambda b, j, i: (b, kc + kgrp(j))),
                 pl.BlockSpec((t_loc, kw), lambda b, j, i: (b, vc + kgrp(j)))]
    args += [qkv, qkv, qkv]
    if cache is not None:
        in_specs += [pl.BlockSpec((1, n_cache, kw), lambda b, j, i: (b, 0, kgrp(j))),
                     pl.BlockSpec((1, n_cache, kw), lambda b, j, i: (b, 0, kgrp(j)))]
        args += list(cache)
    if natten:
        in_specs.append(pl.BlockSpec((2 * n_lb, 2 * NA_ROWS, GRID_W, LANES), lambda b, j, i: (j, 0, 0, 0)))
        args.append(bias_pairs)
    if mode == "diff":
        in_specs += [pl.BlockSpec((8, LANES), lambda b, j, i: (0, 0)),
                     pl.BlockSpec((1, LANES), lambda b, j, i: (0, 0))]
        args += [lam, subln]
    kern = functools.partial(
        _attn_kernel, n_lb=n_lb, kmap=kmap, mode=mode, gqa=gqa, t_loc=t_loc, n_cache=n_cache, win=win, band=band,
        natten=natten, has_sink=sink is not None, lam_init=lam_init, n_sub=n_sub)
    n_v = 2 if mode == "pair" else 1
    return pl.pallas_call(
        kern,
        grid=(nb, n_qlb // n_lb, nqb),
        in_specs=in_specs,
        out_specs=pl.BlockSpec((tq, qw), lambda b, j, i: (b * nqb + i, j)),
        out_shape=jax.ShapeDtypeStruct((nb * t_loc, d_model), BF16),
        scratch_shapes=[pltpu.VMEM((n_lb, nk, LANES), BF16) for _ in range(1 + n_v)],
        compiler_params=_cparams(("parallel", "arbitrary", "arbitrary")),
    )(*args)


def _natten_pair_table(rpb):
    h, n_dr, n_dc = rpb.shape
    pad = GRID_W - NA_COLS
    ext = jnp.concatenate([jnp.repeat(rpb[..., :1], pad, axis=-1), rpb,
                           jnp.repeat(rpb[..., -1:], pad + 1, axis=-1)], axis=-1).astype(F32) * LOG2E
    neg = jnp.full((h, 1, LANES), NEG_INF, F32)
    ext = jnp.concatenate([neg, ext, neg], axis=1)
    return pl.pallas_call(
        _pair_table_kernel,
        grid=(h,),
        in_specs=[pl.BlockSpec((1, n_dr + 2, LANES), lambda i: (i, 0, 0))],
        out_specs=pl.BlockSpec((1, n_dr + 1, GRID_W, LANES), lambda i: (i, 0, 0, 0)),
        out_shape=jax.ShapeDtypeStruct((h, n_dr + 1, GRID_W, LANES), F32),
        compiler_params=_cparams(("parallel",)),
    )(ext)


def _pair_table_kernel(ext_ref, o_ref):
    qc = lax.broadcasted_iota(jnp.int32, (GRID_W, LANES), 0)
    ln = lax.broadcasted_iota(jnp.int32, (GRID_W, LANES), 1)
    kc = ln % GRID_W
    cstart = jnp.clip(qc - NA_COLS // 2, 0, GRID_W - NA_COLS)
    in_window = (kc >= cstart) & (kc < cstart + NA_COLS)
    for d in range(o_ref.shape[1]):
        lo = jnp.broadcast_to(ext_ref[0, d:d + 1, :], (GRID_W, LANES))
        hi = jnp.broadcast_to(ext_ref[0, d + 1:d + 2, :], (GRID_W, LANES))
        lo = pltpu.roll(lo, GRID_W + 1, 1, stride=1, stride_axis=0)
        hi = pltpu.roll(hi, 1, 1, stride=1, stride_axis=0)
        o_ref[0, d] = jnp.where(in_window, jnp.where(ln < GRID_W, lo, hi), NEG_INF)


def _mixer_residual(a_ref, wo_ref, x_ref, mod_ref):
    return x_ref[...] + mod_ref[0, 2:3, :] * _dot(a_ref[...], wo_ref[...])


def _ffn_kernel(a_ref, wo_ref, x_ref, mod_ref, g_ref, wg_ref, wu_ref, wd_ref, o_ref, h_scr, acc_scr, x1_scr):
    f = pl.program_id(1)

    @pl.when(f == 0)
    def _():
        x1 = _mixer_residual(a_ref, wo_ref, x_ref, mod_ref)
        x1_scr[...] = x1
        h = _norm_mod(x1, g_ref[...], mod_ref[0, 3:4, :], mod_ref[0, 4:5, :])
        h_scr[...] = h.astype(BF16)

    h = h_scr[...]
    t = (_silu(_dot(h, wg_ref[...])) * _dot(h, wu_ref[...])).astype(BF16)
    y = _dot(t, wd_ref[...])

    @pl.when(f == 0)
    def _():
        acc_scr[...] = y

    @pl.when(f > 0)
    def _():
        acc_scr[...] += y

    @pl.when(f == pl.num_programs(1) - 1)
    def _():
        o_ref[...] = x1_scr[...] + mod_ref[0, 5:6, :] * acc_scr[...]


def dense_ffn(a, wo, x, mod, mod_of_block, g, wg, wu, wd):
    r, d = x.shape
    ff = wg.shape[1]
    tm = ROW_TILE
    nf = 2
    tf = ff // nf
    return pl.pallas_call(
        _ffn_kernel,
        grid=(r // tm, nf),
        in_specs=[pl.BlockSpec((tm, d), lambda i, f: (i, 0)),
                  pl.BlockSpec((d, d), lambda i, f: (0, 0)),
                  pl.BlockSpec((tm, d), lambda i, f: (i, 0)),
                  pl.BlockSpec((1, 6, d), lambda i, f: (mod_of_block(i), 0, 0)),
                  pl.BlockSpec((1, d), lambda i, f: (0, 0)),
                  pl.BlockSpec((d, tf), lambda i, f: (0, f)),
                  pl.BlockSpec((d, tf), lambda i, f: (0, f)),
                  pl.BlockSpec((tf, d), lambda i, f: (f, 0))],
        out_specs=pl.BlockSpec((tm, d), lambda i, f: (i, 0)),
        out_shape=jax.ShapeDtypeStruct((r, d), F32),
        scratch_shapes=[pltpu.VMEM((tm, d), BF16), pltpu.VMEM((tm, d), F32), pltpu.VMEM((tm, d), F32)],
        compiler_params=_cparams(("parallel", "arbitrary")),
    )(a, wo, x, mod, g.reshape(1, d), wg, wu, wd)


def _router_kernel(a_ref, wo_ref, x_ref, mod_ref, g_ref, wr_ref,
                   x1_ref, h_ref, rank_row_ref, gate_row_ref, rank_col_ref, cnt_ref):
    x1 = _mixer_residual(a_ref, wo_ref, x_ref, mod_ref)
    x1_ref[...] = x1
    h = _norm_mod(x1, g_ref[...], mod_ref[0, 3:4, :], mod_ref[0, 4:5, :])
    h_hi = h.astype(BF16)
    h_ref[...] = h_hi
    h_lo = (h - h_hi.astype(F32)).astype(BF16)
    wr = wr_ref[...]
    w_hi = wr.astype(BF16)
    w_lo = (wr - w_hi.astype(F32)).astype(BF16)
    logits = _dot(h_hi, w_hi) + (_dot(h_lo, w_hi) + _dot(h_hi, w_lo))
    tb = logits.shape[0]
    lane = lax.broadcasted_iota(jnp.int32, logits.shape, 1)
    lane_f = lane.astype(F32)
    logits = jnp.where(lane < N_EXPERTS, logits, -jnp.inf)
    m1 = jnp.max(logits, axis=1, keepdims=True)
    i1 = jnp.min(jnp.where(logits == m1, lane_f, float(LANES)), axis=1, keepdims=True)
    rest = jnp.where(lane_f == i1, -jnp.inf, logits)
    m2 = jnp.max(rest, axis=1, keepdims=True)
    i2 = jnp.min(jnp.where(rest == m2, lane_f, float(LANES)), axis=1, keepdims=True)
    e2 = jnp.exp(m2 - m1)
    inv = 1.0 / (1.0 + e2)
    is1 = lane_f == i1
    is2 = lane_f == i2
    gates = jnp.where(is1, inv, 0.0) + jnp.where(is2, e2 * inv, 0.0)
    sel = jnp.where(is1 | is2, 1.0, 0.0)
    sel_b = sel.astype(BF16)
    chunk = 256
    parts = []
    for c in range(tb // chunk):
        ri = lax.broadcasted_iota(jnp.int32, (chunk, tb), 0) + c * chunk
        ci = lax.broadcasted_iota(jnp.int32, (chunk, tb), 1)
        parts.append(_dot(jnp.where(ci < ri, 1.0, 0.0).astype(BF16), sel_b))
    rank = jnp.concatenate(parts, axis=0)
    rank_sel = jnp.where(sel > 0.5, rank, -1.0)
    rank_col_ref[...] = rank_sel
    rank_row_ref[0] = rank_sel.T[:N_EXPERTS]
    gate_row_ref[0] = gates.T[:N_EXPERTS]
    cnt_ref[0] = jnp.broadcast_to(jnp.sum(sel, axis=0, keepdims=True), (8, LANES)).astype(jnp.int32)


def _expert_kernel(cnt_ref, h_ref, rank_row_ref, gate_row_ref, rank_col_ref, wg_ref, wu_ref, wd_ref,
                   x_ref, mod_ref, o_ref):
    b, e = pl.program_id(0), pl.program_id(1)
    tb = h_ref.shape[0]
    n_rows = cnt_ref[b * N_EXPERTS + e]

    @pl.when(e == 0)
    def _():
        o_ref[...] = jnp.zeros_like(o_ref)

    rank_e = rank_row_ref[0, pl.ds(e, 1), :]
    gate_e = gate_row_ref[0, pl.ds(e, 1), :]
    lane = lax.broadcasted_iota(jnp.int32, (tb, LANES), 1)
    rank_c = jnp.sum(jnp.where(lane == e, rank_col_ref[...], 0.0), axis=1, keepdims=True)

    def tile(base, m):
        want = (lax.broadcasted_iota(jnp.int32, (m, tb), 0) + base).astype(F32)
        p = jnp.where(rank_e == want, 1.0, 0.0)
        xr = _dot(p.astype(BF16), h_ref[...]).astype(BF16)
        gate = jnp.sum(p * gate_e, axis=1, keepdims=True)
        t = (_silu(_dot(xr, wg_ref[0])) * _dot(xr, wu_ref[0])).astype(BF16)
        yg = (_dot(t, wd_ref[0]) * gate).astype(BF16)
        want_t = (lax.broadcasted_iota(jnp.int32, (tb, m), 1) + base).astype(F32)
        pt = jnp.where(rank_c == want_t, 1.0, 0.0).astype(BF16)
        o_ref[...] += _dot(pt, yg)

    n_full = n_rows // MOE_SUB
    rest = n_rows - n_full * MOE_SUB

    def full_tile(i, carry):
        tile(pl.multiple_of(i * MOE_SUB, MOE_SUB), MOE_SUB)
        return carry

    lax.fori_loop(0, n_full, full_tile, 0)
    tail_base = pl.multiple_of(n_full * MOE_SUB, MOE_SUB)
    pl.when(rest > MOE_SUB // 2)(lambda: tile(tail_base, MOE_SUB))
    pl.when((rest > 0) & (rest <= MOE_SUB // 2))(lambda: tile(tail_base, MOE_SUB // 2))

    @pl.when(e == N_EXPERTS - 1)
    def _():
        o_ref[...] = x_ref[...] + mod_ref[0, 5:6, :] * o_ref[...]


def moe_ffn(a, wo, x, mod, mod_of_block, g, w_router, wg, wu, wd):
    r, d = x.shape
    ff = wg.shape[2]
    tb = MOE_BLOCK
    nb = r // tb
    wr = jnp.pad(w_router, ((0, 0), (0, LANES - N_EXPERTS)))
    x, h, rank_row, gate_row, rank_col, cnt = pl.pallas_call(
        _router_kernel,
        grid=(nb,),
        in_specs=[pl.BlockSpec((tb, d), lambda i: (i, 0)),
                  pl.BlockSpec((d, d), lambda i: (0, 0)),
                  pl.BlockSpec((tb, d), lambda i: (i, 0)),
                  pl.BlockSpec((1, 6, d), lambda i: (mod_of_block(i), 0, 0)),
                  pl.BlockSpec((1, d), lambda i: (0, 0)),
                  pl.BlockSpec((d, LANES), lambda i: (0, 0))],
        out_specs=[pl.BlockSpec((tb, d), lambda i: (i, 0)),
                   pl.BlockSpec((tb, d), lambda i: (i, 0)),
                   pl.BlockSpec((1, N_EXPERTS, tb), lambda i: (i, 0, 0)),
                   pl.BlockSpec((1, N_EXPERTS, tb), lambda i: (i, 0, 0)),
                   pl.BlockSpec((tb, LANES), lambda i: (i, 0)),
                   pl.BlockSpec((1, 8, LANES), lambda i: (i, 0, 0))],
        out_shape=[jax.ShapeDtypeStruct((r, d), F32),
                   jax.ShapeDtypeStruct((r, d), BF16),
                   jax.ShapeDtypeStruct((nb, N_EXPERTS, tb), F32),
                   jax.ShapeDtypeStruct((nb, N_EXPERTS, tb), F32),
                   jax.ShapeDtypeStruct((r, LANES), F32),
                   jax.ShapeDtypeStruct((nb, 8, LANES), jnp.int32)],
        compiler_params=_cparams(("parallel",)),
    )(a, wo, x, mod, g.reshape(1, d), wr)
    counts = cnt[:, 0, :N_EXPERTS].reshape(nb * N_EXPERTS)

    grid_spec = pltpu.PrefetchScalarGridSpec(
        num_scalar_prefetch=1,
        grid=(nb, N_EXPERTS),
        in_specs=[pl.BlockSpec((tb, d), lambda i, e, c: (i, 0)),
                  pl.BlockSpec((1, N_EXPERTS, tb), lambda i, e, c: (i, 0, 0)),
                  pl.BlockSpec((1, N_EXPERTS, tb), lambda i, e, c: (i, 0, 0)),
                  pl.BlockSpec((tb, LANES), lambda i, e, c: (i, 0)),
                  pl.BlockSpec((1, d, ff), lambda i, e, c: (e, 0, 0)),
                  pl.BlockSpec((1, d, ff), lambda i, e, c: (e, 0, 0)),
                  pl.BlockSpec((1, ff, d), lambda i, e, c: (e, 0, 0)),
                  pl.BlockSpec((tb, d), lambda i, e, c: (i, 0)),
                  pl.BlockSpec((1, 6, d), lambda i, e, c: (mod_of_block(i), 0, 0))],
        out_specs=pl.BlockSpec((tb, d), lambda i, e, c: (i, 0)),
    )
    return pl.pallas_call(
        _expert_kernel,
        grid_spec=grid_spec,
        out_shape=jax.ShapeDtypeStruct((r, d), F32),
        compiler_params=_cparams(("parallel", "arbitrary")),
    )(counts, h, rank_row, gate_row, rank_col, wg, wu, wd, x, mod)


def _diff_lambda_init(layer):
    return 0.8 - 0.6 * math.exp(-0.3 * layer)


def _head_gain_row(q_norm, k_norm, n_q, n_k):
    return jnp.concatenate([jnp.tile(q_norm.astype(F32), n_q // HEAD_DIM),
                            jnp.tile(k_norm.astype(F32), n_k // HEAD_DIM)]).reshape(1, -1)


def kernel(x_prompt, x_sample, cache_k_0, cache_v_0, cache_k_1, cache_v_1, cache_k_2, cache_v_2, cache_k_3, cache_v_3, c, c_ctx, norm1_0, w_ada_0, b_ada_0, w_qkv_0, q_norm_0, k_norm_0, rpb_0, w_o_0, norm2_0, w_gate_0, w_up_0, w_down_0, norm1_1, w_ada_1, b_ada_1, w_qkv_1, q_norm_1, k_norm_1, lam_q1_1, lam_k1_1, lam_q2_1, lam_k2_1, subln_1, w_o_1, norm2_1, w_router_1, w_egate_1, w_eup_1, w_edown_1, norm1_2, w_ada_2, b_ada_2, w_qkv_2, q_norm_2, k_norm_2, sink_2, w_o_2, norm2_2, w_gate_2, w_up_2, w_down_2, norm1_3, w_ada_3, b_ada_3, w_qkv_3, q_norm_3, k_norm_3, w_o_3, norm2_3, w_router_3, w_egate_3, w_eup_3, w_edown_3):
    nbc, tc, d = x_prompt.shape
    nbl, tl, _ = x_sample.shape
    n_past = cache_k_0.shape[1]
    assert d % (2 * LANES) == 0 and tc % ATTN_Q_TILE == 0 and tl % MOE_BLOCK == 0
    assert (nbc * tc) % MOE_BLOCK == 0 and tl // GRID_W >= NA_KEY_ROWS and nbl <= 7

    norm1 = (norm1_0, norm1_1, norm1_2, norm1_3)
    norm2 = (norm2_0, norm2_1, norm2_2, norm2_3)
    ada_p = ((w_ada_0, b_ada_0), (w_ada_1, b_ada_1), (w_ada_2, b_ada_2), (w_ada_3, b_ada_3))
    w_qkv = (w_qkv_0, w_qkv_1, w_qkv_2, w_qkv_3)
    qk_norm = ((q_norm_0, k_norm_0), (q_norm_1, k_norm_1), (q_norm_2, k_norm_2), (q_norm_3, k_norm_3))
    w_out = (w_o_0, w_o_1, w_o_2, w_o_3)
    ffn_p = ((w_gate_0, w_up_0, w_down_0), (w_router_1, w_egate_1, w_eup_1, w_edown_1),
             (w_gate_2, w_up_2, w_down_2), (w_router_3, w_egate_3, w_eup_3, w_edown_3))
    caches = ((cache_k_0, cache_v_0), (cache_k_1, cache_v_1), (cache_k_2, cache_v_2), (cache_k_3, cache_v_3))

    xc = x_prompt.reshape(nbc * tc, d)
    xl = x_sample.reshape(nbl * tl, d)
    cond = jnp.concatenate([c_ctx[None, :], c, jnp.zeros((7 - nbl, d), F32)], axis=0)
    cond16 = jnp.concatenate([cond, cond], axis=0)

    def ctx_mod(blk_rows):
        return lambda i: 0

    def lat_mod(blk_rows):
        per = tl // blk_rows
        return lambda i: 1 + i // per

    lam_rows = jnp.zeros((8, LANES), F32)
    for r_, v_ in enumerate((lam_q1_1, lam_k1_1, lam_q2_1, lam_k2_1)):
        lam_rows = lam_rows.at[r_, :HEAD_DIM].set(v_.astype(F32))
    bias_pairs = _natten_pair_table(rpb_0)
    rope_tabs = _rope_tables(tl)

    states = []
    for l in range(4):
        mixer = l % 4
        mod = ada_mod(cond16, *ada_p[l])
        wq = w_qkv[l].astype(BF16)
        n = wq.shape[1]
        n_q = d
        n_k = d if mixer < 2 else (n - d) // 2
        n_v = n - n_q - n_k
        hn = _head_gain_row(*qk_norm[l], n_q, n_k)
        rope = None if mixer == 0 else rope_tabs

        qkv_c = qkv_proj(xc, mod, ctx_mod(ROW_TILE), norm1[l], wq, hn, None, F32)
        qkv_l = qkv_proj(xl, mod, lat_mod(ROW_TILE), norm1[l], wq, hn, rope, BF16)
        states.append((qkv_c[:, n_q:n_q + n_k], qkv_c[:, n_q + n_k:]))

        ck, cv = caches[l]
        cache = (ck.astype(BF16).reshape(nbl, n_past, n_k), cv.astype(BF16).reshape(nbl, n_past, n_v))
        common = dict(k_col=n_q, v_col=n_q + n_k, d_model=d)
        ctx = dict(n_sub=1, n_lb=CTX_LANE_BLOCKS, **common)
        lat = dict(n_sub=4, n_lb=1, cache=cache, **common)
        if mixer == 0:
            o_c = attention(qkv_c, nbc, tc, mode="pair", gqa=False, **ctx)
            o_l = attention(qkv_l, nbl, tl, mode="pair", gqa=False, bias_pairs=bias_pairs, **lat)
        elif mixer == 1:
            diff = dict(mode="diff", gqa=False, lam=lam_rows, subln=subln_1.astype(F32).reshape(1, LANES),
                        lam_init=_diff_lambda_init(l))
            o_c = attention(qkv_c, nbc, tc, **diff, **ctx)
            o_l = attention(qkv_l, nbl, tl, **diff, **lat)
        elif mixer == 2:
            sk = sink_2.astype(F32)
            o_c = attention(qkv_c, nbc, tc, mode="pair", gqa=True, sink=sk, **ctx)
            o_l = attention(qkv_l, nbl, tl, mode="pair", gqa=True, band=True, sink=sk, **lat)
        else:
            o_c = attention(qkv_c, nbc, tc, mode="pair", gqa=True, **ctx)
            o_l = attention(qkv_l, nbl, tl, mode="pair", gqa=True, **lat)

        wo = w_out[l].astype(BF16)
        if l % 2 == 0:
            wg, wu, wd = (w.astype(BF16) for w in ffn_p[l])
            xc = dense_ffn(o_c, wo, xc, mod, ctx_mod(ROW_TILE), norm2[l], wg, wu, wd)
            xl = dense_ffn(o_l, wo, xl, mod, lat_mod(ROW_TILE), norm2[l], wg, wu, wd)
        else:
            wr = ffn_p[l][0]
            wg, wu, wd = (w.astype(BF16) for w in ffn_p[l][1:])
            xc = moe_ffn(o_c, wo, xc, mod, ctx_mod(MOE_BLOCK), norm2[l], wr, wg, wu, wd)
            xl = moe_ffn(o_l, wo, xl, mod, lat_mod(MOE_BLOCK), norm2[l], wr, wg, wu, wd)

    nh = d // HEAD_DIM
    k0, v0 = states[0]
    k1, v1 = states[1]
    k2, v2 = states[2]
    k3, v3 = states[3]
    nkv = k2.shape[1] // HEAD_DIM
    return (xc.reshape(nbc, tc, d), xl.reshape(nbl, tl, d),
            k0.reshape(nbc, tc, nh, HEAD_DIM), v0.reshape(nbc, tc, nh, HEAD_DIM),
            k1.reshape(nbc, tc, nh // 2, 2, HEAD_DIM), v1.reshape(nbc, tc, nh // 2, 2 * HEAD_DIM),
            k2.reshape(nbc, tc, nkv, HEAD_DIM), v2.reshape(nbc, tc, nkv, HEAD_DIM),
            k3.reshape(nbc, tc, nkv, HEAD_DIM), v3.reshape(nbc, tc, nkv, HEAD_DIM))
```

```python
import functools
import math

import numpy as np
import jax
import jax.numpy as jnp
from jax import lax
from jax.experimental import pallas as pl
from jax.experimental.pallas import tpu as pltpu

F32 = jnp.float32
BF16 = jnp.bfloat16

HEAD_DIM = 64
GRID_W = 64
NA_ROWS = 8
NA_COLS = 16
WINDOW = 128
ROPE_BASE = 10000.0
ROPE_PAIRS_AXIS = HEAD_DIM // 4
N_EXPERTS = 8
EPS = 1e-6
NEG_INF = -1e30
QK_SCALE = HEAD_DIM ** -0.5
LOG2E = math.log2(math.e)

LANES = 128
V7X_VMEM_BYTES = 64 * 1024 * 1024
VMEM_LIMIT = V7X_VMEM_BYTES - 8 * 1024 * 1024

ROW_TILE = 512
ATTN_Q_TILE = 256
NA_KEY_ROWS = 12
CTX_LANE_BLOCKS = 8
MOE_BLOCK = 1024
MOE_SUB = 256


def _cparams(sem):
    return pltpu.CompilerParams(dimension_semantics=sem, vmem_limit_bytes=VMEM_LIMIT)


def _silu(x):
    return x * (1.0 / (1.0 + jnp.exp(-x)))


def _norm_mod(x, g, shift, scale):
    ms = jnp.mean(x * x, axis=-1, keepdims=True)
    y = x * lax.rsqrt(ms + EPS) * g
    return y * (1.0 + scale) + shift


def _dot(a, b):
    return jnp.dot(a, b, preferred_element_type=F32)


def _dot_nt(a, b):
    return lax.dot_general(a, b, (((1,), (1,)), ((), ())), preferred_element_type=F32)


def _ada_kernel(c_ref, w_ref, b_ref, o_ref):
    a = _silu(c_ref[...])
    a_hi = a.astype(BF16)
    a_lo = a - a_hi.astype(F32)
    row = lax.broadcasted_iota(jnp.int32, a.shape, 0)
    lhs = jnp.where(row < 8, a_hi.astype(F32), a_lo).astype(BF16)
    w = w_ref[...]
    w_hi = w.astype(BF16)
    w_lo = (w - w_hi.astype(F32)).astype(BF16)
    r = _dot(lhs, w_hi)
    r2 = _dot(a_hi, w_lo)
    o_ref[...] = r[:8] + r[8:] + r2[:8] + b_ref[...]


def ada_mod(cond16, w_ada, b_ada):
    d, n = w_ada.shape
    tn = 1024
    out = pl.pallas_call(
        _ada_kernel,
        grid=(n // tn,),
        in_specs=[pl.BlockSpec((16, d), lambda j: (0, 0)),
                  pl.BlockSpec((d, tn), lambda j: (0, j)),
                  pl.BlockSpec((1, tn), lambda j: (0, j))],
        out_specs=pl.BlockSpec((8, tn), lambda j: (0, j)),
        out_shape=jax.ShapeDtypeStruct((8, n), F32),
        compiler_params=_cparams(("arbitrary",)),
    )(cond16, w_ada, b_ada.reshape(1, n))
    return out.reshape(8, 6, d)


def _qkv_kernel(*refs, rope, n_norm):
    if rope:
        x_ref, mod_ref, g_ref, w_ref, seg_ref, hn_ref, cos_ref, sin_ref, o_ref = refs
    else:
        x_ref, mod_ref, g_ref, w_ref, seg_ref, hn_ref, o_ref = refs
    h = _norm_mod(x_ref[...], g_ref[...], mod_ref[0, 0:1, :], mod_ref[0, 1:2, :]).astype(BF16)
    acc = _dot(h, w_ref[...])
    qk = acc[:, :n_norm]
    sq = (qk * qk).astype(BF16)
    seg = seg_ref[...]
    ssum = jnp.concatenate(
        [_dot(sq[:, c * 2 * LANES:(c + 1) * 2 * LANES], seg) for c in range(n_norm // (2 * LANES))], axis=1)
    y = qk * lax.rsqrt(ssum * (1.0 / HEAD_DIM) + EPS) * hn_ref[...]
    if rope:
        lane = lax.broadcasted_iota(jnp.int32, (acc.shape[0], LANES), 1)
        even = (lane % 2) == 0
        cos, sin = cos_ref[...], sin_ref[...]
        parts = []
        for c in range(n_norm // LANES):
            yc = y[:, c * LANES:(c + 1) * LANES]
            nxt = pltpu.roll(yc, LANES - 1, 1)
            prv = pltpu.roll(yc, 1, 1)
            parts.append(yc * cos + jnp.where(even, nxt, prv) * sin)
        y = jnp.concatenate(parts, axis=1)
    o_ref[:, :n_norm] = y.astype(o_ref.dtype)
    o_ref[:, n_norm:] = acc[:, n_norm:].astype(o_ref.dtype)


def _seg_matrix():
    i = np.arange(2 * LANES)
    return jnp.asarray((i[:, None] // HEAD_DIM) == (i[None, :] // HEAD_DIM), BF16)


def qkv_proj(x, mod, mod_of_block, g, w, hn, rope_tabs, out_dtype):
    r, d = x.shape
    n = w.shape[1]
    n_norm = hn.shape[1]
    tm = ROW_TILE
    rope = rope_tabs is not None
    in_specs = [pl.BlockSpec((tm, d), lambda i: (i, 0)),
                pl.BlockSpec((1, 6, d), lambda i: (mod_of_block(i), 0, 0)),
                pl.BlockSpec((1, d), lambda i: (0, 0)),
                pl.BlockSpec((d, n), lambda i: (0, 0)),
                pl.BlockSpec((2 * LANES, 2 * LANES), lambda i: (0, 0)),
                pl.BlockSpec((1, n_norm), lambda i: (0, 0))]
    args = [x, mod, g.reshape(1, d), w, _seg_matrix(), hn]
    if rope:
        cos_t, sin_t = rope_tabs
        nblk = cos_t.shape[0] // tm
        in_specs += [pl.BlockSpec((tm, LANES), lambda i: (i % nblk, 0)),
                     pl.BlockSpec((tm, LANES), lambda i: (i % nblk, 0))]
        args += [cos_t, sin_t]
    return pl.pallas_call(
        functools.partial(_qkv_kernel, rope=rope, n_norm=n_norm),
        grid=(r // tm,),
        in_specs=in_specs,
        out_specs=pl.BlockSpec((tm, n), lambda i: (i, 0)),
        out_shape=jax.ShapeDtypeStruct((r, n), out_dtype),
        compiler_params=_cparams(("parallel",)),
    )(*args)


def _rope_tables(t):
    pos = np.arange(t)
    row = (pos // GRID_W).astype(np.float32)
    col = (pos % GRID_W).astype(np.float32)
    freqs = jnp.asarray(ROPE_BASE, F32) ** (-jnp.arange(ROPE_PAIRS_AXIS, dtype=F32) / ROPE_PAIRS_AXIS)
    ang = jnp.concatenate([jnp.asarray(row)[:, None] * freqs, jnp.asarray(col)[:, None] * freqs], axis=-1)
    cos = jnp.repeat(jnp.cos(ang), 2, axis=-1)
    sin = jnp.repeat(jnp.sin(ang), 2, axis=-1)
    sign = jnp.asarray(np.where(np.arange(HEAD_DIM) % 2 == 0, -1.0, 1.0), F32)
    reps = LANES // HEAD_DIM
    return jnp.tile(cos, (1, reps)), jnp.tile(sin * sign, (1, reps))


def _in_waves(gens):
    while gens:
        gens = [g for g in gens if next(g, StopIteration) is not StopIteration]
        yield


def _attn_kernel(*refs, n_lb, kmap, mode, n_cache, natten, has_sink, **static):
    it = iter(refs)
    sink_ref = next(it) if has_sink else None
    q_ref, k_ref, v_ref = next(it), next(it), next(it)
    ck_ref, cv_ref = (next(it), next(it)) if n_cache else (None, None)
    bp_ref = next(it) if natten else None
    lam_refs = [next(it), next(it)] if mode == "diff" else []
    o_ref, kf = next(it), next(it)
    v_scr = list(it)
    gens = []
    for jl in range(n_lb):
        q_cols = pl.ds(jl * LANES, LANES)
        k_cols = pl.ds(kmap(jl) * LANES, LANES)
        views = [sink_ref] if has_sink else []
        views += [q_ref.at[:, q_cols], k_ref.at[:, k_cols], v_ref.at[:, k_cols]]
        if n_cache:
            views += [ck_ref.at[:, :, k_cols], cv_ref.at[:, :, k_cols]]
        if natten:
            views.append(bp_ref.at[pl.ds(2 * jl, 2)])
        views += lam_refs + [o_ref.at[:, q_cols], kf.at[jl]] + [v.at[jl] for v in v_scr]
        gens.append(_attn_lane_block(views, pl.program_id(1) * n_lb + jl, pl.program_id(2), mode=mode,
                                     n_cache=n_cache, natten=natten, has_sink=has_sink, **static))
    for _ in _in_waves(gens):
        pass


def _attn_lane_block(refs, j, qb, *, mode, gqa, t_loc, n_cache, win, band, natten, has_sink, lam_init, n_sub):
    it = iter(refs)
    sink_ref = next(it) if has_sink else None
    q_ref, k_ref, v_ref = next(it), next(it), next(it)
    ck_ref = cv_ref = bp_ref = lam_ref = subln_ref = None
    if n_cache:
        ck_ref, cv_ref = next(it), next(it)
    if natten:
        bp_ref = next(it)
    if mode == "diff":
        lam_ref, subln_ref = next(it), next(it)
    o_ref, kf = next(it), next(it)
    v_scr = [next(it), next(it)] if mode == "pair" else [next(it)]

    tq = ATTN_Q_TILE
    nk = t_loc + n_cache
    rows = t_loc // GRID_W
    qrows = tq // GRID_W

    def place(a):
        if not gqa:
            return a
        half = (j // 2) % 2
        lane_half = (lax.broadcasted_iota(jnp.int32, a.shape, 1) >= HEAD_DIM).astype(jnp.int32)
        return jnp.where(lane_half == half, a, pltpu.roll(a, HEAD_DIM, 1))

    def put_v(lo, hi, a):
        if mode == "pair":
            ln = lax.broadcasted_iota(jnp.int32, a.shape, 1)
            v_scr[0][lo:hi, :] = jnp.where(ln < HEAD_DIM, a, 1.0).astype(BF16)
            v_scr[1][lo:hi, :] = jnp.where(ln >= HEAD_DIM, a, 1.0).astype(BF16)
        else:
            v_scr[0][lo:hi, :] = a.astype(BF16)

    @pl.when(qb == 0)
    def _():
        kf[0:t_loc, :] = place(k_ref[...].astype(F32)).astype(BF16)
        put_v(0, t_loc, place(v_ref[...].astype(F32)))
        if n_cache:
            kf[t_loc:nk, :] = place(ck_ref[0].astype(F32)).astype(BF16)
            put_v(t_loc, nk, place(cv_ref[0].astype(F32)))

    lane = lax.broadcasted_iota(jnp.int32, (tq, LANES), 1)
    if mode == "diff":
        lp = lam_ref[...]
        lam = (jnp.exp(jnp.sum(lp[0:1] * lp[1:2], axis=1, keepdims=True))
               - jnp.exp(jnp.sum(lp[2:3] * lp[3:4], axis=1, keepdims=True)) + lam_init)

    def chain(g, hh, q2, sl, k_loc, keep, ws_row, done):
        in_half = (lane < HEAD_DIM) if hh == 0 else (lane >= HEAD_DIM)
        qh = jnp.where(in_half, q2, 0.0).astype(BF16)
        s = _dot_nt(qh, k_loc)
        if n_cache:
            s_c = _dot_nt(qh, kf[t_loc:nk, :])
        yield
        if natten:
            blocks = []
            for a in range(qrows):
                first = ws_row - (g * qrows + a) + NA_ROWS
                blocks.append(jnp.concatenate(
                    [bp_ref[hh, pl.ds(jnp.clip(first + 2 * p, 0, 2 * NA_ROWS - 1), 1)][0]
                     for p in range(win // LANES)], axis=1))
            s = s + jnp.concatenate(blocks, axis=0)
        if keep is not None:
            s = jnp.where(keep, s, NEG_INF)
        m = jnp.max(s, axis=1, keepdims=True)
        if n_cache:
            m = jnp.maximum(m, jnp.max(s_c, axis=1, keepdims=True))
        if has_sink:
            sink = sink_ref[2 * j + hh] * LOG2E
            m = jnp.maximum(m, sink)
        e = jnp.exp2(s - m)
        e_c = jnp.exp2(s_c - m) if n_cache else None
        yield
        if mode == "pair":
            o = _dot(e.astype(BF16), v_scr[hh][sl, :])
            if n_cache:
                o = o + _dot(e_c.astype(BF16), v_scr[hh][t_loc:nk, :])
            yield
            den = pltpu.roll(o, HEAD_DIM, 1)
            if has_sink:
                den = den + jnp.exp2(sink - m)
            done[hh] = o * (1.0 / den)
        else:
            den = jnp.sum(e, axis=1, keepdims=True)
            if n_cache:
                den = den + jnp.sum(e_c, axis=1, keepdims=True)
            done[hh] = (e, e_c, 1.0 / den)

    def sub_tile(t):
        g = qb * n_sub + t
        q2 = q_ref[t * tq:(t + 1) * tq, :].astype(F32) * (QK_SCALE * LOG2E)
        keep = ws_row = None
        if natten:
            ws_row = jnp.clip(g * qrows - NA_ROWS // 2, 0, rows - NA_KEY_ROWS)
            sl = pl.ds(pl.multiple_of(ws_row * GRID_W, GRID_W), win)
            q_row = lax.broadcasted_iota(jnp.int32, (tq, win), 0) // GRID_W + g * qrows
            r0 = jnp.clip(q_row - NA_ROWS // 2, 0, rows - NA_ROWS)
            k_row = lax.broadcasted_iota(jnp.int32, (tq, win), 1) // GRID_W + ws_row
            keep = (k_row >= r0) & (k_row < r0 + NA_ROWS)
        elif band:
            ws = pl.multiple_of(jnp.clip(g * tq - WINDOW, 0, t_loc - win), WINDOW)
            sl = pl.ds(ws, win)
            qpos = g * tq + lax.broadcasted_iota(jnp.int32, (tq, win), 0)
            kpos = ws + lax.broadcasted_iota(jnp.int32, (tq, win), 1)
            keep = jnp.abs(qpos - kpos) <= WINDOW
        else:
            sl = slice(0, win)
        k_loc = kf[sl, :]
        done = [None, None]
        yield from _in_waves([chain(g, hh, q2, sl, k_loc, keep, ws_row, done) for hh in range(2)])
        if mode == "pair":
            out = jnp.where(lane < HEAD_DIM, done[0], done[1])
        else:
            (e0, ec0, w0), (e1, ec1, w1) = done
            w1 = w1 * lam
            out = _dot((e0 * w0 - e1 * w1).astype(BF16), v_scr[0][sl, :])
            if n_cache:
                out = out + _dot((ec0 * w0 - ec1 * w1).astype(BF16), v_scr[0][t_loc:nk, :])
            yield
            ms = jnp.mean(out * out, axis=-1, keepdims=True)
            out = out * lax.rsqrt(ms + EPS) * subln_ref[...] * (1.0 - lam_init)
        o_ref[t * tq:(t + 1) * tq, :] = out.astype(o_ref.dtype)

    yield from _in_waves([sub_tile(t) for t in range(n_sub)])


def attention(qkv, nb, t_loc, *, mode, gqa, k_col, v_col, n_sub, n_lb, cache=None, band=False, bias_pairs=None,
              sink=None, lam=None, subln=None, lam_init=0.0, d_model):
    tq = ATTN_Q_TILE * n_sub
    n_qlb = d_model // LANES
    natten = bias_pairs is not None
    n_cache = 0 if cache is None else cache[0].shape[1]
    if natten:
        win = NA_KEY_ROWS * GRID_W
    elif band:
        win = ATTN_Q_TILE + 2 * WINDOW
    else:
        win = t_loc
    if gqa:
        assert n_lb == 1 or n_lb % 4 == 0
        k_lb = max(n_lb // 4, 1)
        kgrp = (lambda jg: jg // 4) if n_lb == 1 else (lambda jg: jg)
        kmap = lambda jl: jl // 4
    else:
        k_lb = n_lb
        kgrp = lambda jg: jg
        kmap = lambda jl: jl
    qw, kw = n_lb * LANES, k_lb * LANES
    assert k_col % kw == 0 and v_col % kw == 0 and n_qlb % n_lb == 0
    kc, vc = k_col // kw, v_col // kw
    nqb = t_loc // tq
    nk = t_loc + n_cache

    in_specs = []
    args = []
    if sink is not None:
        in_specs.append(pl.BlockSpec(memory_space=pltpu.SMEM))
        args.append(sink)
    in_specs += [pl.BlockSpec((tq, qw), lambda b, j, i: (b * nqb + i, j)),
                 pl.BlockSpec((t_loc, kw), lambda b, j, i: (b, kc + kgrp(j))),
                 pl.BlockSpec((t_loc, kw), lambda b, j, i: (b, vc + kgrp(j)))]
    args += [qkv, qkv, qkv]
    if cache is not None:
        in_specs += [pl.BlockSpec((1, n_cache, kw), lambda b, j, i: (b, 0, kgrp(j))),
                     pl.BlockSpec((1, n_cache, kw), lambda b, j, i: (b, 0, kgrp(j)))]
        args += list(cache)
    if natten:
        in_specs.append(pl.BlockSpec((2 * n_lb, 2 * NA_ROWS, GRID_W, LANES), lambda b, j, i: (j, 0, 0, 0)))
        args.append(bias_pairs)
    if mode == "diff":
        in_specs += [pl.BlockSpec((8, LANES), lambda b, j, i: (0, 0)),
                     pl.BlockSpec((1, LANES), lambda b, j, i: (0, 0))]
        args += [lam, subln]
    kern = functools.partial(
        _attn_kernel, n_lb=n_lb, kmap=kmap, mode=mode, gqa=gqa, t_loc=t_loc, n_cache=n_cache, win=win, band=band,
        natten=natten, has_sink=sink is not None, lam_init=lam_init, n_sub=n_sub)
    n_v = 2 if mode == "pair" else 1
    return pl.pallas_call(
        kern,
        grid=(nb, n_qlb // n_lb, nqb),
        in_specs=in_specs,
        out_specs=pl.BlockSpec((tq, qw), lambda b, j, i: (b * nqb + i, j)),
        out_shape=jax.ShapeDtypeStruct((nb * t_loc, d_model), BF16),
        scratch_shapes=[pltpu.VMEM((n_lb, nk, LANES), BF16) for _ in range(1 + n_v)],
        compiler_params=_cparams(("parallel", "arbitrary", "arbitrary")),
    )(*args)


def _natten_pair_table(rpb):
    h, n_dr, n_dc = rpb.shape
    pad = GRID_W - NA_COLS
    ext = jnp.concatenate([jnp.repeat(rpb[..., :1], pad, axis=-1), rpb,
                           jnp.repeat(rpb[..., -1:], pad + 1, axis=-1)], axis=-1).astype(F32) * LOG2E
    neg = jnp.full((h, 1, LANES), NEG_INF, F32)
    ext = jnp.concatenate([neg, ext, neg], axis=1)
    return pl.pallas_call(
        _pair_table_kernel,
        grid=(h,),
        in_specs=[pl.BlockSpec((1, n_dr + 2, LANES), lambda i: (i, 0, 0))],
        out_specs=pl.BlockSpec((1, n_dr + 1, GRID_W, LANES), lambda i: (i, 0, 0, 0)),
        out_shape=jax.ShapeDtypeStruct((h, n_dr + 1, GRID_W, LANES), F32),
        compiler_params=_cparams(("parallel",)),
    )(ext)


def _pair_table_kernel(ext_ref, o_ref):
    qc = lax.broadcasted_iota(jnp.int32, (GRID_W, LANES), 0)
    ln = lax.broadcasted_iota(jnp.int32, (GRID_W, LANES), 1)
    kc = ln % GRID_W
    cstart = jnp.clip(qc - NA_COLS // 2, 0, GRID_W - NA_COLS)
    in_window = (kc >= cstart) & (kc < cstart + NA_COLS)
    for d in range(o_ref.shape[1]):
        lo = jnp.broadcast_to(ext_ref[0, d:d + 1, :], (GRID_W, LANES))
        hi = jnp.broadcast_to(ext_ref[0, d + 1:d + 2, :], (GRID_W, LANES))
        lo = pltpu.roll(lo, GRID_W + 1, 1, stride=1, stride_axis=0)
        hi = pltpu.roll(hi, 1, 1, stride=1, stride_axis=0)
        o_ref[0, d] = jnp.where(in_window, jnp.where(ln < GRID_W, lo, hi), NEG_INF)


def _mixer_residual(a_ref, wo_ref, x_ref, mod_ref):
    return x_ref[...] + mod_ref[0, 2:3, :] * _dot(a_ref[...], wo_ref[...])


def _ffn_kernel(a_ref, wo_ref, x_ref, mod_ref, g_ref, wg_ref, wu_ref, wd_ref, o_ref):
    x1 = _mixer_residual(a_ref, wo_ref, x_ref, mod_ref)
    h = _norm_mod(x1, g_ref[...], mod_ref[0, 3:4, :], mod_ref[0, 4:5, :]).astype(BF16)
    t = (_silu(_dot(h, wg_ref[...])) * _dot(h, wu_ref[...])).astype(BF16)
    o_ref[...] = x1 + mod_ref[0, 5:6, :] * _dot(t, wd_ref[...])


def dense_ffn(a, wo, x, mod, mod_of_block, g, wg, wu, wd):
    r, d = x.shape
    ff = wg.shape[1]
    tm = ROW_TILE

    def resident(shape):
        return pl.BlockSpec(shape, lambda i: (0, 0), pipeline_mode=pl.Buffered(1))

    return pl.pallas_call(
        _ffn_kernel,
        grid=(r // tm,),
        in_specs=[pl.BlockSpec((tm, d), lambda i: (i, 0)),
                  resident((d, d)),
                  pl.BlockSpec((tm, d), lambda i: (i, 0)),
                  pl.BlockSpec((1, 6, d), lambda i: (mod_of_block(i), 0, 0)),
                  pl.BlockSpec((1, d), lambda i: (0, 0)),
                  resident((d, ff)),
                  resident((d, ff)),
                  resident((ff, d))],
        out_specs=pl.BlockSpec((tm, d), lambda i: (i, 0)),
        out_shape=jax.ShapeDtypeStruct((r, d), F32),
        compiler_params=_cparams(("parallel",)),
    )(a, wo, x, mod, g.reshape(1, d), wg, wu, wd)


def _router_kernel(a_ref, wo_ref, x_ref, mod_ref, g_ref, wr_ref,
                   x1_ref, h_ref, rank_row_ref, gate_row_ref, rank_col_ref, cnt_ref):
    x1 = _mixer_residual(a_ref, wo_ref, x_ref, mod_ref)
    x1_ref[...] = x1
    h = _norm_mod(x1, g_ref[...], mod_ref[0, 3:4, :], mod_ref[0, 4:5, :])
    h_hi = h.astype(BF16)
    h_ref[...] = h_hi
    h_lo = (h - h_hi.astype(F32)).astype(BF16)
    wr = wr_ref[...]
    w_hi = wr.astype(BF16)
    w_lo = (wr - w_hi.astype(F32)).astype(BF16)
    logits = _dot(h_hi, w_hi) + (_dot(h_lo, w_hi) + _dot(h_hi, w_lo))
    tb = logits.shape[0]
    lane = lax.broadcasted_iota(jnp.int32, logits.shape, 1)
    lane_f = lane.astype(F32)
    logits = jnp.where(lane < N_EXPERTS, logits, -jnp.inf)
    m1 = jnp.max(logits, axis=1, keepdims=True)
    i1 = jnp.min(jnp.where(logits == m1, lane_f, float(LANES)), axis=1, keepdims=True)
    rest = jnp.where(lane_f == i1, -jnp.inf, logits)
    m2 = jnp.max(rest, axis=1, keepdims=True)
    i2 = jnp.min(jnp.where(rest == m2, lane_f, float(LANES)), axis=1, keepdims=True)
    e2 = jnp.exp(m2 - m1)
    inv = 1.0 / (1.0 + e2)
    is1 = lane_f == i1
    is2 = lane_f == i2
    gates = jnp.where(is1, inv, 0.0) + jnp.where(is2, e2 * inv, 0.0)
    sel = jnp.where(is1 | is2, 1.0, 0.0)
    sel_b = sel.astype(BF16)
    chunk = 256
    parts = []
    for c in range(tb // chunk):
        ri = lax.broadcasted_iota(jnp.int32, (chunk, tb), 0) + c * chunk
        ci = lax.broadcasted_iota(jnp.int32, (chunk, tb), 1)
        parts.append(_dot(jnp.where(ci < ri, 1.0, 0.0).astype(BF16), sel_b))
    rank = jnp.concatenate(parts, axis=0)
    rank_sel = jnp.where(sel > 0.5, rank, -1.0)
    rank_col_ref[...] = rank_sel
    rank_row_ref[0] = rank_sel.T[:N_EXPERTS]
    gate_row_ref[0] = gates.T[:N_EXPERTS]
    cnt_ref[0] = jnp.broadcast_to(jnp.sum(sel, axis=0, keepdims=True), (8, LANES)).astype(jnp.int32)


def _expert_kernel(cnt_ref, h_ref, rank_row_ref, gate_row_ref, rank_col_ref, wg_ref, wu_ref, wd_ref,
                   x_ref, mod_ref, o_ref):
    b, e = pl.program_id(0), pl.program_id(1)
    tb = h_ref.shape[0]
    n_rows = cnt_ref[b * N_EXPERTS + e]

    @pl.when(e == 0)
    def _():
        o_ref[...] = jnp.zeros_like(o_ref)

    rank_e = rank_row_ref[0, pl.ds(e, 1), :]
    gate_e = gate_row_ref[0, pl.ds(e, 1), :]
    lane = lax.broadcasted_iota(jnp.int32, (tb, LANES), 1)
    rank_c = jnp.sum(jnp.where(lane == e, rank_col_ref[...], 0.0), axis=1, keepdims=True)

    def tile(base, m):
        want = (lax.broadcasted_iota(jnp.int32, (m, tb), 0) + base).astype(F32)
        p = jnp.where(rank_e == want, 1.0, 0.0)
        xr = _dot(p.astype(BF16), h_ref[...]).astype(BF16)
        gate = jnp.sum(p * gate_e, axis=1, keepdims=True)
        t = (_silu(_dot(xr, wg_ref[0])) * _dot(xr, wu_ref[0])).astype(BF16)
        yg = (_dot(t, wd_ref[0]) * gate).astype(BF16)
        want_t = (lax.broadcasted_iota(jnp.int32, (tb, m), 1) + base).astype(F32)
        pt = jnp.where(rank_c == want_t, 1.0, 0.0).astype(BF16)
        o_ref[...] += _dot(pt, yg)

    n_full = n_rows // MOE_SUB
    rest = n_rows - n_full * MOE_SUB

    def full_tile(i, carry):
        tile(pl.multiple_of(i * MOE_SUB, MOE_SUB), MOE_SUB)
        return carry

    lax.fori_loop(0, n_full, full_tile, 0)
    tail_base = pl.multiple_of(n_full * MOE_SUB, MOE_SUB)
    pl.when(rest > MOE_SUB // 2)(lambda: tile(tail_base, MOE_SUB))
    pl.when((rest > 0) & (rest <= MOE_SUB // 2))(lambda: tile(tail_base, MOE_SUB // 2))

    @pl.when(e == N_EXPERTS - 1)
    def _():
        o_ref[...] = x_ref[...] + mod_ref[0, 5:6, :] * o_ref[...]


def moe_ffn(a, wo, x, mod, mod_of_block, g, w_router, wg, wu, wd):
    r, d = x.shape
    ff = wg.shape[2]
    tb = MOE_BLOCK
    nb = r // tb
    wr = jnp.pad(w_router, ((0, 0), (0, LANES - N_EXPERTS)))
    x, h, rank_row, gate_row, rank_col, cnt = pl.pallas_call(
        _router_kernel,
        grid=(nb,),
        in_specs=[pl.BlockSpec((tb, d), lambda i: (i, 0)),
                  pl.BlockSpec((d, d), lambda i: (0, 0)),
                  pl.BlockSpec((tb, d), lambda i: (i, 0)),
                  pl.BlockSpec((1, 6, d), lambda i: (mod_of_block(i), 0, 0)),
                  pl.BlockSpec((1, d), lambda i: (0, 0)),
                  pl.BlockSpec((d, LANES), lambda i: (0, 0))],
        out_specs=[pl.BlockSpec((tb, d), lambda i: (i, 0)),
                   pl.BlockSpec((tb, d), lambda i: (i, 0)),
                   pl.BlockSpec((1, N_EXPERTS, tb), lambda i: (i, 0, 0)),
                   pl.BlockSpec((1, N_EXPERTS, tb), lambda i: (i, 0, 0)),
                   pl.BlockSpec((tb, LANES), lambda i: (i, 0)),
                   pl.BlockSpec((1, 8, LANES), lambda i: (i, 0, 0))],
        out_shape=[jax.ShapeDtypeStruct((r, d), F32),
                   jax.ShapeDtypeStruct((r, d), BF16),
                   jax.ShapeDtypeStruct((nb, N_EXPERTS, tb), F32),
                   jax.ShapeDtypeStruct((nb, N_EXPERTS, tb), F32),
                   jax.ShapeDtypeStruct((r, LANES), F32),
                   jax.ShapeDtypeStruct((nb, 8, LANES), jnp.int32)],
        compiler_params=_cparams(("parallel",)),
    )(a, wo, x, mod, g.reshape(1, d), wr)
    counts = cnt[:, 0, :N_EXPERTS].reshape(nb * N_EXPERTS)

    grid_spec = pltpu.PrefetchScalarGridSpec(
        num_scalar_prefetch=1,
        grid=(nb, N_EXPERTS),
        in_specs=[pl.BlockSpec((tb, d), lambda i, e, c: (i, 0)),
                  pl.BlockSpec((1, N_EXPERTS, tb), lambda i, e, c: (i, 0, 0)),
                  pl.BlockSpec((1, N_EXPERTS, tb), lambda i, e, c: (i, 0, 0)),
                  pl.BlockSpec((tb, LANES), lambda i, e, c: (i, 0)),
                  pl.BlockSpec((1, d, ff), lambda i, e, c: (e, 0, 0)),
                  pl.BlockSpec((1, d, ff), lambda i, e, c: (e, 0, 0)),
                  pl.BlockSpec((1, ff, d), lambda i, e, c: (e, 0, 0)),
                  pl.BlockSpec((tb, d), lambda i, e, c: (i, 0)),
                  pl.BlockSpec((1, 6, d), lambda i, e, c: (mod_of_block(i), 0, 0))],
        out_specs=pl.BlockSpec((tb, d), lambda i, e, c: (i, 0)),
    )
    return pl.pallas_call(
        _expert_kernel,
        grid_spec=grid_spec,
        out_shape=jax.ShapeDtypeStruct((r, d), F32),
        compiler_params=_cparams(("parallel", "arbitrary")),
    )(counts, h, rank_row, gate_row, rank_col, wg, wu, wd, x, mod)


def _diff_lambda_init(layer):
    return 0.8 - 0.6 * math.exp(-0.3 * layer)


def _head_gain_row(q_norm, k_norm, n_q, n_k):
    return jnp.concatenate([jnp.tile(q_norm.astype(F32), n_q // HEAD_DIM),
                            jnp.tile(k_norm.astype(F32), n_k // HEAD_DIM)]).reshape(1, -1)


def kernel(x_prompt, x_sample, cache_k_0, cache_v_0, cache_k_1, cache_v_1, cache_k_2, cache_v_2, cache_k_3, cache_v_3, c, c_ctx, norm1_0, w_ada_0, b_ada_0, w_qkv_0, q_norm_0, k_norm_0, rpb_0, w_o_0, norm2_0, w_gate_0, w_up_0, w_down_0, norm1_1, w_ada_1, b_ada_1, w_qkv_1, q_norm_1, k_norm_1, lam_q1_1, lam_k1_1, lam_q2_1, lam_k2_1, subln_1, w_o_1, norm2_1, w_router_1, w_egate_1, w_eup_1, w_edown_1, norm1_2, w_ada_2, b_ada_2, w_qkv_2, q_norm_2, k_norm_2, sink_2, w_o_2, norm2_2, w_gate_2, w_up_2, w_down_2, norm1_3, w_ada_3, b_ada_3, w_qkv_3, q_norm_3, k_norm_3, w_o_3, norm2_3, w_router_3, w_egate_3, w_eup_3, w_edown_3):
    nbc, tc, d = x_prompt.shape
    nbl, tl, _ = x_sample.shape
    n_past = cache_k_0.shape[1]
    assert d % (2 * LANES) == 0 and tc % ATTN_Q_TILE == 0 and tl % MOE_BLOCK == 0
    assert (nbc * tc) % MOE_BLOCK == 0 and tl // GRID_W >= NA_KEY_ROWS and nbl <= 7

    norm1 = (norm1_0, norm1_1, norm1_2, norm1_3)
    norm2 = (norm2_0, norm2_1, norm2_2, norm2_3)
    ada_p = ((w_ada_0, b_ada_0), (w_ada_1, b_ada_1), (w_ada_2, b_ada_2), (w_ada_3, b_ada_3))
    w_qkv = (w_qkv_0, w_qkv_1, w_qkv_2, w_qkv_3)
    qk_norm = ((q_norm_0, k_norm_0), (q_norm_1, k_norm_1), (q_norm_2, k_norm_2), (q_norm_3, k_norm_3))
    w_out = (w_o_0, w_o_1, w_o_2, w_o_3)
    ffn_p = ((w_gate_0, w_up_0, w_down_0), (w_router_1, w_egate_1, w_eup_1, w_edown_1),
             (w_gate_2, w_up_2, w_down_2), (w_router_3, w_egate_3, w_eup_3, w_edown_3))
    caches = ((cache_k_0, cache_v_0), (cache_k_1, cache_v_1), (cache_k_2, cache_v_2), (cache_k_3, cache_v_3))

    xc = x_prompt.reshape(nbc * tc, d)
    xl = x_sample.reshape(nbl * tl, d)
    cond = jnp.concatenate([c_ctx[None, :], c, jnp.zeros((7 - nbl, d), F32)], axis=0)
    cond16 = jnp.concatenate([cond, cond], axis=0)

    def ctx_mod(blk_rows):
        return lambda i: 0

    def lat_mod(blk_rows):
        per = tl // blk_rows
        return lambda i: 1 + i // per

    lam_rows = jnp.zeros((8, LANES), F32)
    for r_, v_ in enumerate((lam_q1_1, lam_k1_1, lam_q2_1, lam_k2_1)):
        lam_rows = lam_rows.at[r_, :HEAD_DIM].set(v_.astype(F32))
    bias_pairs = _natten_pair_table(rpb_0)
    rope_tabs = _rope_tables(tl)

    states = []
    for l in range(4):
        mixer = l % 4
        mod = ada_mod(cond16, *ada_p[l])
        wq = w_qkv[l].astype(BF16)
        n = wq.shape[1]
        n_q = d
        n_k = d if mixer < 2 else (n - d) // 2
        n_v = n - n_q - n_k
        hn = _head_gain_row(*qk_norm[l], n_q, n_k)
        rope = None if mixer == 0 else rope_tabs

        qkv_c = qkv_proj(xc, mod, ctx_mod(ROW_TILE), norm1[l], wq, hn, None, F32)
        qkv_l = qkv_proj(xl, mod, lat_mod(ROW_TILE), norm1[l], wq, hn, rope, BF16)
        states.append((qkv_c[:, n_q:n_q + n_k], qkv_c[:, n_q + n_k:]))

        ck, cv = caches[l]
        cache = (ck.astype(BF16).reshape(nbl, n_past, n_k), cv.astype(BF16).reshape(nbl, n_past, n_v))
        common = dict(k_col=n_q, v_col=n_q + n_k, d_model=d)
        ctx = dict(n_sub=1, n_lb=CTX_LANE_BLOCKS, **common)
        lat = dict(n_sub=4, n_lb=1, cache=cache, **common)
        if mixer == 0:
            o_c = attention(qkv_c, nbc, tc, mode="pair", gqa=False, **ctx)
            o_l = attention(qkv_l, nbl, tl, mode="pair", gqa=False, bias_pairs=bias_pairs, **lat)
        elif mixer == 1:
            diff = dict(mode="diff", gqa=False, lam=lam_rows, subln=subln_1.astype(F32).reshape(1, LANES),
                        lam_init=_diff_lambda_init(l))
            o_c = attention(qkv_c, nbc, tc, **diff, **ctx)
            o_l = attention(qkv_l, nbl, tl, **diff, **lat)
        elif mixer == 2:
            sk = sink_2.astype(F32)
            o_c = attention(qkv_c, nbc, tc, mode="pair", gqa=True, sink=sk, **ctx)
            o_l = attention(qkv_l, nbl, tl, mode="pair", gqa=True, band=True, sink=sk, **lat)
        else:
            o_c = attention(qkv_c, nbc, tc, mode="pair", gqa=True, **ctx)
            o_l = attention(qkv_l, nbl, tl, mode="pair", gqa=True, **lat)

        wo = w_out[l].astype(BF16)
        if l % 2 == 0:
            wg, wu, wd = (w.astype(BF16) for w in ffn_p[l])
            xc = dense_ffn(o_c, wo, xc, mod, ctx_mod(ROW_TILE), norm2[l], wg, wu, wd)
            xl = dense_ffn(o_l, wo, xl, mod, lat_mod(ROW_TILE), norm2[l], wg, wu, wd)
        else:
            wr = ffn_p[l][0]
            wg, wu, wd = (w.astype(BF16) for w in ffn_p[l][1:])
            xc = moe_ffn(o_c, wo, xc, mod, ctx_mod(MOE_BLOCK), norm2[l], wr, wg, wu, wd)
            xl = moe_ffn(o_l, wo, xl, mod, lat_mod(MOE_BLOCK), norm2[l], wr, wg, wu, wd)

    nh = d // HEAD_DIM
    k0, v0 = states[0]
    k1, v1 = states[1]
    k2, v2 = states[2]
    k3, v3 = states[3]
    nkv = k2.shape[1] // HEAD_DIM
    return (xc.reshape(nbc, tc, d), xl.reshape(nbl, tl, d),
            k0.reshape(nbc, tc, nh, HEAD_DIM), v0.reshape(nbc, tc, nh, HEAD_DIM),
            k1.reshape(nbc, tc, nh // 2, 2, HEAD_DIM), v1.reshape(nbc, tc, nh // 2, 2 * HEAD_DIM),
            k2.reshape(nbc, tc, nkv, HEAD_DIM), v2.reshape(nbc, tc, nkv, HEAD_DIM),
            k3.reshape(nbc, tc, nkv, HEAD_DIM), v3.reshape(nbc, tc, nkv, HEAD_DIM))
```

```python
import functools
import math

import numpy as np
import jax
import jax.numpy as jnp
from jax import lax
from jax.experimental import pallas as pl
from jax.experimental.pallas import tpu as pltpu

F32 = jnp.float32
BF16 = jnp.bfloat16

HEAD_DIM = 64
GRID_W = 64
NA_ROWS = 8
NA_COLS = 16
WINDOW = 128
ROPE_BASE = 10000.0
ROPE_PAIRS_AXIS = HEAD_DIM // 4
N_EXPERTS = 8
EPS = 1e-6
NEG_INF = -1e30
QK_SCALE = HEAD_DIM ** -0.5
LOG2E = math.log2(math.e)

LANES = 128
V7X_VMEM_BYTES = 64 * 1024 * 1024
VMEM_LIMIT = V7X_VMEM_BYTES - 8 * 1024 * 1024

ROW_TILE = 512
ATTN_Q_TILE = 256
NA_KEY_ROWS = 12
CTX_LANE_BLOCKS = 8
MOE_BLOCK = 1024
MOE_SUB = 256


def _cparams(sem):
    return pltpu.CompilerParams(dimension_semantics=sem, vmem_limit_bytes=VMEM_LIMIT)


def _silu(x):
    return x * (1.0 / (1.0 + jnp.exp(-x)))


def _norm_mod(x, g, shift, scale):
    ms = jnp.mean(x * x, axis=-1, keepdims=True)
    y = x * lax.rsqrt(ms + EPS) * g
    return y * (1.0 + scale) + shift


def _dot(a, b):
    return jnp.dot(a, b, preferred_element_type=F32)


def _dot_nt(a, b):
    return lax.dot_general(a, b, (((1,), (1,)), ((), ())), preferred_element_type=F32)


def _ada_kernel(c_ref, w_ref, b_ref, o_ref):
    a = _silu(c_ref[...])
    a_hi = a.astype(BF16)
    a_lo = a - a_hi.astype(F32)
    row = lax.broadcasted_iota(jnp.int32, a.shape, 0)
    lhs = jnp.where(row < 8, a_hi.astype(F32), a_lo).astype(BF16)
    w = w_ref[...]
    w_hi = w.astype(BF16)
    w_lo = (w - w_hi.astype(F32)).astype(BF16)
    r = _dot(lhs, w_hi)
    r2 = _dot(a_hi, w_lo)
    o_ref[...] = r[:8] + r[8:] + r2[:8] + b_ref[...]


def ada_mod(cond16, w_ada, b_ada):
    d, n = w_ada.shape
    tn = 1024
    out = pl.pallas_call(
        _ada_kernel,
        grid=(n // tn,),
        in_specs=[pl.BlockSpec((16, d), lambda j: (0, 0)),
                  pl.BlockSpec((d, tn), lambda j: (0, j)),
                  pl.BlockSpec((1, tn), lambda j: (0, j))],
        out_specs=pl.BlockSpec((8, tn), lambda j: (0, j)),
        out_shape=jax.ShapeDtypeStruct((8, n), F32),
        compiler_params=_cparams(("arbitrary",)),
    )(cond16, w_ada, b_ada.reshape(1, n))
    return out.reshape(8, 6, d)


def _qkv_kernel(*refs, rope, n_norm):
    if rope:
        x_ref, mod_ref, g_ref, w_ref, seg_ref, hn_ref, cos_ref, sin_ref, o_ref = refs
    else:
        x_ref, mod_ref, g_ref, w_ref, seg_ref, hn_ref, o_ref = refs
    h = _norm_mod(x_ref[...], g_ref[...], mod_ref[0, 0:1, :], mod_ref[0, 1:2, :]).astype(BF16)
    acc = _dot(h, w_ref[...])
    qk = acc[:, :n_norm]
    sq = (qk * qk).astype(BF16)
    seg = seg_ref[...]
    ssum = jnp.concatenate(
        [_dot(sq[:, c * 2 * LANES:(c + 1) * 2 * LANES], seg) for c in range(n_norm // (2 * LANES))], axis=1)
    y = qk * lax.rsqrt(ssum * (1.0 / HEAD_DIM) + EPS) * hn_ref[...]
    if rope:
        lane = lax.broadcasted_iota(jnp.int32, (acc.shape[0], LANES), 1)
        even = (lane % 2) == 0
        cos, sin = cos_ref[...], sin_ref[...]
        parts = []
        for c in range(n_norm // LANES):
            yc = y[:, c * LANES:(c + 1) * LANES]
            nxt = pltpu.roll(yc, LANES - 1, 1)
            prv = pltpu.roll(yc, 1, 1)
            parts.append(yc * cos + jnp.where(even, nxt, prv) * sin)
        y = jnp.concatenate(parts, axis=1)
    o_ref[:, :n_norm] = y.astype(o_ref.dtype)
    o_ref[:, n_norm:] = acc[:, n_norm:].astype(o_ref.dtype)


def _seg_matrix():
    i = np.arange(2 * LANES)
    return jnp.asarray((i[:, None] // HEAD_DIM) == (i[None, :] // HEAD_DIM), BF16)


def qkv_proj(x, mod, mod_of_block, g, w, hn, rope_tabs, out_dtype):
    r, d = x.shape
    n = w.shape[1]
    n_norm = hn.shape[1]
    tm = ROW_TILE
    rope = rope_tabs is not None
    in_specs = [pl.BlockSpec((tm, d), lambda i: (i, 0)),
                pl.BlockSpec((1, 6, d), lambda i: (mod_of_block(i), 0, 0)),
                pl.BlockSpec((1, d), lambda i: (0, 0)),
                pl.BlockSpec((d, n), lambda i: (0, 0)),
                pl.BlockSpec((2 * LANES, 2 * LANES), lambda i: (0, 0)),
                pl.BlockSpec((1, n_norm), lambda i: (0, 0))]
    args = [x, mod, g.reshape(1, d), w, _seg_matrix(), hn]
    if rope:
        cos_t, sin_t = rope_tabs
        nblk = cos_t.shape[0] // tm
        in_specs += [pl.BlockSpec((tm, LANES), lambda i: (i % nblk, 0)),
                     pl.BlockSpec((tm, LANES), lambda i: (i % nblk, 0))]
        args += [cos_t, sin_t]
    return pl.pallas_call(
        functools.partial(_qkv_kernel, rope=rope, n_norm=n_norm),
        grid=(r // tm,),
        in_specs=in_specs,
        out_specs=pl.BlockSpec((tm, n), lambda i: (i, 0)),
        out_shape=jax.ShapeDtypeStruct((r, n), out_dtype),
        compiler_params=_cparams(("parallel",)),
    )(*args)


def _rope_tables(t):
    pos = np.arange(t)
    row = (pos // GRID_W).astype(np.float32)
    col = (pos % GRID_W).astype(np.float32)
    freqs = jnp.asarray(ROPE_BASE, F32) ** (-jnp.arange(ROPE_PAIRS_AXIS, dtype=F32) / ROPE_PAIRS_AXIS)
    ang = jnp.concatenate([jnp.asarray(row)[:, None] * freqs, jnp.asarray(col)[:, None] * freqs], axis=-1)
    cos = jnp.repeat(jnp.cos(ang), 2, axis=-1)
    sin = jnp.repeat(jnp.sin(ang), 2, axis=-1)
    sign = jnp.asarray(np.where(np.arange(HEAD_DIM) % 2 == 0, -1.0, 1.0), F32)
    reps = LANES // HEAD_DIM
    return jnp.tile(cos, (1, reps)), jnp.tile(sin * sign, (1, reps))


def _in_waves(gens):
    while gens:
        gens = [g for g in gens if next(g, StopIteration) is not StopIteration]
        yield


def _attn_kernel(*refs, n_lb, kmap, mode, n_cache, natten, has_sink, **static):
    it = iter(refs)
    sink_ref = next(it) if has_sink else None
    q_ref, k_ref, v_ref = next(it), next(it), next(it)
    ck_ref, cv_ref = (next(it), next(it)) if n_cache else (None, None)
    bp_ref = next(it) if natten else None
    lam_refs = [next(it), next(it)] if mode == "diff" else []
    o_ref, kf = next(it), next(it)
    v_scr = list(it)
    gens = []
    for jl in range(n_lb):
        q_cols = pl.ds(jl * LANES, LANES)
        k_cols = pl.ds(kmap(jl) * LANES, LANES)
        views = [sink_ref] if has_sink else []
        views += [q_ref.at[:, q_cols], k_ref.at[:, k_cols], v_ref.at[:, k_cols]]
        if n_cache:
            views += [ck_ref.at[:, :, k_cols], cv_ref.at[:, :, k_cols]]
        if natten:
            views.append(bp_ref.at[pl.ds(2 * jl, 2)])
        views += lam_refs + [o_ref.at[:, q_cols], kf.at[jl]] + [v.at[jl] for v in v_scr]
        gens.append(_attn_lane_block(views, pl.program_id(1) * n_lb + jl, pl.program_id(2), mode=mode,
                                     n_cache=n_cache, natten=natten, has_sink=has_sink, **static))
    for _ in _in_waves(gens):
        pass


def _attn_lane_block(refs, j, qb, *, mode, gqa, t_loc, n_cache, win, band, natten, has_sink, lam_init, n_sub):
    it = iter(refs)
    sink_ref = next(it) if has_sink else None
    q_ref, k_ref, v_ref = next(it), next(it), next(it)
    ck_ref = cv_ref = bp_ref = lam_ref = subln_ref = None
    if n_cache:
        ck_ref, cv_ref = next(it), next(it)
    if natten:
        bp_ref = next(it)
    if mode == "diff":
        lam_ref, subln_ref = next(it), next(it)
    o_ref, kf = next(it), next(it)
    v_scr = [next(it), next(it)] if mode == "pair" else [next(it)]

    tq = ATTN_Q_TILE
    nk = t_loc + n_cache
    rows = t_loc // GRID_W
    qrows = tq // GRID_W

    def place(a):
        if not gqa:
            return a
        half = (j // 2) % 2
        lane_half = (lax.broadcasted_iota(jnp.int32, a.shape, 1) >= HEAD_DIM).astype(jnp.int32)
        return jnp.where(lane_half == half, a, pltpu.roll(a, HEAD_DIM, 1))

    def put_v(lo, hi, a):
        if mode == "pair":
            ln = lax.broadcasted_iota(jnp.int32, a.shape, 1)
            v_scr[0][lo:hi, :] = jnp.where(ln < HEAD_DIM, a, 1.0).astype(BF16)
            v_scr[1][lo:hi, :] = jnp.where(ln >= HEAD_DIM, a, 1.0).astype(BF16)
        else:
            v_scr[0][lo:hi, :] = a.astype(BF16)

    @pl.when(qb == 0)
    def _():
        kf[0:t_loc, :] = place(k_ref[...].astype(F32)).astype(BF16)
        put_v(0, t_loc, place(v_ref[...].astype(F32)))
        if n_cache:
            kf[t_loc:nk, :] = place(ck_ref[0].astype(F32)).astype(BF16)
            put_v(t_loc, nk, place(cv_ref[0].astype(F32)))

    lane = lax.broadcasted_iota(jnp.int32, (tq, LANES), 1)
    if mode == "diff":
        lp = lam_ref[...]
        lam = (jnp.exp(jnp.sum(lp[0:1] * lp[1:2], axis=1, keepdims=True))
               - jnp.exp(jnp.sum(lp[2:3] * lp[3:4], axis=1, keepdims=True)) + lam_init)

    def chain(g, hh, q2, sl, k_loc, keep, ws_row, done):
        in_half = (lane < HEAD_DIM) if hh == 0 else (lane >= HEAD_DIM)
        qh = jnp.where(in_half, q2, 0.0).astype(BF16)
        s = _dot_nt(qh, k_loc)
        if n_cache:
            s_c = _dot_nt(qh, kf[t_loc:nk, :])
        yield
        if natten:
            blocks = []
            for a in range(qrows):
                first = ws_row - (g * qrows + a) + NA_ROWS
                blocks.append(jnp.concatenate(
                    [bp_ref[hh, pl.ds(jnp.clip(first + 2 * p, 0, 2 * NA_ROWS - 1), 1)][0]
                     for p in range(win // LANES)], axis=1))
            s = s + jnp.concatenate(blocks, axis=0)
        if keep is not None:
            s = jnp.where(keep, s, NEG_INF)
        m = jnp.max(s, axis=1, keepdims=True)
        if n_cache:
            m = jnp.maximum(m, jnp.max(s_c, axis=1, keepdims=True))
        if has_sink:
            sink = sink_ref[2 * j + hh] * LOG2E
            m = jnp.maximum(m, sink)
        e = jnp.exp2(s - m)
        e_c = jnp.exp2(s_c - m) if n_cache else None
        yield
        if mode == "pair":
            o = _dot(e.astype(BF16), v_scr[hh][sl, :])
            if n_cache:
                o = o + _dot(e_c.astype(BF16), v_scr[hh][t_loc:nk, :])
            yield
            den = pltpu.roll(o, HEAD_DIM, 1)
            if has_sink:
                den = den + jnp.exp2(sink - m)
            done[hh] = o * (1.0 / den)
        else:
            den = jnp.sum(e, axis=1, keepdims=True)
            if n_cache:
                den = den + jnp.sum(e_c, axis=1, keepdims=True)
            done[hh] = (e, e_c, 1.0 / den)

    def sub_tile(t):
        g = qb * n_sub + t
        q2 = q_ref[t * tq:(t + 1) * tq, :].astype(F32) * (QK_SCALE * LOG2E)
        keep = ws_row = None
        if natten:
            ws_row = jnp.clip(g * qrows - NA_ROWS // 2, 0, rows - NA_KEY_ROWS)
            sl = pl.ds(pl.multiple_of(ws_row * GRID_W, GRID_W), win)
            q_row = lax.broadcasted_iota(jnp.int32, (tq, 1), 0) // GRID_W + g * qrows
            r0 = jnp.clip(q_row - NA_ROWS // 2, 0, rows - NA_ROWS)
            k_row = lax.broadcasted_iota(jnp.int32, (1, win), 1) // GRID_W + ws_row
            keep = (k_row >= r0) & (k_row < r0 + NA_ROWS)
        elif band:
            ws = pl.multiple_of(jnp.clip(g * tq - WINDOW, 0, t_loc - win), WINDOW)
            sl = pl.ds(ws, win)
            qpos = g * tq + lax.broadcasted_iota(jnp.int32, (tq, 1), 0)
            kpos = ws + lax.broadcasted_iota(jnp.int32, (1, win), 1)
            keep = jnp.abs(qpos - kpos) <= WINDOW
        else:
            sl = slice(0, win)
        k_loc = kf[sl, :]
        done = [None, None]
        yield from _in_waves([chain(g, hh, q2, sl, k_loc, keep, ws_row, done) for hh in range(2)])
        if mode == "pair":
            out = jnp.where(lane < HEAD_DIM, done[0], done[1])
        else:
            (e0, ec0, w0), (e1, ec1, w1) = done
            c = lam * w1 / w0
            out = _dot((e0 - e1 * c).astype(BF16), v_scr[0][sl, :])
            if n_cache:
                out = out + _dot((ec0 - ec1 * c).astype(BF16), v_scr[0][t_loc:nk, :])
            yield
            out = out * w0
            ms = jnp.mean(out * out, axis=-1, keepdims=True)
            out = out * lax.rsqrt(ms + EPS) * subln_ref[...] * (1.0 - lam_init)
        o_ref[t * tq:(t + 1) * tq, :] = out.astype(o_ref.dtype)

    yield from _in_waves([sub_tile(t) for t in range(n_sub)])


def attention(qkv, nb, t_loc, *, mode, gqa, k_col, v_col, n_sub, n_lb, cache=None, band=False, bias_pairs=None,
              sink=None, lam=None, subln=None, lam_init=0.0, d_model):
    tq = ATTN_Q_TILE * n_sub
    n_qlb = d_model // LANES
    natten = bias_pairs is not None
    n_cache = 0 if cache is None else cache[0].shape[1]
    if natten:
        win = NA_KEY_ROWS * GRID_W
    elif band:
        win = ATTN_Q_TILE + 2 * WINDOW
    else:
        win = t_loc
    if gqa:
        assert n_lb == 1 or n_lb % 4 == 0
        k_lb = max(n_lb // 4, 1)
        kgrp = (lambda jg: jg // 4) if n_lb == 1 else (lambda jg: jg)
        kmap = lambda jl: jl // 4
    else:
        k_lb = n_lb
        kgrp = lambda jg: jg
        kmap = lambda jl: jl
    qw, kw = n_lb * LANES, k_lb * LANES
    assert k_col % kw == 0 and v_col % kw == 0 and n_qlb % n_lb == 0
    kc, vc = k_col // kw, v_col // kw
    nqb = t_loc // tq
    nk = t_loc + n_cache

    in_specs = []
    args = []
    if sink is not None:
        in_specs.append(pl.BlockSpec(memory_space=pltpu.SMEM))
        args.append(sink)
    in_specs += [pl.BlockSpec((tq, qw), lambda b, j, i: (b * nqb + i, j)),
                 pl.BlockSpec((t_loc, kw), lambda b, j, i: (b, kc + kgrp(j))),
                 pl.BlockSpec((t_loc, kw), lambda b, j, i: (b, vc + kgrp(j)))]
    args += [qkv, qkv, qkv]
    if cache is not None:
        in_specs += [pl.BlockSpec((1, n_cache, kw), lambda b, j, i: (b, 0, kgrp(j))),
                     pl.BlockSpec((1, n_cache, kw), lambda b, j, i: (b, 0, kgrp(j)))]
        args += list(cache)
    if natten:
        in_specs.append(pl.BlockSpec((2 * n_lb, 2 * NA_ROWS, GRID_W, LANES), lambda b, j, i: (j, 0, 0, 0)))
        args.append(bias_pairs)
    if mode == "diff":
        in_specs += [pl.BlockSpec((8, LANES), lambda b, j, i: (0, 0)),
                     pl.BlockSpec((1, LANES), lambda b, j, i: (0, 0))]
        args += [lam, subln]
    kern = functools.partial(
        _attn_kernel, n_lb=n_lb, kmap=kmap, mode=mode, gqa=gqa, t_loc=t_loc, n_cache=n_cache, win=win, band=band,
        natten=natten, has_sink=sink is not None, lam_init=lam_init, n_sub=n_sub)
    n_v = 2 if mode == "pair" else 1
    return pl.pallas_call(
        kern,
        grid=(nb, n_qlb // n_lb, nqb),
        in_specs=in_specs,
        out_specs=pl.BlockSpec((tq, qw), lambda b, j, i: (b * nqb + i, j)),
        out_shape=jax.ShapeDtypeStruct((nb * t_loc, d_model), BF16),
        scratch_shapes=[pltpu.VMEM((n_lb, nk, LANES), BF16) for _ in range(1 + n_v)],
        compiler_params=_cparams(("parallel", "arbitrary", "arbitrary")),
    )(*args)


def _natten_pair_table(rpb):
    h, n_dr, n_dc = rpb.shape
    pad = GRID_W - NA_COLS
    ext = jnp.concatenate([jnp.repeat(rpb[..., :1], pad, axis=-1), rpb,
                           jnp.repeat(rpb[..., -1:], pad + 1, axis=-1)], axis=-1).astype(F32) * LOG2E
    neg = jnp.full((h, 1, LANES), NEG_INF, F32)
    ext = jnp.concatenate([neg, ext, neg], axis=1)
    return pl.pallas_call(
        _pair_table_kernel,
        grid=(h,),
        in_specs=[pl.BlockSpec((1, n_dr + 2, LANES), lambda i: (i, 0, 0))],
        out_specs=pl.BlockSpec((1, n_dr + 1, GRID_W, LANES), lambda i: (i, 0, 0, 0)),
        out_shape=jax.ShapeDtypeStruct((h, n_dr + 1, GRID_W, LANES), F32),
        compiler_params=_cparams(("parallel",)),
    )(ext)


def _pair_table_kernel(ext_ref, o_ref):
    qc = lax.broadcasted_iota(jnp.int32, (GRID_W, LANES), 0)
    ln = lax.broadcasted_iota(jnp.int32, (GRID_W, LANES), 1)
    kc = ln % GRID_W
    cstart = jnp.clip(qc - NA_COLS // 2, 0, GRID_W - NA_COLS)
    in_window = (kc >= cstart) & (kc < cstart + NA_COLS)
    for d in range(o_ref.shape[1]):
        lo = jnp.broadcast_to(ext_ref[0, d:d + 1, :], (GRID_W, LANES))
        hi = jnp.broadcast_to(ext_ref[0, d + 1:d + 2, :], (GRID_W, LANES))
        lo = pltpu.roll(lo, GRID_W + 1, 1, stride=1, stride_axis=0)
        hi = pltpu.roll(hi, 1, 1, stride=1, stride_axis=0)
        o_ref[0, d] = jnp.where(in_window, jnp.where(ln < GRID_W, lo, hi), NEG_INF)


def _mixer_residual(a_ref, wo_ref, x_ref, mod_ref):
    return x_ref[...] + mod_ref[0, 2:3, :] * _dot(a_ref[...], wo_ref[...])


def _ffn_kernel(a_ref, wo_ref, x_ref, mod_ref, g_ref, wg_ref, wu_ref, wd_ref, o_ref):
    x1 = _mixer_residual(a_ref, wo_ref, x_ref, mod_ref)
    h = _norm_mod(x1, g_ref[...], mod_ref[0, 3:4, :], mod_ref[0, 4:5, :]).astype(BF16)
    t = (_silu(_dot(h, wg_ref[...])) * _dot(h, wu_ref[...])).astype(BF16)
    o_ref[...] = x1 + mod_ref[0, 5:6, :] * _dot(t, wd_ref[...])


def dense_ffn(a, wo, x, mod, mod_of_block, g, wg, wu, wd):
    r, d = x.shape
    ff = wg.shape[1]
    tm = ROW_TILE

    def resident(shape):
        return pl.BlockSpec(shape, lambda i: (0, 0), pipeline_mode=pl.Buffered(1))

    return pl.pallas_call(
        _ffn_kernel,
        grid=(r // tm,),
        in_specs=[pl.BlockSpec((tm, d), lambda i: (i, 0)),
                  resident((d, d)),
                  pl.BlockSpec((tm, d), lambda i: (i, 0)),
                  pl.BlockSpec((1, 6, d), lambda i: (mod_of_block(i), 0, 0)),
                  pl.BlockSpec((1, d), lambda i: (0, 0)),
                  resident((d, ff)),
                  resident((d, ff)),
                  resident((ff, d))],
        out_specs=pl.BlockSpec((tm, d), lambda i: (i, 0)),
        out_shape=jax.ShapeDtypeStruct((r, d), F32),
        compiler_params=_cparams(("parallel",)),
    )(a, wo, x, mod, g.reshape(1, d), wg, wu, wd)


def _router_kernel(a_ref, wo_ref, x_ref, mod_ref, g_ref, wr_ref,
                   x1_ref, h_ref, rank_row_ref, gate_row_ref, rank_col_ref, cnt_ref):
    x1 = _mixer_residual(a_ref, wo_ref, x_ref, mod_ref)
    x1_ref[...] = x1
    h = _norm_mod(x1, g_ref[...], mod_ref[0, 3:4, :], mod_ref[0, 4:5, :])
    h_hi = h.astype(BF16)
    h_ref[...] = h_hi
    h_lo = (h - h_hi.astype(F32)).astype(BF16)
    wr = wr_ref[...]
    w_hi = wr.astype(BF16)
    w_lo = (wr - w_hi.astype(F32)).astype(BF16)
    hh = _dot(h_hi, jnp.concatenate([w_hi, w_lo], axis=1))
    logits = hh[:, :LANES] + (_dot(h_lo, w_hi) + hh[:, LANES:])
    tb = logits.shape[0]
    lane = lax.broadcasted_iota(jnp.int32, logits.shape, 1)
    lane_f = lane.astype(F32)
    logits = jnp.where(lane < N_EXPERTS, logits, -jnp.inf)
    m1 = jnp.max(logits, axis=1, keepdims=True)
    i1 = jnp.min(jnp.where(logits == m1, lane_f, float(LANES)), axis=1, keepdims=True)
    rest = jnp.where(lane_f == i1, -jnp.inf, logits)
    m2 = jnp.max(rest, axis=1, keepdims=True)
    i2 = jnp.min(jnp.where(rest == m2, lane_f, float(LANES)), axis=1, keepdims=True)
    e2 = jnp.exp(m2 - m1)
    inv = 1.0 / (1.0 + e2)
    is1 = lane_f == i1
    is2 = lane_f == i2
    gates = jnp.where(is1, inv, 0.0) + jnp.where(is2, e2 * inv, 0.0)
    sel = jnp.where(is1 | is2, 1.0, 0.0)
    sel_b = sel.astype(BF16)
    chunk = 256
    parts = []
    for c in range(tb // chunk):
        ri = lax.broadcasted_iota(jnp.int32, (chunk, tb), 0) + c * chunk
        ci = lax.broadcasted_iota(jnp.int32, (chunk, tb), 1)
        parts.append(_dot(jnp.where(ci < ri, 1.0, 0.0).astype(BF16), sel_b))
    rank = jnp.concatenate(parts, axis=0)
    rank_sel = jnp.where(sel > 0.5, rank, -1.0)
    rank_col_ref[...] = rank_sel
    rank_row_ref[0] = rank_sel.T[:N_EXPERTS]
    gate_row_ref[0] = gates.T[:N_EXPERTS]
    cnt_ref[0] = jnp.broadcast_to(jnp.sum(sel, axis=0, keepdims=True), (8, LANES)).astype(jnp.int32)


def _expert_kernel(cnt_ref, h_ref, rank_row_ref, gate_row_ref, rank_col_ref, wg_ref, wu_ref, wd_ref,
                   x_ref, mod_ref, o_ref):
    b, e = pl.program_id(0), pl.program_id(1)
    tb = h_ref.shape[0]
    n_rows = cnt_ref[b * N_EXPERTS + e]

    @pl.when(e == 0)
    def _():
        o_ref[...] = jnp.zeros_like(o_ref)

    rank_e = rank_row_ref[0, pl.ds(e, 1), :]
    gate_e = gate_row_ref[0, pl.ds(e, 1), :]
    lane = lax.broadcasted_iota(jnp.int32, (tb, LANES), 1)
    rank_c = jnp.sum(jnp.where(lane == e, rank_col_ref[...], 0.0), axis=1, keepdims=True)

    def tile(base, m):
        want = (lax.broadcasted_iota(jnp.int32, (m, tb), 0) + base).astype(F32)
        p = jnp.where(rank_e == want, 1.0, 0.0)
        xr = _dot(p.astype(BF16), h_ref[...]).astype(BF16)
        gate = jnp.sum(p * gate_e, axis=1, keepdims=True)
        t = (_silu(_dot(xr, wg_ref[0])) * _dot(xr, wu_ref[0])).astype(BF16)
        yg = (_dot(t, wd_ref[0]) * gate).astype(BF16)
        want_t = (lax.broadcasted_iota(jnp.int32, (tb, m), 1) + base).astype(F32)
        pt = jnp.where(rank_c == want_t, 1.0, 0.0).astype(BF16)
        o_ref[...] += _dot(pt, yg)

    n_full = n_rows // MOE_SUB
    rest = n_rows - n_full * MOE_SUB

    def full_tile(i, carry):
        tile(pl.multiple_of(i * MOE_SUB, MOE_SUB), MOE_SUB)
        return carry

    lax.fori_loop(0, n_full, full_tile, 0)
    tail_base = pl.multiple_of(n_full * MOE_SUB, MOE_SUB)
    pl.when(rest > MOE_SUB // 2)(lambda: tile(tail_base, MOE_SUB))
    pl.when((rest > 0) & (rest <= MOE_SUB // 2))(lambda: tile(tail_base, MOE_SUB // 2))

    @pl.when(e == N_EXPERTS - 1)
    def _():
        o_ref[...] = x_ref[...] + mod_ref[0, 5:6, :] * o_ref[...]


def moe_ffn(a, wo, x, mod, mod_of_block, g, w_router, wg, wu, wd):
    r, d = x.shape
    ff = wg.shape[2]
    tb = MOE_BLOCK
    nb = r // tb
    wr = jnp.pad(w_router, ((0, 0), (0, LANES - N_EXPERTS)))
    x, h, rank_row, gate_row, rank_col, cnt = pl.pallas_call(
        _router_kernel,
        grid=(nb,),
        in_specs=[pl.BlockSpec((tb, d), lambda i: (i, 0)),
                  pl.BlockSpec((d, d), lambda i: (0, 0)),
                  pl.BlockSpec((tb, d), lambda i: (i, 0)),
                  pl.BlockSpec((1, 6, d), lambda i: (mod_of_block(i), 0, 0)),
                  pl.BlockSpec((1, d), lambda i: (0, 0)),
                  pl.BlockSpec((d, LANES), lambda i: (0, 0))],
        out_specs=[pl.BlockSpec((tb, d), lambda i: (i, 0)),
                   pl.BlockSpec((tb, d), lambda i: (i, 0)),
                   pl.BlockSpec((1, N_EXPERTS, tb), lambda i: (i, 0, 0)),
                   pl.BlockSpec((1, N_EXPERTS, tb), lambda i: (i, 0, 0)),
                   pl.BlockSpec((tb, LANES), lambda i: (i, 0)),
                   pl.BlockSpec((1, 8, LANES), lambda i: (i, 0, 0))],
        out_shape=[jax.ShapeDtypeStruct((r, d), F32),
                   jax.ShapeDtypeStruct((r, d), BF16),
                   jax.ShapeDtypeStruct((nb, N_EXPERTS, tb), F32),
                   jax.ShapeDtypeStruct((nb, N_EXPERTS, tb), F32),
                   jax.ShapeDtypeStruct((r, LANES), F32),
                   jax.ShapeDtypeStruct((nb, 8, LANES), jnp.int32)],
        compiler_params=_cparams(("parallel",)),
    )(a, wo, x, mod, g.reshape(1, d), wr)
    counts = cnt[:, 0, :N_EXPERTS].reshape(nb * N_EXPERTS)

    grid_spec = pltpu.PrefetchScalarGridSpec(
        num_scalar_prefetch=1,
        grid=(nb, N_EXPERTS),
        in_specs=[pl.BlockSpec((tb, d), lambda i, e, c: (i, 0)),
                  pl.BlockSpec((1, N_EXPERTS, tb), lambda i, e, c: (i, 0, 0)),
                  pl.BlockSpec((1, N_EXPERTS, tb), lambda i, e, c: (i, 0, 0)),
                  pl.BlockSpec((tb, LANES), lambda i, e, c: (i, 0)),
                  pl.BlockSpec((1, d, ff), lambda i, e, c: (e, 0, 0)),
                  pl.BlockSpec((1, d, ff), lambda i, e, c: (e, 0, 0)),
                  pl.BlockSpec((1, ff, d), lambda i, e, c: (e, 0, 0)),
                  pl.BlockSpec((tb, d), lambda i, e, c: (i, 0)),
                  pl.BlockSpec((1, 6, d), lambda i, e, c: (mod_of_block(i), 0, 0))],
        out_specs=pl.BlockSpec((tb, d), lambda i, e, c: (i, 0)),
    )
    return pl.pallas_call(
        _expert_kernel,
        grid_spec=grid_spec,
        out_shape=jax.ShapeDtypeStruct((r, d), F32),
        compiler_params=_cparams(("parallel", "arbitrary")),
    )(counts, h, rank_row, gate_row, rank_col, wg, wu, wd, x, mod)


def _diff_lambda_init(layer):
    return 0.8 - 0.6 * math.exp(-0.3 * layer)


def _head_gain_row(q_norm, k_norm, n_q, n_k):
    return jnp.concatenate([jnp.tile(q_norm.astype(F32), n_q // HEAD_DIM),
                            jnp.tile(k_norm.astype(F32), n_k // HEAD_DIM)]).reshape(1, -1)


def kernel(x_prompt, x_sample, cache_k_0, cache_v_0, cache_k_1, cache_v_1, cache_k_2, cache_v_2, cache_k_3, cache_v_3, c, c_ctx, norm1_0, w_ada_0, b_ada_0, w_qkv_0, q_norm_0, k_norm_0, rpb_0, w_o_0, norm2_0, w_gate_0, w_up_0, w_down_0, norm1_1, w_ada_1, b_ada_1, w_qkv_1, q_norm_1, k_norm_1, lam_q1_1, lam_k1_1, lam_q2_1, lam_k2_1, subln_1, w_o_1, norm2_1, w_router_1, w_egate_1, w_eup_1, w_edown_1, norm1_2, w_ada_2, b_ada_2, w_qkv_2, q_norm_2, k_norm_2, sink_2, w_o_2, norm2_2, w_gate_2, w_up_2, w_down_2, norm1_3, w_ada_3, b_ada_3, w_qkv_3, q_norm_3, k_norm_3, w_o_3, norm2_3, w_router_3, w_egate_3, w_eup_3, w_edown_3):
    nbc, tc, d = x_prompt.shape
    nbl, tl, _ = x_sample.shape
    n_past = cache_k_0.shape[1]
    assert d % (2 * LANES) == 0 and tc % ATTN_Q_TILE == 0 and tl % MOE_BLOCK == 0
    assert (nbc * tc) % MOE_BLOCK == 0 and tl // GRID_W >= NA_KEY_ROWS and nbl <= 7

    norm1 = (norm1_0, norm1_1, norm1_2, norm1_3)
    norm2 = (norm2_0, norm2_1, norm2_2, norm2_3)
    ada_p = ((w_ada_0, b_ada_0), (w_ada_1, b_ada_1), (w_ada_2, b_ada_2), (w_ada_3, b_ada_3))
    w_qkv = (w_qkv_0, w_qkv_1, w_qkv_2, w_qkv_3)
    qk_norm = ((q_norm_0, k_norm_0), (q_norm_1, k_norm_1), (q_norm_2, k_norm_2), (q_norm_3, k_norm_3))
    w_out = (w_o_0, w_o_1, w_o_2, w_o_3)
    ffn_p = ((w_gate_0, w_up_0, w_down_0), (w_router_1, w_egate_1, w_eup_1, w_edown_1),
             (w_gate_2, w_up_2, w_down_2), (w_router_3, w_egate_3, w_eup_3, w_edown_3))
    caches = ((cache_k_0, cache_v_0), (cache_k_1, cache_v_1), (cache_k_2, cache_v_2), (cache_k_3, cache_v_3))

    xc = x_prompt.reshape(nbc * tc, d)
    xl = x_sample.reshape(nbl * tl, d)
    cond = jnp.concatenate([c_ctx[None, :], c, jnp.zeros((7 - nbl, d), F32)], axis=0)
    cond16 = jnp.concatenate([cond, cond], axis=0)

    def ctx_mod(blk_rows):
        return lambda i: 0

    def lat_mod(blk_rows):
        per = tl // blk_rows
        return lambda i: 1 + i // per

    lam_rows = jnp.zeros((8, LANES), F32)
    for r_, v_ in enumerate((lam_q1_1, lam_k1_1, lam_q2_1, lam_k2_1)):
        lam_rows = lam_rows.at[r_, :HEAD_DIM].set(v_.astype(F32))
    bias_pairs = _natten_pair_table(rpb_0)
    rope_tabs = _rope_tables(tl)

    states = []
    for l in range(4):
        mixer = l % 4
        mod = ada_mod(cond16, *ada_p[l])
        wq = w_qkv[l].astype(BF16)
        n = wq.shape[1]
        n_q = d
        n_k = d if mixer < 2 else (n - d) // 2
        n_v = n - n_q - n_k
        hn = _head_gain_row(*qk_norm[l], n_q, n_k)
        rope = None if mixer == 0 else rope_tabs

        qkv_c = qkv_proj(xc, mod, ctx_mod(ROW_TILE), norm1[l], wq, hn, None, F32)
        qkv_l = qkv_proj(xl, mod, lat_mod(ROW_TILE), norm1[l], wq, hn, rope, BF16)
        states.append((qkv_c[:, n_q:n_q + n_k], qkv_c[:, n_q + n_k:]))

        ck, cv = caches[l]
        cache = (ck.astype(BF16).reshape(nbl, n_past, n_k), cv.astype(BF16).reshape(nbl, n_past, n_v))
        common = dict(k_col=n_q, v_col=n_q + n_k, d_model=d)
        ctx = dict(n_sub=1, n_lb=CTX_LANE_BLOCKS, **common)
        lat = dict(n_sub=4, n_lb=1, cache=cache, **common)
        if mixer == 0:
            o_c = attention(qkv_c, nbc, tc, mode="pair", gqa=False, **ctx)
            o_l = attention(qkv_l, nbl, tl, mode="pair", gqa=False, bias_pairs=bias_pairs, **lat)
        elif mixer == 1:
            diff = dict(mode="diff", gqa=False, lam=lam_rows, subln=subln_1.astype(F32).reshape(1, LANES),
                        lam_init=_diff_lambda_init(l))
            o_c = attention(qkv_c, nbc, tc, **diff, **ctx)
            o_l = attention(qkv_l, nbl, tl, **diff, **lat)
        elif mixer == 2:
            sk = sink_2.astype(F32)
            o_c = attention(qkv_c, nbc, tc, mode="pair", gqa=True, sink=sk, **ctx)
            o_l = attention(qkv_l, nbl, tl, mode="pair", gqa=True, band=True, sink=sk, **lat)
        else:
            o_c = attention(qkv_c, nbc, tc, mode="pair", gqa=True, **ctx)
            o_l = attention(qkv_l, nbl, tl, mode="pair", gqa=True, **lat)

        wo = w_out[l].astype(BF16)
        if l % 2 == 0:
            wg, wu, wd = (w.astype(BF16) for w in ffn_p[l])
            xc = dense_ffn(o_c, wo, xc, mod, ctx_mod(ROW_TILE), norm2[l], wg, wu, wd)
            xl = dense_ffn(o_l, wo, xl, mod, lat_mod(ROW_TILE), norm2[l], wg, wu, wd)
        else:
            wr = ffn_p[l][0]
            wg, wu, wd = (w.astype(BF16) for w in ffn_p[l][1:])
            xc = moe_ffn(o_c, wo, xc, mod, ctx_mod(MOE_BLOCK), norm2[l], wr, wg, wu, wd)
            xl = moe_ffn(o_l, wo, xl, mod, lat_mod(MOE_BLOCK), norm2[l], wr, wg, wu, wd)

    nh = d // HEAD_DIM
    k0, v0 = states[0]
    k1, v1 = states[1]
    k2, v2 = states[2]
    k3, v3 = states[3]
    nkv = k2.shape[1] // HEAD_DIM
    return (xc.reshape(nbc, tc, d), xl.reshape(nbl, tl, d),
            k0.reshape(nbc, tc, nh, HEAD_DIM), v0.reshape(nbc, tc, nh, HEAD_DIM),
            k1.reshape(nbc, tc, nh // 2, 2, HEAD_DIM), v1.reshape(nbc, tc, nh // 2, 2 * HEAD_DIM),
            k2.reshape(nbc, tc, nkv, HEAD_DIM), v2.reshape(nbc, tc, nkv, HEAD_DIM),
            k3.reshape(nbc, tc, nkv, HEAD_DIM), v3.reshape(nbc, tc, nkv, HEAD_DIM))
```

```python
import functools
import math

import numpy as np
import jax
import jax.numpy as jnp
from jax import lax
from jax.experimental import pallas as pl
from jax.experimental.pallas import tpu as pltpu

F32 = jnp.float32
BF16 = jnp.bfloat16

HEAD_DIM = 64
GRID_W = 64
NA_ROWS = 8
NA_COLS = 16
WINDOW = 128
ROPE_BASE = 10000.0
ROPE_PAIRS_AXIS = HEAD_DIM // 4
N_EXPERTS = 8
EPS = 1e-6
NEG_INF = -1e30
QK_SCALE = HEAD_DIM ** -0.5
LOG2E = math.log2(math.e)

LANES = 128
V7X_VMEM_BYTES = 64 * 1024 * 1024
VMEM_LIMIT = V7X_VMEM_BYTES - 8 * 1024 * 1024

ROW_TILE = 512
ATTN_Q_TILE = 256
NA_KEY_ROWS = 12
CTX_LANE_BLOCKS = 8
MOE_BLOCK = 1024
MOE_SUB = 256


def _cparams(sem):
    return pltpu.CompilerParams(dimension_semantics=sem, vmem_limit_bytes=VMEM_LIMIT)


def _silu(x):
    return x * (1.0 / (1.0 + jnp.exp(-x)))


def _norm_mod(x, g, shift, scale):
    ms = jnp.mean(x * x, axis=-1, keepdims=True)
    y = x * lax.rsqrt(ms + EPS) * g
    return y * (1.0 + scale) + shift


def _dot(a, b):
    return jnp.dot(a, b, preferred_element_type=F32)


def _dot_nt(a, b):
    return lax.dot_general(a, b, (((1,), (1,)), ((), ())), preferred_element_type=F32)


def _ada_kernel(c_ref, w_ref, b_ref, o_ref):
    a = _silu(c_ref[...])
    a_hi = a.astype(BF16)
    a_lo = a - a_hi.astype(F32)
    row = lax.broadcasted_iota(jnp.int32, a.shape, 0)
    lhs = jnp.where(row < 8, a_hi.astype(F32), a_lo).astype(BF16)
    w = w_ref[...]
    w_hi = w.astype(BF16)
    w_lo = (w - w_hi.astype(F32)).astype(BF16)
    r = _dot(lhs, w_hi)
    r2 = _dot(a_hi, w_lo)
    o_ref[...] = r[:8] + r[8:] + r2[:8] + b_ref[...]


def ada_mod(cond16, w_ada, b_ada):
    d, n = w_ada.shape
    tn = 1024
    out = pl.pallas_call(
        _ada_kernel,
        grid=(n // tn,),
        in_specs=[pl.BlockSpec((16, d), lambda j: (0, 0)),
                  pl.BlockSpec((d, tn), lambda j: (0, j)),
                  pl.BlockSpec((1, tn), lambda j: (0, j))],
        out_specs=pl.BlockSpec((8, tn), lambda j: (0, j)),
        out_shape=jax.ShapeDtypeStruct((8, n), F32),
        compiler_params=_cparams(("arbitrary",)),
    )(cond16, w_ada, b_ada.reshape(1, n))
    return out.reshape(8, 6, d)


def _qkv_kernel(*refs, rope, n_norm, n_out):
    o_refs = refs[len(refs) - n_out:]
    if rope:
        x_ref, mod_ref, g_ref, w_ref, seg_ref, hn_ref, cos_ref, sin_ref = refs[:-n_out]
    else:
        x_ref, mod_ref, g_ref, w_ref, seg_ref, hn_ref = refs[:-n_out]
    h = _norm_mod(x_ref[...], g_ref[...], mod_ref[0, 0:1, :], mod_ref[0, 1:2, :]).astype(BF16)
    acc = _dot(h, w_ref[...])
    qk = acc[:, :n_norm]
    sq = (qk * qk).astype(BF16)
    seg = seg_ref[...]
    ssum = jnp.concatenate(
        [_dot(sq[:, c * 2 * LANES:(c + 1) * 2 * LANES], seg) for c in range(n_norm // (2 * LANES))], axis=1)
    y = qk * lax.rsqrt(ssum * (1.0 / HEAD_DIM) + EPS) * hn_ref[...]
    if rope:
        lane = lax.broadcasted_iota(jnp.int32, (acc.shape[0], LANES), 1)
        even = (lane % 2) == 0
        cos, sin = cos_ref[...], sin_ref[...]
        parts = []
        for c in range(n_norm // LANES):
            yc = y[:, c * LANES:(c + 1) * LANES]
            nxt = pltpu.roll(yc, LANES - 1, 1)
            prv = pltpu.roll(yc, 1, 1)
            parts.append(yc * cos + jnp.where(even, nxt, prv) * sin)
        y = jnp.concatenate(parts, axis=1)
    if n_out == 1:
        o_ref, = o_refs
        o_ref[:, :n_norm] = y.astype(o_ref.dtype)
        o_ref[:, n_norm:] = acc[:, n_norm:].astype(o_ref.dtype)
    else:
        q_ref, k_ref, v_ref = o_refs
        n_q = q_ref.shape[1]
        q_ref[...] = y[:, :n_q].astype(q_ref.dtype)
        k_ref[...] = y[:, n_q:].astype(k_ref.dtype)
        v_ref[...] = acc[:, n_norm:].astype(v_ref.dtype)


def _seg_matrix():
    i = np.arange(2 * LANES)
    return jnp.asarray((i[:, None] // HEAD_DIM) == (i[None, :] // HEAD_DIM), BF16)


def qkv_proj(x, mod, mod_of_block, g, w, hn, rope_tabs, out_dtype, split=None):
    r, d = x.shape
    n = w.shape[1]
    n_norm = hn.shape[1]
    tm = ROW_TILE
    rope = rope_tabs is not None
    in_specs = [pl.BlockSpec((tm, d), lambda i: (i, 0)),
                pl.BlockSpec((1, 6, d), lambda i: (mod_of_block(i), 0, 0)),
                pl.BlockSpec((1, d), lambda i: (0, 0)),
                pl.BlockSpec((d, n), lambda i: (0, 0)),
                pl.BlockSpec((2 * LANES, 2 * LANES), lambda i: (0, 0)),
                pl.BlockSpec((1, n_norm), lambda i: (0, 0))]
    args = [x, mod, g.reshape(1, d), w, _seg_matrix(), hn]
    if rope:
        cos_t, sin_t = rope_tabs
        nblk = cos_t.shape[0] // tm
        in_specs += [pl.BlockSpec((tm, LANES), lambda i: (i % nblk, 0)),
                     pl.BlockSpec((tm, LANES), lambda i: (i % nblk, 0))]
        args += [cos_t, sin_t]
    widths = (n,) if split is None else split
    outs = pl.pallas_call(
        functools.partial(_qkv_kernel, rope=rope, n_norm=n_norm, n_out=len(widths)),
        grid=(r // tm,),
        in_specs=in_specs,
        out_specs=[pl.BlockSpec((tm, wd), lambda i: (i, 0)) for wd in widths],
        out_shape=[jax.ShapeDtypeStruct((r, wd), out_dtype) for wd in widths],
        compiler_params=_cparams(("parallel",)),
    )(*args)
    return outs[0] if split is None else tuple(outs)


def _rope_tables(t):
    pos = np.arange(t)
    row = (pos // GRID_W).astype(np.float32)
    col = (pos % GRID_W).astype(np.float32)
    freqs = jnp.asarray(ROPE_BASE, F32) ** (-jnp.arange(ROPE_PAIRS_AXIS, dtype=F32) / ROPE_PAIRS_AXIS)
    ang = jnp.concatenate([jnp.asarray(row)[:, None] * freqs, jnp.asarray(col)[:, None] * freqs], axis=-1)
    cos = jnp.repeat(jnp.cos(ang), 2, axis=-1)
    sin = jnp.repeat(jnp.sin(ang), 2, axis=-1)
    sign = jnp.asarray(np.where(np.arange(HEAD_DIM) % 2 == 0, -1.0, 1.0), F32)
    reps = LANES // HEAD_DIM
    return jnp.tile(cos, (1, reps)), jnp.tile(sin * sign, (1, reps))


def _in_waves(gens):
    while gens:
        gens = [g for g in gens if next(g, StopIteration) is not StopIteration]
        yield


def _attn_kernel(*refs, n_lb, kmap, mode, n_cache, natten, has_sink, **static):
    it = iter(refs)
    sink_ref = next(it) if has_sink else None
    q_ref, k_ref, v_ref = next(it), next(it), next(it)
    ck_ref, cv_ref = (next(it), next(it)) if n_cache else (None, None)
    bp_ref = next(it) if natten else None
    lam_refs = [next(it), next(it)] if mode == "diff" else []
    o_ref, kf = next(it), next(it)
    v_scr = list(it)
    gens = []
    for jl in range(n_lb):
        q_cols = pl.ds(jl * LANES, LANES)
        k_cols = pl.ds(kmap(jl) * LANES, LANES)
        views = [sink_ref] if has_sink else []
        views += [q_ref.at[:, q_cols], k_ref.at[:, k_cols], v_ref.at[:, k_cols]]
        if n_cache:
            views += [ck_ref.at[:, :, k_cols], cv_ref.at[:, :, k_cols]]
        if natten:
            views.append(bp_ref.at[pl.ds(2 * jl, 2)])
        views += lam_refs + [o_ref.at[:, q_cols], kf.at[jl]] + [v.at[jl] for v in v_scr]
        gens.append(_attn_lane_block(views, pl.program_id(1) * n_lb + jl, pl.program_id(2), mode=mode,
                                     n_cache=n_cache, natten=natten, has_sink=has_sink, **static))
    for _ in _in_waves(gens):
        pass


def _attn_lane_block(refs, j, qb, *, mode, gqa, t_loc, n_cache, win, band, natten, has_sink, lam_init, n_sub):
    it = iter(refs)
    sink_ref = next(it) if has_sink else None
    q_ref, k_ref, v_ref = next(it), next(it), next(it)
    ck_ref = cv_ref = bp_ref = lam_ref = subln_ref = None
    if n_cache:
        ck_ref, cv_ref = next(it), next(it)
    if natten:
        bp_ref = next(it)
    if mode == "diff":
        lam_ref, subln_ref = next(it), next(it)
    o_ref, kf = next(it), next(it)
    v_scr = [next(it), next(it)] if mode == "pair" else [next(it)]

    tq = ATTN_Q_TILE
    nk = t_loc + n_cache
    rows = t_loc // GRID_W
    qrows = tq // GRID_W

    def place(a):
        if not gqa:
            return a
        half = (j // 2) % 2
        lane_half = (lax.broadcasted_iota(jnp.int32, a.shape, 1) >= HEAD_DIM).astype(jnp.int32)
        return jnp.where(lane_half == half, a, pltpu.roll(a, HEAD_DIM, 1))

    def put_v(lo, hi, a):
        if mode == "pair":
            ln = lax.broadcasted_iota(jnp.int32, a.shape, 1)
            v_scr[0][lo:hi, :] = jnp.where(ln < HEAD_DIM, a, 1.0).astype(BF16)
            v_scr[1][lo:hi, :] = jnp.where(ln >= HEAD_DIM, a, 1.0).astype(BF16)
        else:
            v_scr[0][lo:hi, :] = a.astype(BF16)

    @pl.when(qb == 0)
    def _():
        kf[0:t_loc, :] = place(k_ref[...].astype(F32)).astype(BF16)
        put_v(0, t_loc, place(v_ref[...].astype(F32)))
        if n_cache:
            kf[t_loc:nk, :] = place(ck_ref[0].astype(F32)).astype(BF16)
            put_v(t_loc, nk, place(cv_ref[0].astype(F32)))

    lane = lax.broadcasted_iota(jnp.int32, (tq, LANES), 1)
    if mode == "diff":
        lp = lam_ref[...]
        lam = (jnp.exp(jnp.sum(lp[0:1] * lp[1:2], axis=1, keepdims=True))
               - jnp.exp(jnp.sum(lp[2:3] * lp[3:4], axis=1, keepdims=True)) + lam_init)

    def chain(g, hh, q2, sl, k_loc, keep, ws_row, done):
        in_half = (lane < HEAD_DIM) if hh == 0 else (lane >= HEAD_DIM)
        qh = jnp.where(in_half, q2, 0.0).astype(BF16)
        s = _dot_nt(qh, k_loc)
        if n_cache:
            s_c = _dot_nt(qh, kf[t_loc:nk, :])
        yield
        if natten:
            blocks = []
            for a in range(qrows):
                first = ws_row - (g * qrows + a) + NA_ROWS
                blocks.append(jnp.concatenate(
                    [bp_ref[hh, pl.ds(jnp.clip(first + 2 * p, 0, 2 * NA_ROWS - 1), 1)][0]
                     for p in range(win // LANES)], axis=1))
            s = s + jnp.concatenate(blocks, axis=0)
        if keep is not None:
            s = jnp.where(keep, s, NEG_INF)
        m = jnp.max(s, axis=1, keepdims=True)
        if n_cache:
            m = jnp.maximum(m, jnp.max(s_c, axis=1, keepdims=True))
        if has_sink:
            sink = sink_ref[2 * j + hh] * LOG2E
            m = jnp.maximum(m, sink)
        e = jnp.exp2(s - m)
        e_c = jnp.exp2(s_c - m) if n_cache else None
        yield
        if mode == "pair":
            o = _dot(e.astype(BF16), v_scr[hh][sl, :])
            if n_cache:
                o = o + _dot(e_c.astype(BF16), v_scr[hh][t_loc:nk, :])
            yield
            den = pltpu.roll(o, HEAD_DIM, 1)
            if has_sink:
                den = den + jnp.exp2(sink - m)
            done[hh] = o * (1.0 / den)
        else:
            den = jnp.sum(e, axis=1, keepdims=True)
            if n_cache:
                den = den + jnp.sum(e_c, axis=1, keepdims=True)
            done[hh] = (e, e_c, 1.0 / den)

    def sub_tile(t):
        g = qb * n_sub + t
        q2 = q_ref[t * tq:(t + 1) * tq, :].astype(F32) * (QK_SCALE * LOG2E)
        keep = ws_row = None
        if natten:
            ws_row = jnp.clip(g * qrows - NA_ROWS // 2, 0, rows - NA_KEY_ROWS)
            sl = pl.ds(pl.multiple_of(ws_row * GRID_W, GRID_W), win)
            q_row = lax.broadcasted_iota(jnp.int32, (tq, 1), 0) // GRID_W + g * qrows
            r0 = jnp.clip(q_row - NA_ROWS // 2, 0, rows - NA_ROWS)
            k_row = lax.broadcasted_iota(jnp.int32, (1, win), 1) // GRID_W + ws_row
            keep = (k_row >= r0) & (k_row < r0 + NA_ROWS)
        elif band:
            ws = pl.multiple_of(jnp.clip(g * tq - WINDOW, 0, t_loc - win), WINDOW)
            sl = pl.ds(ws, win)
            qpos = g * tq + lax.broadcasted_iota(jnp.int32, (tq, 1), 0)
            kpos = ws + lax.broadcasted_iota(jnp.int32, (1, win), 1)
            keep = jnp.abs(qpos - kpos) <= WINDOW
        else:
            sl = slice(0, win)
        k_loc = kf[sl, :]
        done = [None, None]
        yield from _in_waves([chain(g, hh, q2, sl, k_loc, keep, ws_row, done) for hh in range(2)])
        if mode == "pair":
            out = jnp.where(lane < HEAD_DIM, done[0], done[1])
        else:
            (e0, ec0, w0), (e1, ec1, w1) = done
            c = lam * w1 / w0
            out = _dot((e0 - e1 * c).astype(BF16), v_scr[0][sl, :])
            if n_cache:
                out = out + _dot((ec0 - ec1 * c).astype(BF16), v_scr[0][t_loc:nk, :])
            yield
            out = out * w0
            ms = jnp.mean(out * out, axis=-1, keepdims=True)
            out = out * lax.rsqrt(ms + EPS) * subln_ref[...] * (1.0 - lam_init)
        o_ref[t * tq:(t + 1) * tq, :] = out.astype(o_ref.dtype)

    yield from _in_waves([sub_tile(t) for t in range(n_sub)])


def attention(qkv, nb, t_loc, *, mode, gqa, k_col, v_col, n_sub, n_lb, cache=None, band=False, bias_pairs=None,
              sink=None, lam=None, subln=None, lam_init=0.0, d_model):
    q_arr, k_arr, v_arr = qkv if isinstance(qkv, tuple) else (qkv, qkv, qkv)
    tq = ATTN_Q_TILE * n_sub
    n_qlb = d_model // LANES
    natten = bias_pairs is not None
    n_cache = 0 if cache is None else cache[0].shape[1]
    if natten:
        win = NA_KEY_ROWS * GRID_W
    elif band:
        win = ATTN_Q_TILE + 2 * WINDOW
    else:
        win = t_loc
    if gqa:
        assert n_lb == 1 or n_lb % 4 == 0
        k_lb = max(n_lb // 4, 1)
        kgrp = (lambda jg: jg // 4) if n_lb == 1 else (lambda jg: jg)
        kmap = lambda jl: jl // 4
    else:
        k_lb = n_lb
        kgrp = lambda jg: jg
        kmap = lambda jl: jl
    qw, kw = n_lb * LANES, k_lb * LANES
    assert k_col % kw == 0 and v_col % kw == 0 and n_qlb % n_lb == 0
    kc, vc = k_col // kw, v_col // kw
    nqb = t_loc // tq
    nk = t_loc + n_cache

    in_specs = []
    args = []
    if sink is not None:
        in_specs.append(pl.BlockSpec(memory_space=pltpu.SMEM))
        args.append(sink)
    in_specs += [pl.BlockSpec((tq, qw), lambda b, j, i: (b * nqb + i, j)),
                 pl.BlockSpec((t_loc, kw), lambda b, j, i: (b, kc + kgrp(j))),
                 pl.BlockSpec((t_loc, kw), lambda b, j, i: (b, vc + kgrp(j)))]
    args += [q_arr, k_arr, v_arr]
    if cache is not None:
        in_specs += [pl.BlockSpec((1, n_cache, kw), lambda b, j, i: (b, 0, kgrp(j))),
                     pl.BlockSpec((1, n_cache, kw), lambda b, j, i: (b, 0, kgrp(j)))]
        args += list(cache)
    if natten:
        in_specs.append(pl.BlockSpec((2 * n_lb, 2 * NA_ROWS, GRID_W, LANES), lambda b, j, i: (j, 0, 0, 0)))
        args.append(bias_pairs)
    if mode == "diff":
        in_specs += [pl.BlockSpec((8, LANES), lambda b, j, i: (0, 0)),
                     pl.BlockSpec((1, LANES), lambda b, j, i: (0, 0))]
        args += [lam, subln]
    kern = functools.partial(
        _attn_kernel, n_lb=n_lb, kmap=kmap, mode=mode, gqa=gqa, t_loc=t_loc, n_cache=n_cache, win=win, band=band,
        natten=natten, has_sink=sink is not None, lam_init=lam_init, n_sub=n_sub)
    n_v = 2 if mode == "pair" else 1
    return pl.pallas_call(
        kern,
        grid=(nb, n_qlb // n_lb, nqb),
        in_specs=in_specs,
        out_specs=pl.BlockSpec((tq, qw), lambda b, j, i: (b * nqb + i, j)),
        out_shape=jax.ShapeDtypeStruct((nb * t_loc, d_model), BF16),
        scratch_shapes=[pltpu.VMEM((n_lb, nk, LANES), BF16) for _ in range(1 + n_v)],
        compiler_params=_cparams(("parallel", "arbitrary", "arbitrary")),
    )(*args)


def _natten_pair_table(rpb):
    h, n_dr, n_dc = rpb.shape
    pad = GRID_W - NA_COLS
    ext = jnp.concatenate([jnp.repeat(rpb[..., :1], pad, axis=-1), rpb,
                           jnp.repeat(rpb[..., -1:], pad + 1, axis=-1)], axis=-1).astype(F32) * LOG2E
    neg = jnp.full((h, 1, LANES), NEG_INF, F32)
    ext = jnp.concatenate([neg, ext, neg], axis=1)
    return pl.pallas_call(
        _pair_table_kernel,
        grid=(h,),
        in_specs=[pl.BlockSpec((1, n_dr + 2, LANES), lambda i: (i, 0, 0))],
        out_specs=pl.BlockSpec((1, n_dr + 1, GRID_W, LANES), lambda i: (i, 0, 0, 0)),
        out_shape=jax.ShapeDtypeStruct((h, n_dr + 1, GRID_W, LANES), F32),
        compiler_params=_cparams(("parallel",)),
    )(ext)


def _pair_table_kernel(ext_ref, o_ref):
    qc = lax.broadcasted_iota(jnp.int32, (GRID_W, LANES), 0)
    ln = lax.broadcasted_iota(jnp.int32, (GRID_W, LANES), 1)
    kc = ln % GRID_W
    cstart = jnp.clip(qc - NA_COLS // 2, 0, GRID_W - NA_COLS)
    in_window = (kc >= cstart) & (kc < cstart + NA_COLS)
    for d in range(o_ref.shape[1]):
        lo = jnp.broadcast_to(ext_ref[0, d:d + 1, :], (GRID_W, LANES))
        hi = jnp.broadcast_to(ext_ref[0, d + 1:d + 2, :], (GRID_W, LANES))
        lo = pltpu.roll(lo, GRID_W + 1, 1, stride=1, stride_axis=0)
        hi = pltpu.roll(hi, 1, 1, stride=1, stride_axis=0)
        o_ref[0, d] = jnp.where(in_window, jnp.where(ln < GRID_W, lo, hi), NEG_INF)


def _mixer_residual(a_ref, wo_ref, x_ref, mod_ref):
    return x_ref[...] + mod_ref[0, 2:3, :] * _dot(a_ref[...], wo_ref[...])


def _ffn_kernel(a_ref, wo_ref, x_ref, mod_ref, g_ref, wg_ref, wu_ref, wd_ref, o_ref):
    x1 = _mixer_residual(a_ref, wo_ref, x_ref, mod_ref)
    h = _norm_mod(x1, g_ref[...], mod_ref[0, 3:4, :], mod_ref[0, 4:5, :]).astype(BF16)
    t = (_silu(_dot(h, wg_ref[...])) * _dot(h, wu_ref[...])).astype(BF16)
    o_ref[...] = x1 + mod_ref[0, 5:6, :] * _dot(t, wd_ref[...])


def dense_ffn(a, wo, x, mod, mod_of_block, g, wg, wu, wd):
    r, d = x.shape
    ff = wg.shape[1]
    tm = ROW_TILE

    def resident(shape):
        return pl.BlockSpec(shape, lambda i: (0, 0), pipeline_mode=pl.Buffered(1))

    return pl.pallas_call(
        _ffn_kernel,
        grid=(r // tm,),
        in_specs=[pl.BlockSpec((tm, d), lambda i: (i, 0)),
                  resident((d, d)),
                  pl.BlockSpec((tm, d), lambda i: (i, 0)),
                  pl.BlockSpec((1, 6, d), lambda i: (mod_of_block(i), 0, 0)),
                  pl.BlockSpec((1, d), lambda i: (0, 0)),
                  resident((d, ff)),
                  resident((d, ff)),
                  resident((ff, d))],
        out_specs=pl.BlockSpec((tm, d), lambda i: (i, 0)),
        out_shape=jax.ShapeDtypeStruct((r, d), F32),
        compiler_params=_cparams(("parallel",)),
    )(a, wo, x, mod, g.reshape(1, d), wg, wu, wd)


def _router_kernel(a_ref, wo_ref, x_ref, mod_ref, g_ref, wr_ref,
                   x1_ref, h_ref, rank_row_ref, gate_row_ref, rank_col_ref, cnt_ref):
    x1 = _mixer_residual(a_ref, wo_ref, x_ref, mod_ref)
    x1_ref[...] = x1
    h = _norm_mod(x1, g_ref[...], mod_ref[0, 3:4, :], mod_ref[0, 4:5, :])
    h_hi = h.astype(BF16)
    h_ref[...] = h_hi
    h_lo = (h - h_hi.astype(F32)).astype(BF16)
    wr = wr_ref[...]
    w_hi = wr.astype(BF16)
    w_lo = (wr - w_hi.astype(F32)).astype(BF16)
    hh = _dot(h_hi, jnp.concatenate([w_hi, w_lo], axis=1))
    logits = hh[:, :LANES] + (_dot(h_lo, w_hi) + hh[:, LANES:])
    tb = logits.shape[0]
    lane = lax.broadcasted_iota(jnp.int32, logits.shape, 1)
    lane_f = lane.astype(F32)
    logits = jnp.where(lane < N_EXPERTS, logits, -jnp.inf)
    m1 = jnp.max(logits, axis=1, keepdims=True)
    i1 = jnp.min(jnp.where(logits == m1, lane_f, float(LANES)), axis=1, keepdims=True)
    rest = jnp.where(lane_f == i1, -jnp.inf, logits)
    m2 = jnp.max(rest, axis=1, keepdims=True)
    i2 = jnp.min(jnp.where(rest == m2, lane_f, float(LANES)), axis=1, keepdims=True)
    e2 = jnp.exp(m2 - m1)
    inv = 1.0 / (1.0 + e2)
    is1 = lane_f == i1
    is2 = lane_f == i2
    gates = jnp.where(is1, inv, 0.0) + jnp.where(is2, e2 * inv, 0.0)
    sel = jnp.where(is1 | is2, 1.0, 0.0)
    sel_b = sel.astype(BF16)
    chunk = 256
    parts = []
    for c in range(tb // chunk):
        ri = lax.broadcasted_iota(jnp.int32, (chunk, tb), 0) + c * chunk
        ci = lax.broadcasted_iota(jnp.int32, (chunk, tb), 1)
        parts.append(_dot(jnp.where(ci < ri, 1.0, 0.0).astype(BF16), sel_b))
    rank = jnp.concatenate(parts, axis=0)
    rank_sel = jnp.where(sel > 0.5, rank, -1.0)
    rank_col_ref[...] = rank_sel
    rank_row_ref[0] = rank_sel.T[:N_EXPERTS]
    gate_row_ref[0] = gates.T[:N_EXPERTS]
    cnt_ref[0] = jnp.broadcast_to(jnp.sum(sel, axis=0, keepdims=True), (8, LANES)).astype(jnp.int32)


def _expert_kernel(cnt_ref, h_ref, rank_row_ref, gate_row_ref, rank_col_ref, wg_ref, wu_ref, wd_ref,
                   x_ref, mod_ref, o_ref):
    b, e = pl.program_id(0), pl.program_id(1)
    tb = h_ref.shape[0]
    n_rows = cnt_ref[b * N_EXPERTS + e]

    @pl.when(e == 0)
    def _():
        o_ref[...] = jnp.zeros_like(o_ref)

    rank_e = rank_row_ref[0, pl.ds(e, 1), :]
    gate_e = gate_row_ref[0, pl.ds(e, 1), :]
    lane = lax.broadcasted_iota(jnp.int32, (tb, LANES), 1)
    rank_c = jnp.sum(jnp.where(lane == e, rank_col_ref[...], 0.0), axis=1, keepdims=True)

    def tile(base, m):
        want = (lax.broadcasted_iota(jnp.int32, (m, tb), 0) + base).astype(F32)
        p = jnp.where(rank_e == want, 1.0, 0.0)
        xr = _dot(p.astype(BF16), h_ref[...]).astype(BF16)
        gate = jnp.sum(p * gate_e, axis=1, keepdims=True)
        t = (_silu(_dot(xr, wg_ref[0])) * _dot(xr, wu_ref[0])).astype(BF16)
        yg = (_dot(t, wd_ref[0]) * gate).astype(BF16)
        want_t = (lax.broadcasted_iota(jnp.int32, (tb, m), 1) + base).astype(F32)
        pt = jnp.where(rank_c == want_t, 1.0, 0.0).astype(BF16)
        o_ref[...] += _dot(pt, yg)

    n_full = n_rows // MOE_SUB
    rest = n_rows - n_full * MOE_SUB

    def full_tile(i, carry):
        tile(pl.multiple_of(i * MOE_SUB, MOE_SUB), MOE_SUB)
        return carry

    lax.fori_loop(0, n_full, full_tile, 0)
    tail_base = pl.multiple_of(n_full * MOE_SUB, MOE_SUB)
    pl.when(rest > MOE_SUB // 2)(lambda: tile(tail_base, MOE_SUB))
    pl.when((rest > 0) & (rest <= MOE_SUB // 2))(lambda: tile(tail_base, MOE_SUB // 2))

    @pl.when(e == N_EXPERTS - 1)
    def _():
        o_ref[...] = x_ref[...] + mod_ref[0, 5:6, :] * o_ref[...]


def moe_ffn(a, wo, x, mod, mod_of_block, g, w_router, wg, wu, wd):
    r, d = x.shape
    ff = wg.shape[2]
    tb = MOE_BLOCK
    nb = r // tb
    wr = jnp.pad(w_router, ((0, 0), (0, LANES - N_EXPERTS)))
    x, h, rank_row, gate_row, rank_col, cnt = pl.pallas_call(
        _router_kernel,
        grid=(nb,),
        in_specs=[pl.BlockSpec((tb, d), lambda i: (i, 0)),
                  pl.BlockSpec((d, d), lambda i: (0, 0)),
                  pl.BlockSpec((tb, d), lambda i: (i, 0)),
                  pl.BlockSpec((1, 6, d), lambda i: (mod_of_block(i), 0, 0)),
                  pl.BlockSpec((1, d), lambda i: (0, 0)),
                  pl.BlockSpec((d, LANES), lambda i: (0, 0))],
        out_specs=[pl.BlockSpec((tb, d), lambda i: (i, 0)),
                   pl.BlockSpec((tb, d), lambda i: (i, 0)),
                   pl.BlockSpec((1, N_EXPERTS, tb), lambda i: (i, 0, 0)),
                   pl.BlockSpec((1, N_EXPERTS, tb), lambda i: (i, 0, 0)),
                   pl.BlockSpec((tb, LANES), lambda i: (i, 0)),
                   pl.BlockSpec((1, 8, LANES), lambda i: (i, 0, 0))],
        out_shape=[jax.ShapeDtypeStruct((r, d), F32),
                   jax.ShapeDtypeStruct((r, d), BF16),
                   jax.ShapeDtypeStruct((nb, N_EXPERTS, tb), F32),
                   jax.ShapeDtypeStruct((nb, N_EXPERTS, tb), F32),
                   jax.ShapeDtypeStruct((r, LANES), F32),
                   jax.ShapeDtypeStruct((nb, 8, LANES), jnp.int32)],
        compiler_params=_cparams(("parallel",)),
    )(a, wo, x, mod, g.reshape(1, d), wr)
    counts = cnt[:, 0, :N_EXPERTS].reshape(nb * N_EXPERTS)

    grid_spec = pltpu.PrefetchScalarGridSpec(
        num_scalar_prefetch=1,
        grid=(nb, N_EXPERTS),
        in_specs=[pl.BlockSpec((tb, d), lambda i, e, c: (i, 0)),
                  pl.BlockSpec((1, N_EXPERTS, tb), lambda i, e, c: (i, 0, 0)),
                  pl.BlockSpec((1, N_EXPERTS, tb), lambda i, e, c: (i, 0, 0)),
                  pl.BlockSpec((tb, LANES), lambda i, e, c: (i, 0)),
                  pl.BlockSpec((1, d, ff), lambda i, e, c: (e, 0, 0)),
                  pl.BlockSpec((1, d, ff), lambda i, e, c: (e, 0, 0)),
                  pl.BlockSpec((1, ff, d), lambda i, e, c: (e, 0, 0)),
                  pl.BlockSpec((tb, d), lambda i, e, c: (i, 0)),
                  pl.BlockSpec((1, 6, d), lambda i, e, c: (mod_of_block(i), 0, 0))],
        out_specs=pl.BlockSpec((tb, d), lambda i, e, c: (i, 0)),
    )
    return pl.pallas_call(
        _expert_kernel,
        grid_spec=grid_spec,
        out_shape=jax.ShapeDtypeStruct((r, d), F32),
        compiler_params=_cparams(("parallel", "arbitrary")),
    )(counts, h, rank_row, gate_row, rank_col, wg, wu, wd, x, mod)


def _diff_lambda_init(layer):
    return 0.8 - 0.6 * math.exp(-0.3 * layer)


def _head_gain_row(q_norm, k_norm, n_q, n_k):
    return jnp.concatenate([jnp.tile(q_norm.astype(F32), n_q // HEAD_DIM),
                            jnp.tile(k_norm.astype(F32), n_k // HEAD_DIM)]).reshape(1, -1)


def kernel(x_prompt, x_sample, cache_k_0, cache_v_0, cache_k_1, cache_v_1, cache_k_2, cache_v_2, cache_k_3, cache_v_3, c, c_ctx, norm1_0, w_ada_0, b_ada_0, w_qkv_0, q_norm_0, k_norm_0, rpb_0, w_o_0, norm2_0, w_gate_0, w_up_0, w_down_0, norm1_1, w_ada_1, b_ada_1, w_qkv_1, q_norm_1, k_norm_1, lam_q1_1, lam_k1_1, lam_q2_1, lam_k2_1, subln_1, w_o_1, norm2_1, w_router_1, w_egate_1, w_eup_1, w_edown_1, norm1_2, w_ada_2, b_ada_2, w_qkv_2, q_norm_2, k_norm_2, sink_2, w_o_2, norm2_2, w_gate_2, w_up_2, w_down_2, norm1_3, w_ada_3, b_ada_3, w_qkv_3, q_norm_3, k_norm_3, w_o_3, norm2_3, w_router_3, w_egate_3, w_eup_3, w_edown_3):
    nbc, tc, d = x_prompt.shape
    nbl, tl, _ = x_sample.shape
    n_past = cache_k_0.shape[1]
    assert d % (2 * LANES) == 0 and tc % ATTN_Q_TILE == 0 and tl % MOE_BLOCK == 0
    assert (nbc * tc) % MOE_BLOCK == 0 and tl // GRID_W >= NA_KEY_ROWS and nbl <= 7

    norm1 = (norm1_0, norm1_1, norm1_2, norm1_3)
    norm2 = (norm2_0, norm2_1, norm2_2, norm2_3)
    ada_p = ((w_ada_0, b_ada_0), (w_ada_1, b_ada_1), (w_ada_2, b_ada_2), (w_ada_3, b_ada_3))
    w_qkv = (w_qkv_0, w_qkv_1, w_qkv_2, w_qkv_3)
    qk_norm = ((q_norm_0, k_norm_0), (q_norm_1, k_norm_1), (q_norm_2, k_norm_2), (q_norm_3, k_norm_3))
    w_out = (w_o_0, w_o_1, w_o_2, w_o_3)
    ffn_p = ((w_gate_0, w_up_0, w_down_0), (w_router_1, w_egate_1, w_eup_1, w_edown_1),
             (w_gate_2, w_up_2, w_down_2), (w_router_3, w_egate_3, w_eup_3, w_edown_3))
    caches = ((cache_k_0, cache_v_0), (cache_k_1, cache_v_1), (cache_k_2, cache_v_2), (cache_k_3, cache_v_3))

    xc = x_prompt.reshape(nbc * tc, d)
    xl = x_sample.reshape(nbl * tl, d)
    cond = jnp.concatenate([c_ctx[None, :], c, jnp.zeros((7 - nbl, d), F32)], axis=0)
    cond16 = jnp.concatenate([cond, cond], axis=0)

    def ctx_mod(blk_rows):
        return lambda i: 0

    def lat_mod(blk_rows):
        per = tl // blk_rows
        return lambda i: 1 + i // per

    lam_rows = jnp.zeros((8, LANES), F32)
    for r_, v_ in enumerate((lam_q1_1, lam_k1_1, lam_q2_1, lam_k2_1)):
        lam_rows = lam_rows.at[r_, :HEAD_DIM].set(v_.astype(F32))
    bias_pairs = _natten_pair_table(rpb_0)
    rope_tabs = _rope_tables(tl)

    states = []
    for l in range(4):
        mixer = l % 4
        mod = ada_mod(cond16, *ada_p[l])
        wq = w_qkv[l].astype(BF16)
        n = wq.shape[1]
        n_q = d
        n_k = d if mixer < 2 else (n - d) // 2
        n_v = n - n_q - n_k
        hn = _head_gain_row(*qk_norm[l], n_q, n_k)
        rope = None if mixer == 0 else rope_tabs

        qkv_c = qkv_proj(xc, mod, ctx_mod(ROW_TILE), norm1[l], wq, hn, None, F32, split=(n_q, n_k, n_v))
        qkv_l = qkv_proj(xl, mod, lat_mod(ROW_TILE), norm1[l], wq, hn, rope, BF16)
        states.append(qkv_c[1:])

        ck, cv = caches[l]
        cache = (ck.astype(BF16).reshape(nbl, n_past, n_k), cv.astype(BF16).reshape(nbl, n_past, n_v))
        common = dict(k_col=n_q, v_col=n_q + n_k, d_model=d)
        ctx = dict(n_sub=1, n_lb=CTX_LANE_BLOCKS, k_col=0, v_col=0, d_model=d)
        lat = dict(n_sub=4, n_lb=1, cache=cache, **common)
        if mixer == 0:
            o_c = attention(qkv_c, nbc, tc, mode="pair", gqa=False, **ctx)
            o_l = attention(qkv_l, nbl, tl, mode="pair", gqa=False, bias_pairs=bias_pairs, **lat)
        elif mixer == 1:
            diff = dict(mode="diff", gqa=False, lam=lam_rows, subln=subln_1.astype(F32).reshape(1, LANES),
                        lam_init=_diff_lambda_init(l))
            o_c = attention(qkv_c, nbc, tc, **diff, **ctx)
            o_l = attention(qkv_l, nbl, tl, **diff, **lat)
        elif mixer == 2:
            sk = sink_2.astype(F32)
            o_c = attention(qkv_c, nbc, tc, mode="pair", gqa=True, sink=sk, **ctx)
            o_l = attention(qkv_l, nbl, tl, mode="pair", gqa=True, band=True, sink=sk, **lat)
        else:
            o_c = attention(qkv_c, nbc, tc, mode="pair", gqa=True, **ctx)
            o_l = attention(qkv_l, nbl, tl, mode="pair", gqa=True, **lat)

        wo = w_out[l].astype(BF16)
        if l % 2 == 0:
            wg, wu, wd = (w.astype(BF16) for w in ffn_p[l])
            xc = dense_ffn(o_c, wo, xc, mod, ctx_mod(ROW_TILE), norm2[l], wg, wu, wd)
            xl = dense_ffn(o_l, wo, xl, mod, lat_mod(ROW_TILE), norm2[l], wg, wu, wd)
        else:
            wr = ffn_p[l][0]
            wg, wu, wd = (w.astype(BF16) for w in ffn_p[l][1:])
            xc = moe_ffn(o_c, wo, xc, mod, ctx_mod(MOE_BLOCK), norm2[l], wr, wg, wu, wd)
            xl = moe_ffn(o_l, wo, xl, mod, lat_mod(MOE_BLOCK), norm2[l], wr, wg, wu, wd)

    nh = d // HEAD_DIM
    k0, v0 = states[0]
    k1, v1 = states[1]
    k2, v2 = states[2]
    k3, v3 = states[3]
    nkv = k2.shape[1] // HEAD_DIM
    return (xc.reshape(nbc, tc, d), xl.reshape(nbl, tl, d),
            k0.reshape(nbc, tc, nh, HEAD_DIM), v0.reshape(nbc, tc, nh, HEAD_DIM),
            k1.reshape(nbc, tc, nh // 2, 2, HEAD_DIM), v1.reshape(nbc, tc, nh // 2, 2 * HEAD_DIM),
            k2.reshape(nbc, tc, nkv, HEAD_DIM), v2.reshape(nbc, tc, nkv, HEAD_DIM),
            k3.reshape(nbc, tc, nkv, HEAD_DIM), v3.reshape(nbc, tc, nkv, HEAD_DIM))
```

```python
import functools
import math

import numpy as np
import jax
import jax.numpy as jnp
from jax import lax
from jax.experimental import pallas as pl
from jax.experimental.pallas import tpu as pltpu

F32 = jnp.float32
BF16 = jnp.bfloat16

HEAD_DIM = 64
GRID_W = 64
NA_ROWS = 8
NA_COLS = 16
WINDOW = 128
ROPE_BASE = 10000.0
ROPE_PAIRS_AXIS = HEAD_DIM // 4
N_EXPERTS = 8
EPS = 1e-6
NEG_INF = -1e30
QK_SCALE = HEAD_DIM ** -0.5
LOG2E = math.log2(math.e)

LANES = 128
V7X_VMEM_BYTES = 64 * 1024 * 1024
VMEM_LIMIT = V7X_VMEM_BYTES - 8 * 1024 * 1024

ROW_TILE = 512
ATTN_Q_TILE = 256
NA_KEY_ROWS = 12
CTX_LANE_BLOCKS = 8
MOE_BLOCK = 1024
MOE_SUB = 256


def _cparams(sem):
    return pltpu.CompilerParams(dimension_semantics=sem, vmem_limit_bytes=VMEM_LIMIT)


def _silu(x):
    return x * (1.0 / (1.0 + jnp.exp(-x)))


def _norm_mod(x, g, shift, scale):
    ms = jnp.mean(x * x, axis=-1, keepdims=True)
    y = x * lax.rsqrt(ms + EPS) * g
    return y * (1.0 + scale) + shift


def _dot(a, b):
    return jnp.dot(a, b, preferred_element_type=F32)


def _dot_nt(a, b):
    return lax.dot_general(a, b, (((1,), (1,)), ((), ())), preferred_element_type=F32)


def _ada_kernel(c_ref, w_ref, b_ref, o_ref):
    a = _silu(c_ref[...])
    a_hi = a.astype(BF16)
    a_lo = a - a_hi.astype(F32)
    row = lax.broadcasted_iota(jnp.int32, a.shape, 0)
    lhs = jnp.where(row < 8, a_hi.astype(F32), a_lo).astype(BF16)
    w = w_ref[...]
    w_hi = w.astype(BF16)
    w_lo = (w - w_hi.astype(F32)).astype(BF16)
    r = _dot(lhs, w_hi)
    r2 = _dot(a_hi, w_lo)
    o_ref[...] = r[:8] + r[8:] + r2[:8] + b_ref[...]


def ada_mod(cond16, w_ada, b_ada):
    d, n = w_ada.shape
    tn = 1024
    out = pl.pallas_call(
        _ada_kernel,
        grid=(n // tn,),
        in_specs=[pl.BlockSpec((16, d), lambda j: (0, 0)),
                  pl.BlockSpec((d, tn), lambda j: (0, j)),
                  pl.BlockSpec((1, tn), lambda j: (0, j))],
        out_specs=pl.BlockSpec((8, tn), lambda j: (0, j)),
        out_shape=jax.ShapeDtypeStruct((8, n), F32),
        compiler_params=_cparams(("arbitrary",)),
    )(cond16, w_ada, b_ada.reshape(1, n))
    return out.reshape(8, 6, d)


def _qkv_kernel(*refs, rope, n_norm, n_out, n_cast):
    n_in = len(refs) - n_out - 2 * n_cast
    cast_in, refs = refs[n_in:n_in + n_cast], refs[:n_in] + refs[n_in + n_cast:]
    cast_out, refs = refs[len(refs) - n_cast:], refs[:len(refs) - n_cast]
    for src, dst in zip(cast_in, cast_out):
        dst[...] = src[...].astype(dst.dtype)
    o_refs = refs[len(refs) - n_out:]
    if rope:
        x_ref, mod_ref, g_ref, w_ref, seg_ref, hn_ref, cos_ref, sin_ref = refs[:-n_out]
    else:
        x_ref, mod_ref, g_ref, w_ref, seg_ref, hn_ref = refs[:-n_out]
    h = _norm_mod(x_ref[...], g_ref[...], mod_ref[0, 0:1, :], mod_ref[0, 1:2, :]).astype(BF16)
    acc = _dot(h, w_ref[...])
    qk = acc[:, :n_norm]
    sq = (qk * qk).astype(BF16)
    seg = seg_ref[...]
    ssum = jnp.concatenate(
        [_dot(sq[:, c * 2 * LANES:(c + 1) * 2 * LANES], seg) for c in range(n_norm // (2 * LANES))], axis=1)
    y = qk * lax.rsqrt(ssum * (1.0 / HEAD_DIM) + EPS) * hn_ref[...]
    if rope:
        lane = lax.broadcasted_iota(jnp.int32, (acc.shape[0], LANES), 1)
        even = (lane % 2) == 0
        cos, sin = cos_ref[...], sin_ref[...]
        parts = []
        for c in range(n_norm // LANES):
            yc = y[:, c * LANES:(c + 1) * LANES]
            nxt = pltpu.roll(yc, LANES - 1, 1)
            prv = pltpu.roll(yc, 1, 1)
            parts.append(yc * cos + jnp.where(even, nxt, prv) * sin)
        y = jnp.concatenate(parts, axis=1)
    if n_out == 1:
        o_ref, = o_refs
        o_ref[:, :n_norm] = y.astype(o_ref.dtype)
        o_ref[:, n_norm:] = acc[:, n_norm:].astype(o_ref.dtype)
    else:
        q_ref, k_ref, v_ref = o_refs
        n_q = q_ref.shape[1]
        q_ref[...] = y[:, :n_q].astype(q_ref.dtype)
        k_ref[...] = y[:, n_q:].astype(k_ref.dtype)
        v_ref[...] = acc[:, n_norm:].astype(v_ref.dtype)


def _seg_matrix():
    i = np.arange(2 * LANES)
    return jnp.asarray((i[:, None] // HEAD_DIM) == (i[None, :] // HEAD_DIM), BF16)


def qkv_proj(x, mod, mod_of_block, g, w, hn, rope_tabs, out_dtype, split=None, cast=()):
    r, d = x.shape
    n = w.shape[1]
    n_norm = hn.shape[1]
    tm = ROW_TILE
    steps = r // tm
    rope = rope_tabs is not None
    in_specs = [pl.BlockSpec((tm, d), lambda i: (i, 0)),
                pl.BlockSpec((1, 6, d), lambda i: (mod_of_block(i), 0, 0)),
                pl.BlockSpec((1, d), lambda i: (0, 0)),
                pl.BlockSpec((d, n), lambda i: (0, 0), pipeline_mode=pl.Buffered(1)),
                pl.BlockSpec((2 * LANES, 2 * LANES), lambda i: (0, 0)),
                pl.BlockSpec((1, n_norm), lambda i: (0, 0))]
    args = [x, mod, g.reshape(1, d), w, _seg_matrix(), hn]
    if rope:
        cos_t, sin_t = rope_tabs
        nblk = cos_t.shape[0] // tm
        in_specs += [pl.BlockSpec((tm, LANES), lambda i: (i % nblk, 0)),
                     pl.BlockSpec((tm, LANES), lambda i: (i % nblk, 0))]
        args += [cos_t, sin_t]
    widths = (n,) if split is None else split
    out_specs = [pl.BlockSpec((tm, wd), lambda i: (i, 0)) for wd in widths]
    out_shape = [jax.ShapeDtypeStruct((r, wd), out_dtype) for wd in widths]
    for a in cast:
        assert a.shape[0] % (8 * steps) == 0
        share = pl.BlockSpec((a.shape[0] // steps, a.shape[1]), lambda i: (i, 0))
        in_specs.append(share)
        args.append(a)
        out_specs.append(share)
        out_shape.append(jax.ShapeDtypeStruct(a.shape, BF16))
    outs = pl.pallas_call(
        functools.partial(_qkv_kernel, rope=rope, n_norm=n_norm, n_out=len(widths), n_cast=len(cast)),
        grid=(steps,),
        in_specs=in_specs,
        out_specs=out_specs,
        out_shape=out_shape,
        compiler_params=_cparams(("parallel",)),
    )(*args)
    main, casted = outs[:len(widths)], tuple(outs[len(widths):])
    main = main[0] if split is None else tuple(main)
    return (main, casted) if cast else main


def _rope_tables(t):
    pos = np.arange(t)
    row = (pos // GRID_W).astype(np.float32)
    col = (pos % GRID_W).astype(np.float32)
    freqs = jnp.asarray(ROPE_BASE, F32) ** (-jnp.arange(ROPE_PAIRS_AXIS, dtype=F32) / ROPE_PAIRS_AXIS)
    ang = jnp.concatenate([jnp.asarray(row)[:, None] * freqs, jnp.asarray(col)[:, None] * freqs], axis=-1)
    cos = jnp.repeat(jnp.cos(ang), 2, axis=-1)
    sin = jnp.repeat(jnp.sin(ang), 2, axis=-1)
    sign = jnp.asarray(np.where(np.arange(HEAD_DIM) % 2 == 0, -1.0, 1.0), F32)
    reps = LANES // HEAD_DIM
    return jnp.tile(cos, (1, reps)), jnp.tile(sin * sign, (1, reps))


def _in_waves(gens):
    while gens:
        gens = [g for g in gens if next(g, StopIteration) is not StopIteration]
        yield


def _attn_kernel(*refs, n_lb, kmap, mode, n_cache, natten, has_sink, **static):
    it = iter(refs)
    sink_ref = next(it) if has_sink else None
    q_ref, k_ref, v_ref = next(it), next(it), next(it)
    ck_ref, cv_ref = (next(it), next(it)) if n_cache else (None, None)
    bp_ref = next(it) if natten else None
    lam_refs = [next(it), next(it)] if mode == "diff" else []
    o_ref, kf = next(it), next(it)
    v_scr = list(it)
    gens = []
    for jl in range(n_lb):
        q_cols = pl.ds(jl * LANES, LANES)
        k_cols = pl.ds(kmap(jl) * LANES, LANES)
        views = [sink_ref] if has_sink else []
        views += [q_ref.at[:, q_cols], k_ref.at[:, k_cols], v_ref.at[:, k_cols]]
        if n_cache:
            views += [ck_ref.at[:, :, k_cols], cv_ref.at[:, :, k_cols]]
        if natten:
            views.append(bp_ref.at[pl.ds(2 * jl, 2)])
        views += lam_refs + [o_ref.at[:, q_cols], kf.at[jl]] + [v.at[jl] for v in v_scr]
        gens.append(_attn_lane_block(views, pl.program_id(1) * n_lb + jl, pl.program_id(2), mode=mode,
                                     n_cache=n_cache, natten=natten, has_sink=has_sink, **static))
    for _ in _in_waves(gens):
        pass


def _attn_lane_block(refs, j, qb, *, mode, gqa, t_loc, n_cache, win, band, natten, has_sink, lam_init, n_sub):
    it = iter(refs)
    sink_ref = next(it) if has_sink else None
    q_ref, k_ref, v_ref = next(it), next(it), next(it)
    ck_ref = cv_ref = bp_ref = lam_ref = subln_ref = None
    if n_cache:
        ck_ref, cv_ref = next(it), next(it)
    if natten:
        bp_ref = next(it)
    if mode == "diff":
        lam_ref, subln_ref = next(it), next(it)
    o_ref, kf = next(it), next(it)
    v_scr = [next(it), next(it)] if mode == "pair" else [next(it)]

    tq = ATTN_Q_TILE
    nk = t_loc + n_cache
    rows = t_loc // GRID_W
    qrows = tq // GRID_W

    def place(a):
        if not gqa:
            return a
        half = (j // 2) % 2
        lane_half = (lax.broadcasted_iota(jnp.int32, a.shape, 1) >= HEAD_DIM).astype(jnp.int32)
        return jnp.where(lane_half == half, a, pltpu.roll(a, HEAD_DIM, 1))

    def put_v(lo, hi, a):
        if mode == "pair":
            ln = lax.broadcasted_iota(jnp.int32, a.shape, 1)
            v_scr[0][lo:hi, :] = jnp.where(ln < HEAD_DIM, a, 1.0).astype(BF16)
            v_scr[1][lo:hi, :] = jnp.where(ln >= HEAD_DIM, a, 1.0).astype(BF16)
        else:
            v_scr[0][lo:hi, :] = a.astype(BF16)

    @pl.when(qb == 0)
    def _():
        kf[0:t_loc, :] = place(k_ref[...].astype(F32)).astype(BF16)
        put_v(0, t_loc, place(v_ref[...].astype(F32)))
        if n_cache:
            kf[t_loc:nk, :] = place(ck_ref[0].astype(F32)).astype(BF16)
            put_v(t_loc, nk, place(cv_ref[0].astype(F32)))

    lane = lax.broadcasted_iota(jnp.int32, (tq, LANES), 1)
    if mode == "diff":
        lp = lam_ref[...]
        lam = (jnp.exp(jnp.sum(lp[0:1] * lp[1:2], axis=1, keepdims=True))
               - jnp.exp(jnp.sum(lp[2:3] * lp[3:4], axis=1, keepdims=True)) + lam_init)

    def chain(g, hh, q2, sl, k_loc, keep, ws_row, done):
        in_half = (lane < HEAD_DIM) if hh == 0 else (lane >= HEAD_DIM)
        qh = jnp.where(in_half, q2, 0.0).astype(BF16)
        s = _dot_nt(qh, k_loc)
        if n_cache:
            s_c = _dot_nt(qh, kf[t_loc:nk, :])
        yield
        if natten:
            blocks = []
            for a in range(qrows):
                first = ws_row - (g * qrows + a) + NA_ROWS
                blocks.append(jnp.concatenate(
                    [bp_ref[hh, pl.ds(jnp.clip(first + 2 * p, 0, 2 * NA_ROWS - 1), 1)][0]
                     for p in range(win // LANES)], axis=1))
            s = s + jnp.concatenate(blocks, axis=0)
        if keep is not None:
            s = jnp.where(keep, s, NEG_INF)
        m = jnp.max(s, axis=1, keepdims=True)
        if n_cache:
            m = jnp.maximum(m, jnp.max(s_c, axis=1, keepdims=True))
        if has_sink:
            sink = sink_ref[2 * j + hh] * LOG2E
            m = jnp.maximum(m, sink)
        e = jnp.exp2(s - m)
        e_c = jnp.exp2(s_c - m) if n_cache else None
        yield
        if mode == "pair":
            o = _dot(e.astype(BF16), v_scr[hh][sl, :])
            if n_cache:
                o = o + _dot(e_c.astype(BF16), v_scr[hh][t_loc:nk, :])
            yield
            den = pltpu.roll(o, HEAD_DIM, 1)
            if has_sink:
                den = den + jnp.exp2(sink - m)
            done[hh] = o * (1.0 / den)
        else:
            den = jnp.sum(e, axis=1, keepdims=True)
            if n_cache:
                den = den + jnp.sum(e_c, axis=1, keepdims=True)
            done[hh] = (e, e_c, 1.0 / den)

    def sub_tile(t):
        g = qb * n_sub + t
        q2 = q_ref[t * tq:(t + 1) * tq, :].astype(F32) * (QK_SCALE * LOG2E)
        keep = ws_row = None
        if natten:
            ws_row = jnp.clip(g * qrows - NA_ROWS // 2, 0, rows - NA_KEY_ROWS)
            sl = pl.ds(pl.multiple_of(ws_row * GRID_W, GRID_W), win)
            q_row = lax.broadcasted_iota(jnp.int32, (tq, 1), 0) // GRID_W + g * qrows
            r0 = jnp.clip(q_row - NA_ROWS // 2, 0, rows - NA_ROWS)
            k_row = lax.broadcasted_iota(jnp.int32, (1, win), 1) // GRID_W + ws_row
            keep = (k_row >= r0) & (k_row < r0 + NA_ROWS)
        elif band:
            ws = pl.multiple_of(jnp.clip(g * tq - WINDOW, 0, t_loc - win), WINDOW)
            sl = pl.ds(ws, win)
            qpos = g * tq + lax.broadcasted_iota(jnp.int32, (tq, 1), 0)
            kpos = ws + lax.broadcasted_iota(jnp.int32, (1, win), 1)
            keep = jnp.abs(qpos - kpos) <= WINDOW
        else:
            sl = slice(0, win)
        k_loc = kf[sl, :]
        done = [None, None]
        yield from _in_waves([chain(g, hh, q2, sl, k_loc, keep, ws_row, done) for hh in range(2)])
        if mode == "pair":
            out = jnp.where(lane < HEAD_DIM, done[0], done[1])
        else:
            (e0, ec0, w0), (e1, ec1, w1) = done
            c = lam * w1 / w0
            out = _dot((e0 - e1 * c).astype(BF16), v_scr[0][sl, :])
            if n_cache:
                out = out + _dot((ec0 - ec1 * c).astype(BF16), v_scr[0][t_loc:nk, :])
            yield
            out = out * w0
            ms = jnp.mean(out * out, axis=-1, keepdims=True)
            out = out * lax.rsqrt(ms + EPS) * subln_ref[...] * (1.0 - lam_init)
        o_ref[t * tq:(t + 1) * tq, :] = out.astype(o_ref.dtype)

    yield from _in_waves([sub_tile(t) for t in range(n_sub)])


def attention(qkv, nb, t_loc, *, mode, gqa, k_col, v_col, n_sub, n_lb, cache=None, band=False, bias_pairs=None,
              sink=None, lam=None, subln=None, lam_init=0.0, d_model):
    q_arr, k_arr, v_arr = qkv if isinstance(qkv, tuple) else (qkv, qkv, qkv)
    tq = ATTN_Q_TILE * n_sub
    n_qlb = d_model // LANES
    natten = bias_pairs is not None
    n_cache = 0 if cache is None else cache[0].shape[1]
    if natten:
        win = NA_KEY_ROWS * GRID_W
    elif band:
        win = ATTN_Q_TILE + 2 * WINDOW
    else:
        win = t_loc
    if gqa:
        assert n_lb == 1 or n_lb % 4 == 0
        k_lb = max(n_lb // 4, 1)
        kgrp = (lambda jg: jg // 4) if n_lb == 1 else (lambda jg: jg)
        kmap = lambda jl: jl // 4
    else:
        k_lb = n_lb
        kgrp = lambda jg: jg
        kmap = lambda jl: jl
    qw, kw = n_lb * LANES, k_lb * LANES
    assert k_col % kw == 0 and v_col % kw == 0 and n_qlb % n_lb == 0
    kc, vc = k_col // kw, v_col // kw
    nqb = t_loc // tq
    nk = t_loc + n_cache

    in_specs = []
    args = []
    if sink is not None:
        in_specs.append(pl.BlockSpec(memory_space=pltpu.SMEM))
        args.append(sink)
    in_specs += [pl.BlockSpec((tq, qw), lambda b, j, i: (b * nqb + i, j)),
                 pl.BlockSpec((t_loc, kw), lambda b, j, i: (b, kc + kgrp(j))),
                 pl.BlockSpec((t_loc, kw), lambda b, j, i: (b, vc + kgrp(j)))]
    args += [q_arr, k_arr, v_arr]
    if cache is not None:
        in_specs += [pl.BlockSpec((1, n_cache, kw), lambda b, j, i: (b, 0, kgrp(j))),
                     pl.BlockSpec((1, n_cache, kw), lambda b, j, i: (b, 0, kgrp(j)))]
        args += list(cache)
    if natten:
        in_specs.append(pl.BlockSpec((2 * n_lb, 2 * NA_ROWS, GRID_W, LANES), lambda b, j, i: (j, 0, 0, 0)))
        args.append(bias_pairs)
    if mode == "diff":
        in_specs += [pl.BlockSpec((8, LANES), lambda b, j, i: (0, 0)),
                     pl.BlockSpec((1, LANES), lambda b, j, i: (0, 0))]
        args += [lam, subln]
    kern = functools.partial(
        _attn_kernel, n_lb=n_lb, kmap=kmap, mode=mode, gqa=gqa, t_loc=t_loc, n_cache=n_cache, win=win, band=band,
        natten=natten, has_sink=sink is not None, lam_init=lam_init, n_sub=n_sub)
    n_v = 2 if mode == "pair" else 1
    return pl.pallas_call(
        kern,
        grid=(nb, n_qlb // n_lb, nqb),
        in_specs=in_specs,
        out_specs=pl.BlockSpec((tq, qw), lambda b, j, i: (b * nqb + i, j)),
        out_shape=jax.ShapeDtypeStruct((nb * t_loc, d_model), BF16),
        scratch_shapes=[pltpu.VMEM((n_lb, nk, LANES), BF16) for _ in range(1 + n_v)],
        compiler_params=_cparams(("parallel", "arbitrary", "arbitrary")),
    )(*args)


def _natten_pair_table(rpb):
    h, n_dr, n_dc = rpb.shape
    pad = GRID_W - NA_COLS
    ext = jnp.concatenate([jnp.repeat(rpb[..., :1], pad, axis=-1), rpb,
                           jnp.repeat(rpb[..., -1:], pad + 1, axis=-1)], axis=-1).astype(F32) * LOG2E
    neg = jnp.full((h, 1, LANES), NEG_INF, F32)
    ext = jnp.concatenate([neg, ext, neg], axis=1)
    return pl.pallas_call(
        _pair_table_kernel,
        grid=(h,),
        in_specs=[pl.BlockSpec((1, n_dr + 2, LANES), lambda i: (i, 0, 0))],
        out_specs=pl.BlockSpec((1, n_dr + 1, GRID_W, LANES), lambda i: (i, 0, 0, 0)),
        out_shape=jax.ShapeDtypeStruct((h, n_dr + 1, GRID_W, LANES), F32),
        compiler_params=_cparams(("parallel",)),
    )(ext)


def _pair_table_kernel(ext_ref, o_ref):
    qc = lax.broadcasted_iota(jnp.int32, (GRID_W, LANES), 0)
    ln = lax.broadcasted_iota(jnp.int32, (GRID_W, LANES), 1)
    kc = ln % GRID_W
    cstart = jnp.clip(qc - NA_COLS // 2, 0, GRID_W - NA_COLS)
    in_window = (kc >= cstart) & (kc < cstart + NA_COLS)
    for d in range(o_ref.shape[1]):
        lo = jnp.broadcast_to(ext_ref[0, d:d + 1, :], (GRID_W, LANES))
        hi = jnp.broadcast_to(ext_ref[0, d + 1:d + 2, :], (GRID_W, LANES))
        lo = pltpu.roll(lo, GRID_W + 1, 1, stride=1, stride_axis=0)
        hi = pltpu.roll(hi, 1, 1, stride=1, stride_axis=0)
        o_ref[0, d] = jnp.where(in_window, jnp.where(ln < GRID_W, lo, hi), NEG_INF)


def _mixer_residual(a_ref, wo_ref, x_ref, mod_ref):
    return x_ref[...] + mod_ref[0, 2:3, :] * _dot(a_ref[...], wo_ref[...])


def _ffn_kernel(a_ref, wo_ref, x_ref, mod_ref, g_ref, wg_ref, wu_ref, wd_ref, o_ref):
    x1 = _mixer_residual(a_ref, wo_ref, x_ref, mod_ref)
    h = _norm_mod(x1, g_ref[...], mod_ref[0, 3:4, :], mod_ref[0, 4:5, :]).astype(BF16)
    t = (_silu(_dot(h, wg_ref[...])) * _dot(h, wu_ref[...])).astype(BF16)
    o_ref[...] = x1 + mod_ref[0, 5:6, :] * _dot(t, wd_ref[...])


def dense_ffn(a, wo, x, mod, mod_of_block, g, wg, wu, wd):
    r, d = x.shape
    ff = wg.shape[1]
    tm = ROW_TILE

    def resident(shape):
        return pl.BlockSpec(shape, lambda i: (0, 0), pipeline_mode=pl.Buffered(1))

    return pl.pallas_call(
        _ffn_kernel,
        grid=(r // tm,),
        in_specs=[pl.BlockSpec((tm, d), lambda i: (i, 0)),
                  resident((d, d)),
                  pl.BlockSpec((tm, d), lambda i: (i, 0)),
                  pl.BlockSpec((1, 6, d), lambda i: (mod_of_block(i), 0, 0)),
                  pl.BlockSpec((1, d), lambda i: (0, 0)),
                  resident((d, ff)),
                  resident((d, ff)),
                  resident((ff, d))],
        out_specs=pl.BlockSpec((tm, d), lambda i: (i, 0)),
        out_shape=jax.ShapeDtypeStruct((r, d), F32),
        compiler_params=_cparams(("parallel",)),
    )(a, wo, x, mod, g.reshape(1, d), wg, wu, wd)


def _router_kernel(a_ref, wo_ref, x_ref, mod_ref, g_ref, wr_ref,
                   x1_ref, h_ref, rank_row_ref, gate_row_ref, rank_col_ref, cnt_ref):
    x1 = _mixer_residual(a_ref, wo_ref, x_ref, mod_ref)
    x1_ref[...] = x1
    h = _norm_mod(x1, g_ref[...], mod_ref[0, 3:4, :], mod_ref[0, 4:5, :])
    h_hi = h.astype(BF16)
    h_ref[...] = h_hi
    h_lo = (h - h_hi.astype(F32)).astype(BF16)
    wr = wr_ref[...]
    w_hi = wr.astype(BF16)
    w_lo = (wr - w_hi.astype(F32)).astype(BF16)
    hh = _dot(h_hi, jnp.concatenate([w_hi, w_lo], axis=1))
    logits = hh[:, :LANES] + (_dot(h_lo, w_hi) + hh[:, LANES:])
    tb = logits.shape[0]
    lane = lax.broadcasted_iota(jnp.int32, logits.shape, 1)
    lane_f = lane.astype(F32)
    logits = jnp.where(lane < N_EXPERTS, logits, -jnp.inf)
    m1 = jnp.max(logits, axis=1, keepdims=True)
    i1 = jnp.min(jnp.where(logits == m1, lane_f, float(LANES)), axis=1, keepdims=True)
    rest = jnp.where(lane_f == i1, -jnp.inf, logits)
    m2 = jnp.max(rest, axis=1, keepdims=True)
    i2 = jnp.min(jnp.where(rest == m2, lane_f, float(LANES)), axis=1, keepdims=True)
    e2 = jnp.exp(m2 - m1)
    inv = 1.0 / (1.0 + e2)
    is1 = lane_f == i1
    is2 = lane_f == i2
    gates = jnp.where(is1, inv, 0.0) + jnp.where(is2, e2 * inv, 0.0)
    sel = jnp.where(is1 | is2, 1.0, 0.0)
    sel_b = sel.astype(BF16)
    chunk = 256
    parts = []
    for c in range(tb // chunk):
        ri = lax.broadcasted_iota(jnp.int32, (chunk, tb), 0) + c * chunk
        ci = lax.broadcasted_iota(jnp.int32, (chunk, tb), 1)
        parts.append(_dot(jnp.where(ci < ri, 1.0, 0.0).astype(BF16), sel_b))
    rank = jnp.concatenate(parts, axis=0)
    rank_sel = jnp.where(sel > 0.5, rank, -1.0)
    rank_col_ref[...] = rank_sel
    rank_row_ref[0] = rank_sel.T[:N_EXPERTS]
    gate_row_ref[0] = gates.T[:N_EXPERTS]
    cnt_ref[0] = jnp.broadcast_to(jnp.sum(sel, axis=0, keepdims=True), (8, LANES)).astype(jnp.int32)


def _expert_kernel(cnt_ref, h_ref, rank_row_ref, gate_row_ref, rank_col_ref, wg_ref, wu_ref, wd_ref,
                   x_ref, mod_ref, o_ref):
    b, e = pl.program_id(0), pl.program_id(1)
    tb = h_ref.shape[0]
    n_rows = cnt_ref[b * N_EXPERTS + e]

    @pl.when(e == 0)
    def _():
        o_ref[...] = jnp.zeros_like(o_ref)

    rank_e = rank_row_ref[0, pl.ds(e, 1), :]
    gate_e = gate_row_ref[0, pl.ds(e, 1), :]
    lane = lax.broadcasted_iota(jnp.int32, (tb, LANES), 1)
    rank_c = jnp.sum(jnp.where(lane == e, rank_col_ref[...], 0.0), axis=1, keepdims=True)

    def tile(base, m):
        want = (lax.broadcasted_iota(jnp.int32, (m, tb), 0) + base).astype(F32)
        p = jnp.where(rank_e == want, 1.0, 0.0)
        xr = _dot(p.astype(BF16), h_ref[...]).astype(BF16)
        gate = jnp.sum(p * gate_e, axis=1, keepdims=True)
        t = (_silu(_dot(xr, wg_ref[0])) * _dot(xr, wu_ref[0])).astype(BF16)
        yg = (_dot(t, wd_ref[0]) * gate).astype(BF16)
        want_t = (lax.broadcasted_iota(jnp.int32, (tb, m), 1) + base).astype(F32)
        pt = jnp.where(rank_c == want_t, 1.0, 0.0).astype(BF16)
        o_ref[...] += _dot(pt, yg)

    n_full = n_rows // MOE_SUB
    rest = n_rows - n_full * MOE_SUB

    def full_tile(i, carry):
        tile(pl.multiple_of(i * MOE_SUB, MOE_SUB), MOE_SUB)
        return carry

    lax.fori_loop(0, n_full, full_tile, 0)
    tail_base = pl.multiple_of(n_full * MOE_SUB, MOE_SUB)
    pl.when(rest > MOE_SUB // 2)(lambda: tile(tail_base, MOE_SUB))
    pl.when((rest > 0) & (rest <= MOE_SUB // 2))(lambda: tile(tail_base, MOE_SUB // 2))

    @pl.when(e == N_EXPERTS - 1)
    def _():
        o_ref[...] = x_ref[...] + mod_ref[0, 5:6, :] * o_ref[...]


def moe_ffn(a, wo, x, mod, mod_of_block, g, w_router, wg, wu, wd):
    r, d = x.shape
    ff = wg.shape[2]
    tb = MOE_BLOCK
    nb = r // tb
    wr = jnp.pad(w_router, ((0, 0), (0, LANES - N_EXPERTS)))
    x, h, rank_row, gate_row, rank_col, cnt = pl.pallas_call(
        _router_kernel,
        grid=(nb,),
        in_specs=[pl.BlockSpec((tb, d), lambda i: (i, 0)),
                  pl.BlockSpec((d, d), lambda i: (0, 0)),
                  pl.BlockSpec((tb, d), lambda i: (i, 0)),
                  pl.BlockSpec((1, 6, d), lambda i: (mod_of_block(i), 0, 0)),
                  pl.BlockSpec((1, d), lambda i: (0, 0)),
                  pl.BlockSpec((d, LANES), lambda i: (0, 0))],
        out_specs=[pl.BlockSpec((tb, d), lambda i: (i, 0)),
                   pl.BlockSpec((tb, d), lambda i: (i, 0)),
                   pl.BlockSpec((1, N_EXPERTS, tb), lambda i: (i, 0, 0)),
                   pl.BlockSpec((1, N_EXPERTS, tb), lambda i: (i, 0, 0)),
                   pl.BlockSpec((tb, LANES), lambda i: (i, 0)),
                   pl.BlockSpec((1, 8, LANES), lambda i: (i, 0, 0))],
        out_shape=[jax.ShapeDtypeStruct((r, d), F32),
                   jax.ShapeDtypeStruct((r, d), BF16),
                   jax.ShapeDtypeStruct((nb, N_EXPERTS, tb), F32),
                   jax.ShapeDtypeStruct((nb, N_EXPERTS, tb), F32),
                   jax.ShapeDtypeStruct((r, LANES), F32),
                   jax.ShapeDtypeStruct((nb, 8, LANES), jnp.int32)],
        compiler_params=_cparams(("parallel",)),
    )(a, wo, x, mod, g.reshape(1, d), wr)
    counts = cnt[:, 0, :N_EXPERTS].reshape(nb * N_EXPERTS)

    grid_spec = pltpu.PrefetchScalarGridSpec(
        num_scalar_prefetch=1,
        grid=(nb, N_EXPERTS),
        in_specs=[pl.BlockSpec((tb, d), lambda i, e, c: (i, 0)),
                  pl.BlockSpec((1, N_EXPERTS, tb), lambda i, e, c: (i, 0, 0)),
                  pl.BlockSpec((1, N_EXPERTS, tb), lambda i, e, c: (i, 0, 0)),
                  pl.BlockSpec((tb, LANES), lambda i, e, c: (i, 0)),
                  pl.BlockSpec((1, d, ff), lambda i, e, c: (e, 0, 0)),
                  pl.BlockSpec((1, d, ff), lambda i, e, c: (e, 0, 0)),
                  pl.BlockSpec((1, ff, d), lambda i, e, c: (e, 0, 0)),
                  pl.BlockSpec((tb, d), lambda i, e, c: (i, 0)),
                  pl.BlockSpec((1, 6, d), lambda i, e, c: (mod_of_block(i), 0, 0))],
        out_specs=pl.BlockSpec((tb, d), lambda i, e, c: (i, 0)),
    )
    return pl.pallas_call(
        _expert_kernel,
        grid_spec=grid_spec,
        out_shape=jax.ShapeDtypeStruct((r, d), F32),
        compiler_params=_cparams(("parallel", "arbitrary")),
    )(counts, h, rank_row, gate_row, rank_col, wg, wu, wd, x, mod)


def _diff_lambda_init(layer):
    return 0.8 - 0.6 * math.exp(-0.3 * layer)


def _head_gain_row(q_norm, k_norm, n_q, n_k):
    return jnp.concatenate([jnp.tile(q_norm.astype(F32), n_q // HEAD_DIM),
                            jnp.tile(k_norm.astype(F32), n_k // HEAD_DIM)]).reshape(1, -1)


def kernel(x_prompt, x_sample, cache_k_0, cache_v_0, cache_k_1, cache_v_1, cache_k_2, cache_v_2, cache_k_3, cache_v_3, c, c_ctx, norm1_0, w_ada_0, b_ada_0, w_qkv_0, q_norm_0, k_norm_0, rpb_0, w_o_0, norm2_0, w_gate_0, w_up_0, w_down_0, norm1_1, w_ada_1, b_ada_1, w_qkv_1, q_norm_1, k_norm_1, lam_q1_1, lam_k1_1, lam_q2_1, lam_k2_1, subln_1, w_o_1, norm2_1, w_router_1, w_egate_1, w_eup_1, w_edown_1, norm1_2, w_ada_2, b_ada_2, w_qkv_2, q_norm_2, k_norm_2, sink_2, w_o_2, norm2_2, w_gate_2, w_up_2, w_down_2, norm1_3, w_ada_3, b_ada_3, w_qkv_3, q_norm_3, k_norm_3, w_o_3, norm2_3, w_router_3, w_egate_3, w_eup_3, w_edown_3):
    nbc, tc, d = x_prompt.shape
    nbl, tl, _ = x_sample.shape
    n_past = cache_k_0.shape[1]
    assert d % (2 * LANES) == 0 and tc % ATTN_Q_TILE == 0 and tl % MOE_BLOCK == 0
    assert (nbc * tc) % MOE_BLOCK == 0 and tl // GRID_W >= NA_KEY_ROWS and nbl <= 7

    norm1 = (norm1_0, norm1_1, norm1_2, norm1_3)
    norm2 = (norm2_0, norm2_1, norm2_2, norm2_3)
    ada_p = ((w_ada_0, b_ada_0), (w_ada_1, b_ada_1), (w_ada_2, b_ada_2), (w_ada_3, b_ada_3))
    w_qkv = (w_qkv_0, w_qkv_1, w_qkv_2, w_qkv_3)
    qk_norm = ((q_norm_0, k_norm_0), (q_norm_1, k_norm_1), (q_norm_2, k_norm_2), (q_norm_3, k_norm_3))
    w_out = (w_o_0, w_o_1, w_o_2, w_o_3)
    ffn_p = ((w_gate_0, w_up_0, w_down_0), (w_router_1, w_egate_1, w_eup_1, w_edown_1),
             (w_gate_2, w_up_2, w_down_2), (w_router_3, w_egate_3, w_eup_3, w_edown_3))
    caches = ((cache_k_0, cache_v_0), (cache_k_1, cache_v_1), (cache_k_2, cache_v_2), (cache_k_3, cache_v_3))

    xc = x_prompt.reshape(nbc * tc, d)
    xl = x_sample.reshape(nbl * tl, d)
    cond = jnp.concatenate([c_ctx[None, :], c, jnp.zeros((7 - nbl, d), F32)], axis=0)
    cond16 = jnp.concatenate([cond, cond], axis=0)

    def ctx_mod(blk_rows):
        return lambda i: 0

    def lat_mod(blk_rows):
        per = tl // blk_rows
        return lambda i: 1 + i // per

    lam_rows = jnp.zeros((8, LANES), F32)
    for r_, v_ in enumerate((lam_q1_1, lam_k1_1, lam_q2_1, lam_k2_1)):
        lam_rows = lam_rows.at[r_, :HEAD_DIM].set(v_.astype(F32))
    bias_pairs = _natten_pair_table(rpb_0)
    rope_tabs = _rope_tables(tl)

    states = []
    for l in range(4):
        mixer = l % 4
        mod = ada_mod(cond16, *ada_p[l])
        wq = w_qkv[l].astype(BF16)
        n = wq.shape[1]
        n_q = d
        n_k = d if mixer < 2 else (n - d) // 2
        n_v = n - n_q - n_k
        hn = _head_gain_row(*qk_norm[l], n_q, n_k)
        rope = None if mixer == 0 else rope_tabs

        if l % 2 == 0:
            qkv_c = qkv_proj(xc, mod, ctx_mod(ROW_TILE), norm1[l], wq, hn, None, F32, split=(n_q, n_k, n_v))
            qkv_l = qkv_proj(xl, mod, lat_mod(ROW_TILE), norm1[l], wq, hn, rope, BF16)
        else:
            eg, eu, ed = ffn_p[l][1:]
            n_e, _, ff = eg.shape
            qkv_c, (wd_e,) = qkv_proj(xc, mod, ctx_mod(ROW_TILE), norm1[l], wq, hn, None, F32,
                                      split=(n_q, n_k, n_v), cast=(ed.reshape(n_e * ff, d),))
            qkv_l, (wg_e, wu_e) = qkv_proj(xl, mod, lat_mod(ROW_TILE), norm1[l], wq, hn, rope, BF16,
                                           cast=(eg.reshape(n_e * d, ff), eu.reshape(n_e * d, ff)))
            expert_w = (wg_e.reshape(n_e, d, ff), wu_e.reshape(n_e, d, ff), wd_e.reshape(n_e, ff, d))
        states.append(qkv_c[1:])

        ck, cv = caches[l]
        cache = (ck.astype(BF16).reshape(nbl, n_past, n_k), cv.astype(BF16).reshape(nbl, n_past, n_v))
        common = dict(k_col=n_q, v_col=n_q + n_k, d_model=d)
        ctx = dict(n_sub=1, n_lb=CTX_LANE_BLOCKS, k_col=0, v_col=0, d_model=d)
        lat = dict(n_sub=4, n_lb=1, cache=cache, **common)
        if mixer == 0:
            o_c = attention(qkv_c, nbc, tc, mode="pair", gqa=False, **ctx)
            o_l = attention(qkv_l, nbl, tl, mode="pair", gqa=False, bias_pairs=bias_pairs, **lat)
        elif mixer == 1:
            diff = dict(mode="diff", gqa=False, lam=lam_rows, subln=subln_1.astype(F32).reshape(1, LANES),
                        lam_init=_diff_lambda_init(l))
            o_c = attention(qkv_c, nbc, tc, **diff, **ctx)
            o_l = attention(qkv_l, nbl, tl, **diff, **lat)
        elif mixer == 2:
            sk = sink_2.astype(F32)
            o_c = attention(qkv_c, nbc, tc, mode="pair", gqa=True, sink=sk, **ctx)
            o_l = attention(qkv_l, nbl, tl, mode="pair", gqa=True, band=True, sink=sk, **lat)
        else:
            o_c = attention(qkv_c, nbc, tc, mode="pair", gqa=True, **ctx)
            o_l = attention(qkv_l, nbl, tl, mode="pair", gqa=True, **lat)

        wo = w_out[l].astype(BF16)
        if l % 2 == 0:
            wg, wu, wd = (w.astype(BF16) for w in ffn_p[l])
            xc = dense_ffn(o_c, wo, xc, mod, ctx_mod(ROW_TILE), norm2[l], wg, wu, wd)
            xl = dense_ffn(o_l, wo, xl, mod, lat_mod(ROW_TILE), norm2[l], wg, wu, wd)
        else:
            wr = ffn_p[l][0]
            wg, wu, wd = expert_w
            xc = moe_ffn(o_c, wo, xc, mod, ctx_mod(MOE_BLOCK), norm2[l], wr, wg, wu, wd)
            xl = moe_ffn(o_l, wo, xl, mod, lat_mod(MOE_BLOCK), norm2[l], wr, wg, wu, wd)

    nh = d // HEAD_DIM
    k0, v0 = states[0]
    k1, v1 = states[1]
    k2, v2 = states[2]
    k3, v3 = states[3]
    nkv = k2.shape[1] // HEAD_DIM
    return (xc.reshape(nbc, tc, d), xl.reshape(nbl, tl, d),
            k0.reshape(nbc, tc, nh, HEAD_DIM), v0.reshape(nbc, tc, nh, HEAD_DIM),
            k1.reshape(nbc, tc, nh // 2, 2, HEAD_DIM), v1.reshape(nbc, tc, nh // 2, 2 * HEAD_DIM),
            k2.reshape(nbc, tc, nkv, HEAD_DIM), v2.reshape(nbc, tc, nkv, HEAD_DIM),
            k3.reshape(nbc, tc, nkv, HEAD_DIM), v3.reshape(nbc, tc, nkv, HEAD_DIM))
```

```python
import functools
import math

import numpy as np
import jax
import jax.numpy as jnp
from jax import lax
from jax.experimental import pallas as pl
from jax.experimental.pallas import tpu as pltpu

F32 = jnp.float32
BF16 = jnp.bfloat16

HEAD_DIM = 64
GRID_W = 64
NA_ROWS = 8
NA_COLS = 16
WINDOW = 128
ROPE_BASE = 10000.0
ROPE_PAIRS_AXIS = HEAD_DIM // 4
N_EXPERTS = 8
EPS = 1e-6
NEG_INF = -1e30
QK_SCALE = HEAD_DIM ** -0.5
LOG2E = math.log2(math.e)

LANES = 128
V7X_VMEM_BYTES = 64 * 1024 * 1024
VMEM_LIMIT = V7X_VMEM_BYTES - 8 * 1024 * 1024

ROW_TILE = 512
ATTN_Q_TILE = 256
NA_KEY_ROWS = 12
CTX_LANE_BLOCKS = 8
MOE_BLOCK = 1024
MOE_SUB = 256


def _cparams(sem):
    return pltpu.CompilerParams(dimension_semantics=sem, vmem_limit_bytes=VMEM_LIMIT)


def _silu(x):
    return x * (1.0 / (1.0 + jnp.exp(-x)))


def _norm_mod(x, g, shift, scale):
    ms = jnp.mean(x * x, axis=-1, keepdims=True)
    y = x * lax.rsqrt(ms + EPS) * g
    return y * (1.0 + scale) + shift


def _dot(a, b):
    return jnp.dot(a, b, preferred_element_type=F32)


def _dot_nt(a, b):
    return lax.dot_general(a, b, (((1,), (1,)), ((), ())), preferred_element_type=F32)


def _ada_kernel(c_ref, w_ref, b_ref, o_ref):
    a = _silu(c_ref[...])
    a_hi = a.astype(BF16)
    a_lo = a - a_hi.astype(F32)
    row = lax.broadcasted_iota(jnp.int32, a.shape, 0)
    lhs = jnp.where(row < 8, a_hi.astype(F32), a_lo).astype(BF16)
    w = w_ref[...]
    w_hi = w.astype(BF16)
    w_lo = (w - w_hi.astype(F32)).astype(BF16)
    r = _dot(lhs, w_hi)
    r2 = _dot(a_hi, w_lo)
    o_ref[...] = r[:8] + r[8:] + r2[:8] + b_ref[...]


def ada_mod(cond16, w_ada, b_ada):
    d, n = w_ada.shape
    tn = 1024
    out = pl.pallas_call(
        _ada_kernel,
        grid=(n // tn,),
        in_specs=[pl.BlockSpec((16, d), lambda j: (0, 0)),
                  pl.BlockSpec((d, tn), lambda j: (0, j)),
                  pl.BlockSpec((1, tn), lambda j: (0, j))],
        out_specs=pl.BlockSpec((8, tn), lambda j: (0, j)),
        out_shape=jax.ShapeDtypeStruct((8, n), F32),
        compiler_params=_cparams(("arbitrary",)),
    )(cond16, w_ada, b_ada.reshape(1, n))
    return out.reshape(8, 6, d)


def _qkv_kernel(*refs, rope, n_norm, n_out, n_cast):
    n_in = len(refs) - n_out - 2 * n_cast
    cast_in, refs = refs[n_in:n_in + n_cast], refs[:n_in] + refs[n_in + n_cast:]
    cast_out, refs = refs[len(refs) - n_cast:], refs[:len(refs) - n_cast]
    for src, dst in zip(cast_in, cast_out):
        dst[...] = src[...].astype(dst.dtype)
    o_refs = refs[len(refs) - n_out:]
    if rope:
        x_ref, mod_ref, g_ref, w_ref, seg_ref, hn_ref, cos_ref, sin_ref = refs[:-n_out]
    else:
        x_ref, mod_ref, g_ref, w_ref, seg_ref, hn_ref = refs[:-n_out]
    h = _norm_mod(x_ref[...], g_ref[...], mod_ref[0, 0:1, :], mod_ref[0, 1:2, :]).astype(BF16)
    acc = _dot(h, w_ref[...])
    qk = acc[:, :n_norm]
    sq = (qk * qk).astype(BF16)
    seg = seg_ref[...]
    ssum = jnp.concatenate(
        [_dot(sq[:, c * 2 * LANES:(c + 1) * 2 * LANES], seg) for c in range(n_norm // (2 * LANES))], axis=1)
    y = qk * lax.rsqrt(ssum * (1.0 / HEAD_DIM) + EPS) * hn_ref[...]
    if rope:
        lane = lax.broadcasted_iota(jnp.int32, (acc.shape[0], LANES), 1)
        even = (lane % 2) == 0
        cos, sin = cos_ref[...], sin_ref[...]
        parts = []
        for c in range(n_norm // LANES):
            yc = y[:, c * LANES:(c + 1) * LANES]
            nxt = pltpu.roll(yc, LANES - 1, 1)
            prv = pltpu.roll(yc, 1, 1)
            parts.append(yc * cos + jnp.where(even, nxt, prv) * sin)
        y = jnp.concatenate(parts, axis=1)
    if n_out == 1:
        o_ref, = o_refs
        o_ref[:, :n_norm] = y.astype(o_ref.dtype)
        o_ref[:, n_norm:] = acc[:, n_norm:].astype(o_ref.dtype)
    else:
        q_ref, k_ref, v_ref = o_refs
        n_q = q_ref.shape[1]
        q_ref[...] = y[:, :n_q].astype(q_ref.dtype)
        k_ref[...] = y[:, n_q:].astype(k_ref.dtype)
        v_ref[...] = acc[:, n_norm:].astype(v_ref.dtype)


def _seg_matrix():
    i = np.arange(2 * LANES)
    return jnp.asarray((i[:, None] // HEAD_DIM) == (i[None, :] // HEAD_DIM), BF16)


def qkv_proj(x, mod, mod_of_block, g, w, hn, rope_tabs, out_dtype, split=None, cast=()):
    r, d = x.shape
    n = w.shape[1]
    n_norm = hn.shape[1]
    tm = ROW_TILE
    steps = r // tm
    rope = rope_tabs is not None
    in_specs = [pl.BlockSpec((tm, d), lambda i: (i, 0)),
                pl.BlockSpec((1, 6, d), lambda i: (mod_of_block(i), 0, 0)),
                pl.BlockSpec((1, d), lambda i: (0, 0)),
                pl.BlockSpec((d, n), lambda i: (0, 0), pipeline_mode=pl.Buffered(1)),
                pl.BlockSpec((2 * LANES, 2 * LANES), lambda i: (0, 0)),
                pl.BlockSpec((1, n_norm), lambda i: (0, 0))]
    args = [x, mod, g.reshape(1, d), w, _seg_matrix(), hn]
    if rope:
        cos_t, sin_t = rope_tabs
        nblk = cos_t.shape[0] // tm
        in_specs += [pl.BlockSpec((tm, LANES), lambda i: (i % nblk, 0)),
                     pl.BlockSpec((tm, LANES), lambda i: (i % nblk, 0))]
        args += [cos_t, sin_t]
    widths = (n,) if split is None else split
    out_specs = [pl.BlockSpec((tm, wd), lambda i: (i, 0)) for wd in widths]
    out_shape = [jax.ShapeDtypeStruct((r, wd), out_dtype) for wd in widths]
    for a in cast:
        assert a.shape[0] % (8 * steps) == 0
        share = pl.BlockSpec((a.shape[0] // steps, a.shape[1]), lambda i: (i, 0))
        in_specs.append(share)
        args.append(a)
        out_specs.append(share)
        out_shape.append(jax.ShapeDtypeStruct(a.shape, BF16))
    outs = pl.pallas_call(
        functools.partial(_qkv_kernel, rope=rope, n_norm=n_norm, n_out=len(widths), n_cast=len(cast)),
        grid=(steps,),
        in_specs=in_specs,
        out_specs=out_specs,
        out_shape=out_shape,
        compiler_params=_cparams(("parallel",)),
    )(*args)
    main, casted = outs[:len(widths)], tuple(outs[len(widths):])
    main = main[0] if split is None else tuple(main)
    return (main, casted) if cast else main


def _rope_tables(t):
    pos = np.arange(t)
    row = (pos // GRID_W).astype(np.float32)
    col = (pos % GRID_W).astype(np.float32)
    freqs = jnp.asarray(ROPE_BASE, F32) ** (-jnp.arange(ROPE_PAIRS_AXIS, dtype=F32) / ROPE_PAIRS_AXIS)
    ang = jnp.concatenate([jnp.asarray(row)[:, None] * freqs, jnp.asarray(col)[:, None] * freqs], axis=-1)
    cos = jnp.repeat(jnp.cos(ang), 2, axis=-1)
    sin = jnp.repeat(jnp.sin(ang), 2, axis=-1)
    sign = jnp.asarray(np.where(np.arange(HEAD_DIM) % 2 == 0, -1.0, 1.0), F32)
    reps = LANES // HEAD_DIM
    return jnp.tile(cos, (1, reps)), jnp.tile(sin * sign, (1, reps))


def _in_waves(gens):
    while gens:
        gens = [g for g in gens if next(g, StopIteration) is not StopIteration]
        yield


def _attn_kernel(*refs, n_lb, kmap, mode, n_cache, natten, has_sink, **static):
    it = iter(refs)
    sink_ref = next(it) if has_sink else None
    q_ref, k_ref, v_ref = next(it), next(it), next(it)
    ck_ref, cv_ref = (next(it), next(it)) if n_cache else (None, None)
    bp_ref = next(it) if natten else None
    lam_refs = [next(it), next(it)] if mode == "diff" else []
    o_ref, kf = next(it), next(it)
    v_scr = list(it)
    gens = []
    for jl in range(n_lb):
        q_cols = pl.ds(jl * LANES, LANES)
        k_cols = pl.ds(kmap(jl) * LANES, LANES)
        views = [sink_ref] if has_sink else []
        views += [q_ref.at[:, q_cols], k_ref.at[:, k_cols], v_ref.at[:, k_cols]]
        if n_cache:
            views += [ck_ref.at[:, :, k_cols], cv_ref.at[:, :, k_cols]]
        if natten:
            views.append(bp_ref.at[pl.ds(2 * jl, 2)])
        views += lam_refs + [o_ref.at[:, q_cols], kf.at[jl]] + [v.at[jl] for v in v_scr]
        gens.append(_attn_lane_block(views, pl.program_id(1) * n_lb + jl, pl.program_id(2), mode=mode,
                                     n_cache=n_cache, natten=natten, has_sink=has_sink, **static))
    for _ in _in_waves(gens):
        pass


def _attn_lane_block(refs, j, qb, *, mode, gqa, t_loc, n_cache, win, band, natten, has_sink, lam_init, n_sub):
    it = iter(refs)
    sink_ref = next(it) if has_sink else None
    q_ref, k_ref, v_ref = next(it), next(it), next(it)
    ck_ref = cv_ref = bp_ref = lam_ref = subln_ref = None
    if n_cache:
        ck_ref, cv_ref = next(it), next(it)
    if natten:
        bp_ref = next(it)
    if mode == "diff":
        lam_ref, subln_ref = next(it), next(it)
    o_ref, kf = next(it), next(it)
    v_scr = [next(it), next(it)] if mode == "pair" else [next(it)]

    tq = ATTN_Q_TILE
    nk = t_loc + n_cache
    rows = t_loc // GRID_W
    qrows = tq // GRID_W

    def place(a):
        if not gqa:
            return a
        half = (j // 2) % 2
        lane_half = (lax.broadcasted_iota(jnp.int32, a.shape, 1) >= HEAD_DIM).astype(jnp.int32)
        return jnp.where(lane_half == half, a, pltpu.roll(a, HEAD_DIM, 1))

    def put_v(lo, hi, a):
        if mode == "pair":
            ln = lax.broadcasted_iota(jnp.int32, a.shape, 1)
            v_scr[0][lo:hi, :] = jnp.where(ln < HEAD_DIM, a, 1.0).astype(BF16)
            v_scr[1][lo:hi, :] = jnp.where(ln >= HEAD_DIM, a, 1.0).astype(BF16)
        else:
            v_scr[0][lo:hi, :] = a.astype(BF16)

    @pl.when(qb == 0)
    def _():
        kf[0:t_loc, :] = place(k_ref[...].astype(F32)).astype(BF16)
        put_v(0, t_loc, place(v_ref[...].astype(F32)))
        if n_cache:
            kf[t_loc:nk, :] = place(ck_ref[0].astype(F32)).astype(BF16)
            put_v(t_loc, nk, place(cv_ref[0].astype(F32)))

    lane = lax.broadcasted_iota(jnp.int32, (tq, LANES), 1)
    if mode == "diff":
        lp = lam_ref[...]
        lam = (jnp.exp(jnp.sum(lp[0:1] * lp[1:2], axis=1, keepdims=True))
               - jnp.exp(jnp.sum(lp[2:3] * lp[3:4], axis=1, keepdims=True)) + lam_init)

    def chain(g, hh, q2, sl, k_loc, keep, ws_row, done):
        in_half = (lane < HEAD_DIM) if hh == 0 else (lane >= HEAD_DIM)
        qh = jnp.where(in_half, q2, 0.0).astype(BF16)
        s = _dot_nt(qh, k_loc)
        if n_cache:
            s_c = _dot_nt(qh, kf[t_loc:nk, :])
        yield
        if natten:
            blocks = []
            for a in range(qrows):
                first = ws_row - (g * qrows + a) + NA_ROWS
                blocks.append(jnp.concatenate(
                    [bp_ref[hh, pl.ds(jnp.clip(first + 2 * p, 0, 2 * NA_ROWS - 1), 1)][0]
                     for p in range(win // LANES)], axis=1))
            s = s + jnp.concatenate(blocks, axis=0)
        if keep is not None:
            s = jnp.where(keep, s, NEG_INF)
        m = jnp.max(s, axis=1, keepdims=True)
        if n_cache:
            m = jnp.maximum(m, jnp.max(s_c, axis=1, keepdims=True))
        if has_sink:
            sink = sink_ref[2 * j + hh] * LOG2E
            m = jnp.maximum(m, sink)
        e = jnp.exp2(s - m)
        e_c = jnp.exp2(s_c - m) if n_cache else None
        yield
        if mode == "pair":
            o = _dot(e.astype(BF16), v_scr[hh][sl, :])
            if n_cache:
                o = o + _dot(e_c.astype(BF16), v_scr[hh][t_loc:nk, :])
            yield
            den = pltpu.roll(o, HEAD_DIM, 1)
            if has_sink:
                den = den + jnp.exp2(sink - m)
            done[hh] = o * (1.0 / den)
        else:
            den = jnp.sum(e, axis=1, keepdims=True)
            if n_cache:
                den = den + jnp.sum(e_c, axis=1, keepdims=True)
            done[hh] = (e, e_c, 1.0 / den)

    def sub_tile(t):
        g = qb * n_sub + t
        q2 = q_ref[t * tq:(t + 1) * tq, :].astype(F32) * (QK_SCALE * LOG2E)
        keep = ws_row = None
        if natten:
            ws_row = jnp.clip(g * qrows - NA_ROWS // 2, 0, rows - NA_KEY_ROWS)
            sl = pl.ds(pl.multiple_of(ws_row * GRID_W, GRID_W), win)
            q_row = lax.broadcasted_iota(jnp.int32, (tq, 1), 0) // GRID_W + g * qrows
            r0 = jnp.clip(q_row - NA_ROWS // 2, 0, rows - NA_ROWS)
            k_row = lax.broadcasted_iota(jnp.int32, (1, win), 1) // GRID_W + ws_row
            keep = (k_row >= r0) & (k_row < r0 + NA_ROWS)
        elif band:
            ws = pl.multiple_of(jnp.clip(g * tq - WINDOW, 0, t_loc - win), WINDOW)
            sl = pl.ds(ws, win)
            qpos = g * tq + lax.broadcasted_iota(jnp.int32, (tq, 1), 0)
            kpos = ws + lax.broadcasted_iota(jnp.int32, (1, win), 1)
            keep = jnp.abs(qpos - kpos) <= WINDOW
        else:
            sl = slice(0, win)
        k_loc = kf[sl, :]
        done = [None, None]
        yield from _in_waves([chain(g, hh, q2, sl, k_loc, keep, ws_row, done) for hh in range(2)])
        if mode == "pair":
            out = jnp.where(lane < HEAD_DIM, done[0], done[1])
        else:
            (e0, ec0, w0), (e1, ec1, w1) = done
            c = lam * w1 / w0
            out = _dot((e0 - e1 * c).astype(BF16), v_scr[0][sl, :])
            if n_cache:
                out = out + _dot((ec0 - ec1 * c).astype(BF16), v_scr[0][t_loc:nk, :])
            yield
            out = out * w0
            ms = jnp.mean(out * out, axis=-1, keepdims=True)
            out = out * lax.rsqrt(ms + EPS) * subln_ref[...] * (1.0 - lam_init)
        o_ref[t * tq:(t + 1) * tq, :] = out.astype(o_ref.dtype)

    yield from _in_waves([sub_tile(t) for t in range(n_sub)])


def attention(qkv, nb, t_loc, *, mode, gqa, k_col, v_col, n_sub, n_lb, cache=None, band=False, bias_pairs=None,
              sink=None, lam=None, subln=None, lam_init=0.0, d_model):
    q_arr, k_arr, v_arr = qkv if isinstance(qkv, tuple) else (qkv, qkv, qkv)
    tq = ATTN_Q_TILE * n_sub
    n_qlb = d_model // LANES
    natten = bias_pairs is not None
    n_cache = 0 if cache is None else cache[0].shape[1]
    if natten:
        win = NA_KEY_ROWS * GRID_W
    elif band:
        win = ATTN_Q_TILE + 2 * WINDOW
    else:
        win = t_loc
    if gqa:
        assert n_lb == 1 or n_lb % 4 == 0
        k_lb = max(n_lb // 4, 1)
        kgrp = (lambda jg: jg // 4) if n_lb == 1 else (lambda jg: jg)
        kmap = lambda jl: jl // 4
    else:
        k_lb = n_lb
        kgrp = lambda jg: jg
        kmap = lambda jl: jl
    qw, kw = n_lb * LANES, k_lb * LANES
    assert k_col % kw == 0 and v_col % kw == 0 and n_qlb % n_lb == 0
    kc, vc = k_col // kw, v_col // kw
    nqb = t_loc // tq
    nk = t_loc + n_cache

    in_specs = []
    args = []
    if sink is not None:
        in_specs.append(pl.BlockSpec(memory_space=pltpu.SMEM))
        args.append(sink)
    in_specs += [pl.BlockSpec((tq, qw), lambda b, j, i: (b * nqb + i, j)),
                 pl.BlockSpec((t_loc, kw), lambda b, j, i: (b, kc + kgrp(j))),
                 pl.BlockSpec((t_loc, kw), lambda b, j, i: (b, vc + kgrp(j)))]
    args += [q_arr, k_arr, v_arr]
    if cache is not None:
        in_specs += [pl.BlockSpec((1, n_cache, kw), lambda b, j, i: (b, 0, kgrp(j))),
                     pl.BlockSpec((1, n_cache, kw), lambda b, j, i: (b, 0, kgrp(j)))]
        args += list(cache)
    if natten:
        in_specs.append(pl.BlockSpec((2 * n_lb, 2 * NA_ROWS, GRID_W, LANES), lambda b, j, i: (j, 0, 0, 0)))
        args.append(bias_pairs)
    if mode == "diff":
        in_specs += [pl.BlockSpec((8, LANES), lambda b, j, i: (0, 0)),
                     pl.BlockSpec((1, LANES), lambda b, j, i: (0, 0))]
        args += [lam, subln]
    kern = functools.partial(
        _attn_kernel, n_lb=n_lb, kmap=kmap, mode=mode, gqa=gqa, t_loc=t_loc, n_cache=n_cache, win=win, band=band,
        natten=natten, has_sink=sink is not None, lam_init=lam_init, n_sub=n_sub)
    n_v = 2 if mode == "pair" else 1
    return pl.pallas_call(
        kern,
        grid=(nb, n_qlb // n_lb, nqb),
        in_specs=in_specs,
        out_specs=pl.BlockSpec((tq, qw), lambda b, j, i: (b * nqb + i, j)),
        out_shape=jax.ShapeDtypeStruct((nb * t_loc, d_model), BF16),
        scratch_shapes=[pltpu.VMEM((n_lb, nk, LANES), BF16) for _ in range(1 + n_v)],
        compiler_params=_cparams(("parallel", "arbitrary", "arbitrary")),
    )(*args)


def _natten_pair_table(rpb):
    h, n_dr, n_dc = rpb.shape
    pad = GRID_W - NA_COLS
    ext = jnp.concatenate([jnp.repeat(rpb[..., :1], pad, axis=-1), rpb,
                           jnp.repeat(rpb[..., -1:], pad + 1, axis=-1)], axis=-1).astype(F32) * LOG2E
    neg = jnp.full((h, 1, LANES), NEG_INF, F32)
    ext = jnp.concatenate([neg, ext, neg], axis=1)
    return pl.pallas_call(
        _pair_table_kernel,
        grid=(h,),
        in_specs=[pl.BlockSpec((1, n_dr + 2, LANES), lambda i: (i, 0, 0))],
        out_specs=pl.BlockSpec((1, n_dr + 1, GRID_W, LANES), lambda i: (i, 0, 0, 0)),
        out_shape=jax.ShapeDtypeStruct((h, n_dr + 1, GRID_W, LANES), F32),
        compiler_params=_cparams(("parallel",)),
    )(ext)


def _pair_table_kernel(ext_ref, o_ref):
    qc = lax.broadcasted_iota(jnp.int32, (GRID_W, LANES), 0)
    ln = lax.broadcasted_iota(jnp.int32, (GRID_W, LANES), 1)
    kc = ln % GRID_W
    cstart = jnp.clip(qc - NA_COLS // 2, 0, GRID_W - NA_COLS)
    in_window = (kc >= cstart) & (kc < cstart + NA_COLS)
    for d in range(o_ref.shape[1]):
        lo = jnp.broadcast_to(ext_ref[0, d:d + 1, :], (GRID_W, LANES))
        hi = jnp.broadcast_to(ext_ref[0, d + 1:d + 2, :], (GRID_W, LANES))
        lo = pltpu.roll(lo, GRID_W + 1, 1, stride=1, stride_axis=0)
        hi = pltpu.roll(hi, 1, 1, stride=1, stride_axis=0)
        o_ref[0, d] = jnp.where(in_window, jnp.where(ln < GRID_W, lo, hi), NEG_INF)


def _mixer_residual(a_ref, wo_ref, x_ref, mod_ref):
    return x_ref[...] + mod_ref[0, 2:3, :] * _dot(a_ref[...], wo_ref[...])


def _ffn_kernel(a_ref, wo_ref, x_ref, mod_ref, g_ref, wg_ref, wu_ref, wd_ref, o_ref):
    x1 = _mixer_residual(a_ref, wo_ref, x_ref, mod_ref)
    h = _norm_mod(x1, g_ref[...], mod_ref[0, 3:4, :], mod_ref[0, 4:5, :]).astype(BF16)
    t = (_silu(_dot(h, wg_ref[...])) * _dot(h, wu_ref[...])).astype(BF16)
    o_ref[...] = x1 + mod_ref[0, 5:6, :] * _dot(t, wd_ref[...])


def dense_ffn(a, wo, x, mod, mod_of_block, g, wg, wu, wd):
    r, d = x.shape
    ff = wg.shape[1]
    tm = ROW_TILE

    def resident(shape):
        return pl.BlockSpec(shape, lambda i: (0, 0), pipeline_mode=pl.Buffered(1))

    return pl.pallas_call(
        _ffn_kernel,
        grid=(r // tm,),
        in_specs=[pl.BlockSpec((tm, d), lambda i: (i, 0)),
                  resident((d, d)),
                  pl.BlockSpec((tm, d), lambda i: (i, 0)),
                  pl.BlockSpec((1, 6, d), lambda i: (mod_of_block(i), 0, 0)),
                  pl.BlockSpec((1, d), lambda i: (0, 0)),
                  resident((d, ff)),
                  resident((d, ff)),
                  resident((ff, d))],
        out_specs=pl.BlockSpec((tm, d), lambda i: (i, 0)),
        out_shape=jax.ShapeDtypeStruct((r, d), F32),
        compiler_params=_cparams(("parallel",)),
    )(a, wo, x, mod, g.reshape(1, d), wg, wu, wd)


def _router_kernel(a_ref, wo_ref, x_ref, mod_ref, g_ref, wr_ref,
                   x1_ref, h_ref, rank_row_ref, gate_row_ref, rank_col_ref, cnt_ref):
    x1 = _mixer_residual(a_ref, wo_ref, x_ref, mod_ref)
    x1_ref[...] = x1
    h = _norm_mod(x1, g_ref[...], mod_ref[0, 3:4, :], mod_ref[0, 4:5, :])
    h_hi = h.astype(BF16)
    h_ref[...] = h_hi
    h_lo = (h - h_hi.astype(F32)).astype(BF16)
    wr = wr_ref[...]
    w_hi = wr.astype(BF16)
    w_lo = (wr - w_hi.astype(F32)).astype(BF16)
    hh = _dot(h_hi, jnp.concatenate([w_hi, w_lo], axis=1))
    logits = hh[:, :LANES] + (_dot(h_lo, w_hi) + hh[:, LANES:])
    tb = logits.shape[0]
    lane = lax.broadcasted_iota(jnp.int32, logits.shape, 1)
    lane_f = lane.astype(F32)
    logits = jnp.where(lane < N_EXPERTS, logits, -jnp.inf)
    m1 = jnp.max(logits, axis=1, keepdims=True)
    i1 = jnp.min(jnp.where(logits == m1, lane_f, float(LANES)), axis=1, keepdims=True)
    rest = jnp.where(lane_f == i1, -jnp.inf, logits)
    m2 = jnp.max(rest, axis=1, keepdims=True)
    i2 = jnp.min(jnp.where(rest == m2, lane_f, float(LANES)), axis=1, keepdims=True)
    e2 = jnp.exp(m2 - m1)
    inv = 1.0 / (1.0 + e2)
    is1 = lane_f == i1
    is2 = lane_f == i2
    gates = jnp.where(is1, inv, 0.0) + jnp.where(is2, e2 * inv, 0.0)
    sel = jnp.where(is1 | is2, 1.0, 0.0)
    sel_b = sel.astype(BF16)
    chunk = 256
    parts = []
    for c in range(tb // chunk):
        ri = lax.broadcasted_iota(jnp.int32, (chunk, tb), 0) + c * chunk
        ci = lax.broadcasted_iota(jnp.int32, (chunk, tb), 1)
        parts.append(_dot(jnp.where(ci < ri, 1.0, 0.0).astype(BF16), sel_b))
    rank = jnp.concatenate(parts, axis=0)
    rank_sel = jnp.where(sel > 0.5, rank, -1.0)
    rank_col_ref[...] = rank_sel
    rank_row_ref[0] = rank_sel.T[:N_EXPERTS]
    gate_row_ref[0] = gates.T[:N_EXPERTS]
    cnt_ref[0] = jnp.broadcast_to(jnp.sum(sel, axis=0, keepdims=True), (8, LANES)).astype(jnp.int32)


def _expert_kernel(cnt_ref, h_ref, rank_row_ref, gate_row_ref, rank_col_ref, wg_ref, wu_ref, wd_ref,
                   x_ref, mod_ref, o_ref):
    b, e = pl.program_id(0), pl.program_id(1)
    tb = h_ref.shape[0]
    n_rows = cnt_ref[b * N_EXPERTS + e]

    @pl.when(e == 0)
    def _():
        o_ref[...] = jnp.zeros_like(o_ref)

    rank_e = rank_row_ref[0, pl.ds(e, 1), :]
    gate_e = gate_row_ref[0, pl.ds(e, 1), :]
    lane = lax.broadcasted_iota(jnp.int32, (tb, LANES), 1)
    rank_c = jnp.sum(jnp.where(lane == e, rank_col_ref[...], 0.0), axis=1, keepdims=True)

    def tile(base, m):
        want = (lax.broadcasted_iota(jnp.int32, (m, tb), 0) + base).astype(F32)
        p = jnp.where(rank_e == want, 1.0, 0.0)
        xr = _dot(p.astype(BF16), h_ref[...]).astype(BF16)
        gate = jnp.sum(p * gate_e, axis=1, keepdims=True)
        t = (_silu(_dot(xr, wg_ref[0])) * _dot(xr, wu_ref[0])).astype(BF16)
        yg = (_dot(t, wd_ref[0]) * gate).astype(BF16)
        want_t = (lax.broadcasted_iota(jnp.int32, (tb, m), 1) + base).astype(F32)
        pt = jnp.where(rank_c == want_t, 1.0, 0.0).astype(BF16)
        o_ref[...] += _dot(pt, yg)

    n_full = n_rows // MOE_SUB
    rest = n_rows - n_full * MOE_SUB

    def full_tile(i, carry):
        tile(pl.multiple_of(i * MOE_SUB, MOE_SUB), MOE_SUB)
        return carry

    lax.fori_loop(0, n_full, full_tile, 0)
    tail_base = pl.multiple_of(n_full * MOE_SUB, MOE_SUB)
    pl.when(rest > MOE_SUB // 2)(lambda: tile(tail_base, MOE_SUB))
    pl.when((rest > 0) & (rest <= MOE_SUB // 2))(lambda: tile(tail_base, MOE_SUB // 2))

    @pl.when(e == N_EXPERTS - 1)
    def _():
        o_ref[...] = x_ref[...] + mod_ref[0, 5:6, :] * o_ref[...]


def moe_ffn(a, wo, x, mod, mod_of_block, g, w_router, wg, wu, wd):
    r, d = x.shape
    ff = wg.shape[2]
    tb = MOE_BLOCK
    nb = r // tb
    wr = jnp.pad(w_router, ((0, 0), (0, LANES - N_EXPERTS)))
    x, h, rank_row, gate_row, rank_col, cnt = pl.pallas_call(
        _router_kernel,
        grid=(nb,),
        in_specs=[pl.BlockSpec((tb, d), lambda i: (i, 0)),
                  pl.BlockSpec((d, d), lambda i: (0, 0)),
                  pl.BlockSpec((tb, d), lambda i: (i, 0)),
                  pl.BlockSpec((1, 6, d), lambda i: (mod_of_block(i), 0, 0)),
                  pl.BlockSpec((1, d), lambda i: (0, 0)),
                  pl.BlockSpec((d, LANES), lambda i: (0, 0))],
        out_specs=[pl.BlockSpec((tb, d), lambda i: (i, 0)),
                   pl.BlockSpec((tb, d), lambda i: (i, 0)),
                   pl.BlockSpec((1, N_EXPERTS, tb), lambda i: (i, 0, 0)),
                   pl.BlockSpec((1, N_EXPERTS, tb), lambda i: (i, 0, 0)),
                   pl.BlockSpec((tb, LANES), lambda i: (i, 0)),
                   pl.BlockSpec((1, 8, LANES), lambda i: (i, 0, 0))],
        out_shape=[jax.ShapeDtypeStruct((r, d), F32),
                   jax.ShapeDtypeStruct((r, d), BF16),
                   jax.ShapeDtypeStruct((nb, N_EXPERTS, tb), F32),
                   jax.ShapeDtypeStruct((nb, N_EXPERTS, tb), F32),
                   jax.ShapeDtypeStruct((r, LANES), F32),
                   jax.ShapeDtypeStruct((nb, 8, LANES), jnp.int32)],
        compiler_params=_cparams(("parallel",)),
    )(a, wo, x, mod, g.reshape(1, d), wr)
    counts = cnt[:, 0, :N_EXPERTS].reshape(nb * N_EXPERTS)

    grid_spec = pltpu.PrefetchScalarGridSpec(
        num_scalar_prefetch=1,
        grid=(nb, N_EXPERTS),
        in_specs=[pl.BlockSpec((tb, d), lambda i, e, c: (i, 0)),
                  pl.BlockSpec((1, N_EXPERTS, tb), lambda i, e, c: (i, 0, 0)),
                  pl.BlockSpec((1, N_EXPERTS, tb), lambda i, e, c: (i, 0, 0)),
                  pl.BlockSpec((tb, LANES), lambda i, e, c: (i, 0)),
                  pl.BlockSpec((1, d, ff), lambda i, e, c: (e, 0, 0)),
                  pl.BlockSpec((1, d, ff), lambda i, e, c: (e, 0, 0)),
                  pl.BlockSpec((1, ff, d), lambda i, e, c: (e, 0, 0)),
                  pl.BlockSpec((tb, d), lambda i, e, c: (i, 0)),
                  pl.BlockSpec((1, 6, d), lambda i, e, c: (mod_of_block(i), 0, 0))],
        out_specs=pl.BlockSpec((tb, d), lambda i, e, c: (i, 0)),
    )
    return pl.pallas_call(
        _expert_kernel,
        grid_spec=grid_spec,
        out_shape=jax.ShapeDtypeStruct((r, d), F32),
        compiler_params=_cparams(("parallel", "arbitrary")),
    )(counts, h, rank_row, gate_row, rank_col, wg, wu, wd, x, mod)


def _diff_lambda_init(layer):
    return 0.8 - 0.6 * math.exp(-0.3 * layer)


def _head_gain_row(q_norm, k_norm, n_q, n_k):
    return jnp.concatenate([jnp.tile(q_norm.astype(F32), n_q // HEAD_DIM),
                            jnp.tile(k_norm.astype(F32), n_k // HEAD_DIM)]).reshape(1, -1)


def kernel(x_prompt, x_sample, cache_k_0, cache_v_0, cache_k_1, cache_v_1, cache_k_2, cache_v_2, cache_k_3, cache_v_3, c, c_ctx, norm1_0, w_ada_0, b_ada_0, w_qkv_0, q_norm_0, k_norm_0, rpb_0, w_o_0, norm2_0, w_gate_0, w_up_0, w_down_0, norm1_1, w_ada_1, b_ada_1, w_qkv_1, q_norm_1, k_norm_1, lam_q1_1, lam_k1_1, lam_q2_1, lam_k2_1, subln_1, w_o_1, norm2_1, w_router_1, w_egate_1, w_eup_1, w_edown_1, norm1_2, w_ada_2, b_ada_2, w_qkv_2, q_norm_2, k_norm_2, sink_2, w_o_2, norm2_2, w_gate_2, w_up_2, w_down_2, norm1_3, w_ada_3, b_ada_3, w_qkv_3, q_norm_3, k_norm_3, w_o_3, norm2_3, w_router_3, w_egate_3, w_eup_3, w_edown_3):
    nbc, tc, d = x_prompt.shape
    nbl, tl, _ = x_sample.shape
    n_past = cache_k_0.shape[1]
    assert d % (2 * LANES) == 0 and tc % ATTN_Q_TILE == 0 and tl % MOE_BLOCK == 0
    assert (nbc * tc) % MOE_BLOCK == 0 and tl // GRID_W >= NA_KEY_ROWS and nbl <= 7

    norm1 = (norm1_0, norm1_1, norm1_2, norm1_3)
    norm2 = (norm2_0, norm2_1, norm2_2, norm2_3)
    ada_p = ((w_ada_0, b_ada_0), (w_ada_1, b_ada_1), (w_ada_2, b_ada_2), (w_ada_3, b_ada_3))
    w_qkv = (w_qkv_0, w_qkv_1, w_qkv_2, w_qkv_3)
    qk_norm = ((q_norm_0, k_norm_0), (q_norm_1, k_norm_1), (q_norm_2, k_norm_2), (q_norm_3, k_norm_3))
    w_out = (w_o_0, w_o_1, w_o_2, w_o_3)
    ffn_p = ((w_gate_0, w_up_0, w_down_0), (w_router_1, w_egate_1, w_eup_1, w_edown_1),
             (w_gate_2, w_up_2, w_down_2), (w_router_3, w_egate_3, w_eup_3, w_edown_3))
    caches = ((cache_k_0, cache_v_0), (cache_k_1, cache_v_1), (cache_k_2, cache_v_2), (cache_k_3, cache_v_3))

    xc = x_prompt.reshape(nbc * tc, d)
    xl = x_sample.reshape(nbl * tl, d)
    cond = jnp.concatenate([c_ctx[None, :], c, jnp.zeros((7 - nbl, d), F32)], axis=0)
    cond16 = jnp.concatenate([cond, cond], axis=0)

    def ctx_mod(blk_rows):
        return lambda i: 0

    def lat_mod(blk_rows):
        per = tl // blk_rows
        return lambda i: 1 + i // per

    lam_rows = jnp.zeros((8, LANES), F32)
    for r_, v_ in enumerate((lam_q1_1, lam_k1_1, lam_q2_1, lam_k2_1)):
        lam_rows = lam_rows.at[r_, :HEAD_DIM].set(v_.astype(F32))
    bias_pairs = _natten_pair_table(rpb_0)
    rope_tabs = _rope_tables(tl)

    states = []
    for l in range(4):
        mixer = l % 4
        mod = ada_mod(cond16, *ada_p[l])
        wq = w_qkv[l].astype(BF16)
        n = wq.shape[1]
        n_q = d
        n_k = d if mixer < 2 else (n - d) // 2
        n_v = n - n_q - n_k
        hn = _head_gain_row(*qk_norm[l], n_q, n_k)
        rope = None if mixer == 0 else rope_tabs

        if l % 2 == 0:
            fg, fu, fd = ffn_p[l]
            qkv_c, (wd,) = qkv_proj(xc, mod, ctx_mod(ROW_TILE), norm1[l], wq, hn, None, F32,
                                    split=(n_q, n_k, n_v), cast=(fd,))
            qkv_l, (wg, wu, wo) = qkv_proj(xl, mod, lat_mod(ROW_TILE), norm1[l], wq, hn, rope, BF16,
                                           cast=(fg, fu, w_out[l]))
        else:
            eg, eu, ed = ffn_p[l][1:]
            n_e, _, ff = eg.shape
            qkv_c, (wd_e,) = qkv_proj(xc, mod, ctx_mod(ROW_TILE), norm1[l], wq, hn, None, F32,
                                      split=(n_q, n_k, n_v), cast=(ed.reshape(n_e * ff, d),))
            qkv_l, (wg_e, wu_e, wo) = qkv_proj(xl, mod, lat_mod(ROW_TILE), norm1[l], wq, hn, rope, BF16,
                                               cast=(eg.reshape(n_e * d, ff), eu.reshape(n_e * d, ff), w_out[l]))
            expert_w = (wg_e.reshape(n_e, d, ff), wu_e.reshape(n_e, d, ff), wd_e.reshape(n_e, ff, d))
        states.append(qkv_c[1:])

        ck, cv = caches[l]
        cache = (ck.astype(BF16).reshape(nbl, n_past, n_k), cv.astype(BF16).reshape(nbl, n_past, n_v))
        common = dict(k_col=n_q, v_col=n_q + n_k, d_model=d)
        ctx = dict(n_sub=1, n_lb=CTX_LANE_BLOCKS, k_col=0, v_col=0, d_model=d)
        lat = dict(n_sub=4, n_lb=1, cache=cache, **common)
        if mixer == 0:
            o_c = attention(qkv_c, nbc, tc, mode="pair", gqa=False, **ctx)
            o_l = attention(qkv_l, nbl, tl, mode="pair", gqa=False, bias_pairs=bias_pairs, **lat)
        elif mixer == 1:
            diff = dict(mode="diff", gqa=False, lam=lam_rows, subln=subln_1.astype(F32).reshape(1, LANES),
                        lam_init=_diff_lambda_init(l))
            o_c = attention(qkv_c, nbc, tc, **diff, **ctx)
            o_l = attention(qkv_l, nbl, tl, **diff, **lat)
        elif mixer == 2:
            sk = sink_2.astype(F32)
            o_c = attention(qkv_c, nbc, tc, mode="pair", gqa=True, sink=sk, **ctx)
            o_l = attention(qkv_l, nbl, tl, mode="pair", gqa=True, band=True, sink=sk, **lat)
        else:
            o_c = attention(qkv_c, nbc, tc, mode="pair", gqa=True, **ctx)
            o_l = attention(qkv_l, nbl, tl, mode="pair", gqa=True, **lat)

        if l % 2 == 0:
            xc = dense_ffn(o_c, wo, xc, mod, ctx_mod(ROW_TILE), norm2[l], wg, wu, wd)
            xl = dense_ffn(o_l, wo, xl, mod, lat_mod(ROW_TILE), norm2[l], wg, wu, wd)
        else:
            wr = ffn_p[l][0]
            wg, wu, wd = expert_w
            xc = moe_ffn(o_c, wo, xc, mod, ctx_mod(MOE_BLOCK), norm2[l], wr, wg, wu, wd)
            xl = moe_ffn(o_l, wo, xl, mod, lat_mod(MOE_BLOCK), norm2[l], wr, wg, wu, wd)

    nh = d // HEAD_DIM
    k0, v0 = states[0]
    k1, v1 = states[1]
    k2, v2 = states[2]
    k3, v3 = states[3]
    nkv = k2.shape[1] // HEAD_DIM
    return (xc.reshape(nbc, tc, d), xl.reshape(nbl, tl, d),
            k0.reshape(nbc, tc, nh, HEAD_DIM), v0.reshape(nbc, tc, nh, HEAD_DIM),
            k1.reshape(nbc, tc, nh // 2, 2, HEAD_DIM), v1.reshape(nbc, tc, nh // 2, 2 * HEAD_DIM),
            k2.reshape(nbc, tc, nkv, HEAD_DIM), v2.reshape(nbc, tc, nkv, HEAD_DIM),
            k3.reshape(nbc, tc, nkv, HEAD_DIM), v3.reshape(nbc, tc, nkv, HEAD_DIM))
```
